```python
import jax
import jax.numpy as jnp
from jax import lax
import numpy as np

D_MODEL = 1024
BATCH = 8
SEQ = 8192
DEPTH = 2

CTX_LEN = 256
GRID_W = 64
N_BRANCH = 3
BRANCH_W = 512
MLA_HEADS = 8
MLA_NOPE = 64
MLA_ROPE = 32
MLA_QK = MLA_NOPE + MLA_ROPE
MLA_V = BRANCH_W // MLA_HEADS
MLA_Q_LORA = 256
MLA_KV_LORA = 128
GLA_HEADS = 4
GLA_DK = (D_MODEL // 2) // GLA_HEADS
GLA_DV = BRANCH_W // GLA_HEADS
GLA_GATE_RANK = 16
GLA_GATE_NORMALIZER = 16.0
RET_HEADS = 4
RET_DK = BRANCH_W // RET_HEADS
RET_DV = BRANCH_W // RET_HEADS
D_FF = 2816
CONV_W = 3
CHUNK = 64
Q_BLOCK = 128
ROPE_THETA = 10000.0
RET_THETA = 10000.0
EPS = 1e-6
F32 = jnp.float32

IN_LAYOUT = (
    ('mla_q', MLA_Q_LORA),
    ('mla_kv', MLA_KV_LORA),
    ('mla_kr', MLA_ROPE),
    ('gla_q', GLA_HEADS * GLA_DK),
    ('gla_k', GLA_HEADS * GLA_DK),
    ('gla_v', GLA_HEADS * GLA_DV),
    ('gla_g', GLA_HEADS * GLA_DV),
    ('gla_rf', GLA_GATE_RANK),
    ('gla_rb', GLA_GATE_RANK),
    ('ret_q', RET_HEADS * RET_DK),
    ('ret_k', RET_HEADS * RET_DK),
    ('ret_v', RET_HEADS * RET_DV),
    ('ret_g', RET_HEADS * RET_DV),
    ('gate_mla', D_MODEL),
    ('gate_gla', D_MODEL),
    ('gate_ret', D_MODEL),
)
N_IN = sum(width for _, width in IN_LAYOUT)
CTX_SIDE = ('mla_kv', 'mla_kr', 'gla_k', 'gla_v', 'gla_rf', 'gla_rb', 'ret_k', 'ret_v')
QUERY_SIDE = ('mla_q', 'gla_q', 'gla_g', 'ret_q', 'ret_g', 'gate_mla', 'gate_gla', 'gate_ret')

kernel_name = 'hybrid_mla_gla_retention_dit_block'


def rms_norm(x, w=None):
    x32 = x.astype(F32)
    y = x32 * lax.rsqrt(jnp.mean(x32 * x32, axis=-1, keepdims=True) + EPS)
    if w is not None:
        y = y * w.astype(F32)
    return y.astype(x.dtype)


def modulate(x, shift, scale):
    return x * (1 + scale) + shift


def split_heads(t, n_heads):
    b, s, _ = t.shape
    return t.reshape(b, s, n_heads, -1).transpose(0, 2, 1, 3)


def merge_heads(t):
    b, h, s, d = t.shape
    return t.transpose(0, 2, 1, 3).reshape(b, s, h * d)


def in_proj(a, w, names):
    out, start = {}, 0
    for name, width in IN_LAYOUT:
        if name in names:
            out[name] = a @ w[:, start:start + width]
        start += width
    return out


def rope_tables(pos, dim, theta):
    inv = theta ** (-jnp.arange(dim // 2, dtype=F32) * 2.0 / dim)
    ang = pos.astype(F32)[:, None] * inv[None, :]
    return jnp.cos(ang), jnp.sin(ang)


def retention_tables(pos):
    inv = 1.0 / (RET_THETA ** jnp.linspace(0.0, 1.0, RET_DK // 2, dtype=F32))
    ang = pos.astype(F32)[:, None] * inv[None, :]
    return jnp.cos(ang), jnp.sin(ang)


def rotate_half(x, cos, sin):
    n = x.shape[-1] // 2
    x1, x2 = x[..., :n], x[..., n:]
    return jnp.concatenate([x1 * cos - x2 * sin, x1 * sin + x2 * cos], axis=-1).astype(x.dtype)


def axial_rope(x, tabs):
    cos_r, sin_r, cos_c, sin_c = tabs
    half = x.shape[-1] // 2
    return jnp.concatenate([rotate_half(x[..., :half], cos_r, sin_r),
                            rotate_half(x[..., half:], cos_c, sin_c)], axis=-1)


def mla_rope(t, tabs):
    if tabs is None:
        return t
    return jnp.concatenate([t[..., :MLA_NOPE], axial_rope(t[..., MLA_NOPE:], tabs)], axis=-1)


def mla_queries(cq, q_norm_a, w_qb, q_norm, tabs):
    q = split_heads(rms_norm(cq, q_norm_a) @ w_qb, MLA_HEADS)
    return mla_rope(rms_norm(q, q_norm), tabs)


def mla_keys_values(ckv, kr, kv_norm_a, w_kvb, k_norm, tabs):
    kv = split_heads(rms_norm(ckv, kv_norm_a) @ w_kvb, MLA_HEADS)
    b, h, t, _ = kv.shape
    k_rope = jnp.broadcast_to(kr[:, None], (b, h, t, MLA_ROPE))
    k = rms_norm(jnp.concatenate([kv[..., :MLA_NOPE], k_rope], axis=-1), k_norm)
    return mla_rope(k, tabs), kv[..., MLA_NOPE:]


def attend(q, k, v):
    s = jnp.einsum('bhqd,bhkd->bhqk', q, k, preferred_element_type=F32) * (MLA_QK ** -0.5)
    p = jax.nn.softmax(s, axis=-1).astype(v.dtype)
    return jnp.einsum('bhqk,bhkd->bhqd', p, v)


def blocked_attend(q, k, v):
    b, h, s, d = q.shape
    qb = jnp.moveaxis(q.reshape(b, h, s // Q_BLOCK, Q_BLOCK, d), 2, 0)
    ob = lax.map(lambda qi: attend(qi, k, v), qb)
    return jnp.moveaxis(ob, 0, 2).reshape(b, h, s, v.shape[-1])


def chunk_mask(inclusive):
    idx = jnp.arange(CHUNK)
    return idx[:, None] >= idx[None, :] if inclusive else idx[:, None] > idx[None, :]


def gla_chunk_scan(q, k, v, log_a, s0, inclusive):
    b_, h, t, dk = k.shape
    dv = v.shape[-1]
    n = t // CHUNK
    kc = k.reshape(b_, h, n, CHUNK, dk).astype(F32)
    vc = v.reshape(b_, h, n, CHUNK, dv).astype(F32)
    cum = jnp.cumsum(log_a.reshape(b_, h, n, CHUNK, dk).astype(F32), axis=3)
    cum_last = cum[:, :, :, -1]
    inc = jnp.einsum('bhnjd,bhnjv->bhndv', kc * jnp.exp(cum_last[:, :, :, None] - cum), vc)

    def step(s, xs):
        decay, u = xs
        return decay[..., None] * s + u, s

    s_final, s_start = lax.scan(step, s0, (jnp.moveaxis(jnp.exp(cum_last), 2, 0), jnp.moveaxis(inc, 2, 0)))
    if q is None:
        return None, s_final
    s_start = jnp.moveaxis(s_start, 0, 2)
    q_dec = q.reshape(b_, h, n, CHUNK, dk).astype(F32) * jnp.exp(cum)
    att = jnp.einsum('bhnid,bhnjd->bhnij', q_dec, kc * jnp.exp(-cum))
    att = jnp.where(chunk_mask(inclusive), att, 0.0)
    o = jnp.einsum('bhnij,bhnjv->bhniv', att, vc) + jnp.einsum('bhnid,bhndv->bhniv', q_dec, s_start)
    return o.reshape(b_, h, t, dv).astype(v.dtype), s_final


def ret_chunk_scan(q, k, v, log_g, s0, inclusive):
    b_, h, t, dk = k.shape
    dv = v.shape[-1]
    n = t // CHUNK
    idx = jnp.arange(CHUNK, dtype=F32)
    lg = log_g.astype(F32)
    kc = k.reshape(b_, h, n, CHUNK, dk).astype(F32)
    vc = v.reshape(b_, h, n, CHUNK, dv).astype(F32)
    zeta = jnp.exp((CHUNK - 1 - idx)[None, :] * lg[:, None])
    inc = jnp.einsum('bhnjd,bhnjv->bhndv', kc * zeta[None, :, None, :, None], vc)
    g_chunk = jnp.exp(CHUNK * lg)[None, :, None, None]

    def step(s, u):
        return g_chunk * s + u, s

    s_final, s_start = lax.scan(step, s0, jnp.moveaxis(inc, 2, 0))
    if q is None:
        return None, s_final
    s_start = jnp.moveaxis(s_start, 0, 2)
    mask = chunk_mask(inclusive)
    rel = jnp.where(mask, idx[:, None] - idx[None, :], 0.0)
    dmat = jnp.where(mask[None], jnp.exp(rel[None] * lg[:, None, None]), 0.0)
    xi = jnp.exp((idx + 1.0)[None, :] * lg[:, None])
    qc = q.reshape(b_, h, n, CHUNK, dk).astype(F32)
    att = jnp.einsum('bhnid,bhnjd->bhnij', qc, kc) * dmat[None, :, None]
    o = (jnp.einsum('bhnij,bhnjv->bhniv', att, vc)
         + jnp.einsum('bhnid,bhndv->bhniv', qc, s_start) * xi[None, :, None, :, None])
    return o.reshape(b_, h, t, dv).astype(v.dtype), s_final


def scan_both_directions(chunk_fn, q, k, v, dec_f, dec_b, s_f, s_b, per_token_decay):
    rev = lambda t: None if t is None else jnp.flip(t, axis=2)
    o_f, s_f = chunk_fn(q, k, v, dec_f, s_f, True)
    o_b, s_b = chunk_fn(rev(q), rev(k), rev(v), rev(dec_b) if per_token_decay else dec_b, s_b, False)
    o = None if q is None else o_f + rev(o_b)
    return o, s_f, s_b


def gla_log_decay(r, w2, b):
    return jax.nn.log_sigmoid((r @ w2 + b).astype(F32)) / GLA_GATE_NORMALIZER


def gated_head_norm(o, g, w):
    return merge_heads(rms_norm(o, w)).astype(g.dtype) * jax.nn.silu(g)


def gated_merge(ys, gs, b_gate, w_branch, w_out):
    out = None
    for n in range(N_BRANCH):
        term = jax.nn.sigmoid(gs[n] + b_gate[n]) * (ys[n] @ w_branch[n])
        out = term if out is None else out + term
    return out @ w_out


def token_mixers(a, ac, need_ctx, lat_tabs, ret_lat_tabs, ret_ctx_tabs, w_in, b_gate,
                 mla_q_norm_a, mla_w_qb, mla_kv_norm_a, mla_w_kvb, mla_q_norm, mla_k_norm,
                 gla_w_gk2, gla_b_gk, gla_o_norm, ret_decay, w_branch, w_out):
    bsz = a.shape[0]
    p = in_proj(a, w_in, CTX_SIDE + QUERY_SIDE)
    pc = in_proj(ac, w_in, CTX_SIDE + QUERY_SIDE if need_ctx else CTX_SIDE)

    k_c, v_c = mla_keys_values(pc['mla_kv'], pc['mla_kr'], mla_kv_norm_a, mla_w_kvb, mla_k_norm, None)
    k_l, v_l = mla_keys_values(p['mla_kv'], p['mla_kr'], mla_kv_norm_a, mla_w_kvb, mla_k_norm, lat_tabs)
    q_l = mla_queries(p['mla_q'], mla_q_norm_a, mla_w_qb, mla_q_norm, lat_tabs)
    y_mla = merge_heads(blocked_attend(q_l, jnp.concatenate([k_c, k_l], axis=2),
                                       jnp.concatenate([v_c, v_l], axis=2)))

    def gla_inputs(z, with_q):
        q = split_heads(z['gla_q'], GLA_HEADS) * (GLA_DK ** -0.5) if with_q else None
        k = split_heads(z['gla_k'], GLA_HEADS)
        v = split_heads(z['gla_v'], GLA_HEADS)
        la_f = split_heads(gla_log_decay(z['gla_rf'], gla_w_gk2[0], gla_b_gk[0]), GLA_HEADS)
        la_b = split_heads(gla_log_decay(z['gla_rb'], gla_w_gk2[1], gla_b_gk[1]), GLA_HEADS)
        return q, k, v, la_f, la_b

    zg = jnp.zeros((bsz, GLA_HEADS, GLA_DK, GLA_DV), F32)
    o_gc, sg_f, sg_b = scan_both_directions(gla_chunk_scan, *gla_inputs(pc, need_ctx), zg, zg, True)
    o_gl, _, _ = scan_both_directions(gla_chunk_scan, *gla_inputs(p, True), sg_f, sg_b, True)
    y_gla = gated_head_norm(o_gl, p['gla_g'], gla_o_norm)

    log_g = -jnp.exp(ret_decay.astype(F32))

    def ret_inputs(z, tabs, with_q):
        q = rotate_half(split_heads(z['ret_q'], RET_HEADS), *tabs) if with_q else None
        k = rotate_half(split_heads(z['ret_k'], RET_HEADS), *tabs) * (RET_DK ** -0.5)
        v = split_heads(z['ret_v'], RET_HEADS)
        return q, k, v

    zr = jnp.zeros((bsz, RET_HEADS, RET_DK, RET_DV), F32)
    o_rc, sr_f, sr_b = scan_both_directions(ret_chunk_scan, *ret_inputs(pc, ret_ctx_tabs, need_ctx),
                                            log_g[0], log_g[1], zr, zr, False)
    o_rl, _, _ = scan_both_directions(ret_chunk_scan, *ret_inputs(p, ret_lat_tabs, True),
                                      log_g[0], log_g[1], sr_f, sr_b, False)
    y_ret = gated_head_norm(o_rl, p['ret_g'], None)

    y = gated_merge((y_mla, y_gla, y_ret), (p['gate_mla'], p['gate_gla'], p['gate_ret']),
                    b_gate, w_branch, w_out)
    if not need_ctx:
        return y, None
    q_c = mla_queries(pc['mla_q'], mla_q_norm_a, mla_w_qb, mla_q_norm, None)
    y_c = gated_merge((merge_heads(attend(q_c, k_c, v_c)),
                       gated_head_norm(o_gc, pc['gla_g'], gla_o_norm),
                       gated_head_norm(o_rc, pc['ret_g'], None)),
                      (pc['gate_mla'], pc['gate_gla'], pc['gate_ret']), b_gate, w_branch, w_out)
    return y, y_c


def conv_ffn(a, w_in, w_dw, b_dw, w_out):
    gate = a @ w_in[:, :D_FF]
    up = a @ w_in[:, D_FF:]
    gate = lax.conv_general_dilated(gate, w_dw[:, None, :], window_strides=(1,),
                                    padding=[(CONV_W // 2, CONV_W // 2)],
                                    dimension_numbers=('NWC', 'WIO', 'NWC'),
                                    feature_group_count=D_FF) + b_dw
    return (jax.nn.gelu(gate) * up) @ w_out


def _fwd_setup_inputs(seed: int = 0) -> dict:
    key = jax.random.key(seed)
    ks = iter(jax.random.split(key, 32))

    def nrm(shape, scale):
        return scale * jax.random.normal(next(ks), shape, F32)

    def gain(shape):
        return 1.0 + nrm(shape, 0.02)

    L, D = DEPTH, D_MODEL
    ret_base = jnp.log(-jnp.log1p(-(2.0 ** (-5.0 - jnp.arange(RET_HEADS, dtype=F32)))))
    return {
        'x': nrm((BATCH, SEQ, D), 1.0),
        'c': nrm((BATCH, D), 1.0),
        'ctx': nrm((BATCH, CTX_LEN, D), 1.0),
        'c_ctx': nrm((D,), 1.0),
        'w_ada': nrm((L, D, 6 * D), 0.5 * D ** -0.5),
        'b_ada': nrm((L, 6 * D), 0.02),
        'norm1_w': gain((L, D)),
        'norm2_w': gain((L, D)),
        'w_in': nrm((L, D, N_IN), D ** -0.5),
        'b_gate': nrm((L, N_BRANCH, D), 0.02),
        'mla_q_norm_a': gain((L, MLA_Q_LORA)),
        'mla_w_qb': nrm((L, MLA_Q_LORA, MLA_HEADS * MLA_QK), MLA_Q_LORA ** -0.5),
        'mla_kv_norm_a': gain((L, MLA_KV_LORA)),
        'mla_w_kvb': nrm((L, MLA_KV_LORA, MLA_HEADS * (MLA_NOPE + MLA_V)), MLA_KV_LORA ** -0.5),
        'mla_q_norm': gain((L, MLA_QK)),
        'mla_k_norm': gain((L, MLA_QK)),
        'gla_w_gk2': nrm((L, 2, GLA_GATE_RANK, GLA_HEADS * GLA_DK), GLA_GATE_RANK ** -0.5),
        'gla_b_gk': nrm((L, 2, GLA_HEADS * GLA_DK), 0.1),
        'gla_o_norm': gain((L, GLA_DV)),
        'ret_decay': ret_base + nrm((L, 2, RET_HEADS), 0.01),
        'w_branch': nrm((L, N_BRANCH, BRANCH_W, D), BRANCH_W ** -0.5),
        'w_out': nrm((L, D, D), D ** -0.5),
        'w_ffn_in': nrm((L, D, 2 * D_FF), D ** -0.5),
        'w_dw': nrm((L, CONV_W, D_FF), CONV_W ** -0.5),
        'b_dw': nrm((L, D_FF), 0.02),
        'w_ffn_out': nrm((L, D_FF, D), D_FF ** -0.5),
    }


def _fwd_reference(x, c, ctx, c_ctx, w_ada, b_ada, norm1_w, norm2_w, w_in, b_gate,
              mla_q_norm_a, mla_w_qb, mla_kv_norm_a, mla_w_kvb, mla_q_norm, mla_k_norm,
              gla_w_gk2, gla_b_gk, gla_o_norm, ret_decay, w_branch, w_out,
              w_ffn_in, w_dw, b_dw, w_ffn_out):
    seq = x.shape[1]
    ctx_len = ctx.shape[1]
    rows = seq // GRID_W
    row_pos = jnp.repeat(jnp.arange(rows), GRID_W)
    col_pos = jnp.tile(jnp.arange(GRID_W), rows)
    cos_r, sin_r = rope_tables(row_pos, MLA_ROPE // 2, ROPE_THETA)
    cos_c, sin_c = rope_tables(col_pos, MLA_ROPE // 2, ROPE_THETA)
    lat_tabs = (cos_r, sin_r, cos_c, sin_c)
    ret_ctx_tabs = retention_tables(jnp.arange(ctx_len))
    ret_lat_tabs = retention_tables(ctx_len + jnp.arange(seq))
    cond = jax.nn.silu(c)
    cond_c = jax.nn.silu(c_ctx)
    h, hc = x, ctx
    for l in range(DEPTH):
        need_ctx = l < DEPTH - 1
        mod = jnp.split((cond @ w_ada[l] + b_ada[l])[:, None, :], 6, axis=-1)
        mod_c = jnp.split(cond_c @ w_ada[l] + b_ada[l], 6, axis=-1)
        a = modulate(rms_norm(h, norm1_w[l]), mod[0], mod[1])
        ac = modulate(rms_norm(hc, norm1_w[l]), mod_c[0], mod_c[1])
        y, y_c = token_mixers(a, ac, need_ctx, lat_tabs, ret_lat_tabs, ret_ctx_tabs, w_in[l], b_gate[l],
                              mla_q_norm_a[l], mla_w_qb[l], mla_kv_norm_a[l], mla_w_kvb[l],
                              mla_q_norm[l], mla_k_norm[l], gla_w_gk2[l], gla_b_gk[l], gla_o_norm[l],
                              ret_decay[l], w_branch[l], w_out[l])
        h = h + mod[2] * y
        h = h + mod[5] * conv_ffn(modulate(rms_norm(h, norm2_w[l]), mod[3], mod[4]),
                                  w_ffn_in[l], w_dw[l], b_dw[l], w_ffn_out[l])
        if need_ctx:
            hc = hc + mod_c[2] * y_c
            hc = hc + mod_c[5] * conv_ffn(modulate(rms_norm(hc, norm2_w[l]), mod_c[3], mod_c[4]),
                                          w_ffn_in[l], w_dw[l], b_dw[l], w_ffn_out[l])
    return h


import jax as _jax
import jax.numpy as _jnp

TWIN_FORMAT = 'train_step'
FWD_PARAMS = ['x', 'c', 'ctx', 'c_ctx', 'w_ada', 'b_ada', 'norm1_w', 'norm2_w', 'w_in', 'b_gate', 'mla_q_norm_a', 'mla_w_qb', 'mla_kv_norm_a', 'mla_w_kvb', 'mla_q_norm', 'mla_k_norm', 'gla_w_gk2', 'gla_b_gk', 'gla_o_norm', 'ret_decay', 'w_branch', 'w_out', 'w_ffn_in', 'w_dw', 'b_dw', 'w_ffn_out']
TWIN_WEIGHTS = ['c_ctx', 'w_ada', 'b_ada', 'norm1_w', 'norm2_w', 'w_in', 'b_gate', 'mla_q_norm_a', 'mla_w_qb', 'mla_kv_norm_a', 'mla_w_kvb', 'mla_q_norm', 'mla_k_norm', 'gla_w_gk2', 'gla_b_gk', 'gla_o_norm', 'ret_decay', 'w_branch', 'w_out', 'w_ffn_in', 'w_dw', 'b_dw', 'w_ffn_out']
TWIN_DIFF_INPUT = 'x'
TWIN_INPUTS = ['x', 'c', 'ctx', 'c_ctx', 'w_ada', 'b_ada', 'norm1_w', 'norm2_w', 'w_in', 'b_gate', 'mla_q_norm_a', 'mla_w_qb', 'mla_kv_norm_a', 'mla_w_kvb', 'mla_q_norm', 'mla_k_norm', 'gla_w_gk2', 'gla_b_gk', 'gla_o_norm', 'ret_decay', 'w_branch', 'w_out', 'w_ffn_in', 'w_dw', 'b_dw', 'w_ffn_out', 'loss_target', 'm_c_ctx', 'm_w_ada', 'm_b_ada', 'm_norm1_w', 'm_norm2_w', 'm_w_in', 'm_b_gate', 'm_mla_q_norm_a', 'm_mla_w_qb', 'm_mla_kv_norm_a', 'm_mla_w_kvb', 'm_mla_q_norm', 'm_mla_k_norm', 'm_gla_w_gk2', 'm_gla_b_gk', 'm_gla_o_norm', 'm_ret_decay', 'm_w_branch', 'm_w_out', 'm_w_ffn_in', 'm_w_dw', 'm_b_dw', 'm_w_ffn_out', 'v_c_ctx', 'v_w_ada', 'v_b_ada', 'v_norm1_w', 'v_norm2_w', 'v_w_in', 'v_b_gate', 'v_mla_q_norm_a', 'v_mla_w_qb', 'v_mla_kv_norm_a', 'v_mla_w_kvb', 'v_mla_q_norm', 'v_mla_k_norm', 'v_gla_w_gk2', 'v_gla_b_gk', 'v_gla_o_norm', 'v_ret_decay', 'v_w_branch', 'v_w_out', 'v_w_ffn_in', 'v_w_dw', 'v_b_dw', 'v_w_ffn_out']
TWIN_OUTPUTS = ['loss', 'grad_x', 'grad_c_ctx', 'grad_w_ada', 'grad_b_ada', 'grad_norm1_w', 'grad_norm2_w', 'grad_w_in', 'grad_b_gate', 'grad_mla_q_norm_a', 'grad_mla_w_qb', 'grad_mla_kv_norm_a', 'grad_mla_w_kvb', 'grad_mla_q_norm', 'grad_mla_k_norm', 'grad_gla_w_gk2', 'grad_gla_b_gk', 'grad_gla_o_norm', 'grad_ret_decay', 'grad_w_branch', 'grad_w_out', 'grad_w_ffn_in', 'grad_w_dw', 'grad_b_dw', 'grad_w_ffn_out', 'delta_c_ctx', 'delta_w_ada', 'delta_b_ada', 'delta_norm1_w', 'delta_norm2_w', 'delta_w_in', 'delta_b_gate', 'delta_mla_q_norm_a', 'delta_mla_w_qb', 'delta_mla_kv_norm_a', 'delta_mla_w_kvb', 'delta_mla_q_norm', 'delta_mla_k_norm', 'delta_gla_w_gk2', 'delta_gla_b_gk', 'delta_gla_o_norm', 'delta_ret_decay', 'delta_w_branch', 'delta_w_out', 'delta_w_ffn_in', 'delta_w_dw', 'delta_b_dw', 'delta_w_ffn_out', 'new_m_c_ctx', 'new_m_w_ada', 'new_m_b_ada', 'new_m_norm1_w', 'new_m_norm2_w', 'new_m_w_in', 'new_m_b_gate', 'new_m_mla_q_norm_a', 'new_m_mla_w_qb', 'new_m_mla_kv_norm_a', 'new_m_mla_w_kvb', 'new_m_mla_q_norm', 'new_m_mla_k_norm', 'new_m_gla_w_gk2', 'new_m_gla_b_gk', 'new_m_gla_o_norm', 'new_m_ret_decay', 'new_m_w_branch', 'new_m_w_out', 'new_m_w_ffn_in', 'new_m_w_dw', 'new_m_b_dw', 'new_m_w_ffn_out', 'new_v_c_ctx', 'new_v_w_ada', 'new_v_b_ada', 'new_v_norm1_w', 'new_v_norm2_w', 'new_v_w_in', 'new_v_b_gate', 'new_v_mla_q_norm_a', 'new_v_mla_w_qb', 'new_v_mla_kv_norm_a', 'new_v_mla_w_kvb', 'new_v_mla_q_norm', 'new_v_mla_k_norm', 'new_v_gla_w_gk2', 'new_v_gla_b_gk', 'new_v_gla_o_norm', 'new_v_ret_decay', 'new_v_w_branch', 'new_v_w_out', 'new_v_w_ffn_in', 'new_v_w_dw', 'new_v_b_dw', 'new_v_w_ffn_out']
TWIN_LEAF_KINDS = {'loss': 'loss', 'grad_x': 'grad_x', 'grad_c_ctx': 'grad_w', 'grad_w_ada': 'grad_w', 'grad_b_ada': 'grad_w', 'grad_norm1_w': 'grad_w', 'grad_norm2_w': 'grad_w', 'grad_w_in': 'grad_w', 'grad_b_gate': 'grad_w', 'grad_mla_q_norm_a': 'grad_w', 'grad_mla_w_qb': 'grad_w', 'grad_mla_kv_norm_a': 'grad_w', 'grad_mla_w_kvb': 'grad_w', 'grad_mla_q_norm': 'grad_w', 'grad_mla_k_norm': 'grad_w', 'grad_gla_w_gk2': 'grad_w', 'grad_gla_b_gk': 'grad_w', 'grad_gla_o_norm': 'grad_w', 'grad_ret_decay': 'grad_w', 'grad_w_branch': 'grad_w', 'grad_w_out': 'grad_w', 'grad_w_ffn_in': 'grad_w', 'grad_w_dw': 'grad_w', 'grad_b_dw': 'grad_w', 'grad_w_ffn_out': 'grad_w', 'delta_c_ctx': 'delta_w', 'delta_w_ada': 'delta_w', 'delta_b_ada': 'delta_w', 'delta_norm1_w': 'delta_w', 'delta_norm2_w': 'delta_w', 'delta_w_in': 'delta_w', 'delta_b_gate': 'delta_w', 'delta_mla_q_norm_a': 'delta_w', 'delta_mla_w_qb': 'delta_w', 'delta_mla_kv_norm_a': 'delta_w', 'delta_mla_w_kvb': 'delta_w', 'delta_mla_q_norm': 'delta_w', 'delta_mla_k_norm': 'delta_w', 'delta_gla_w_gk2': 'delta_w', 'delta_gla_b_gk': 'delta_w', 'delta_gla_o_norm': 'delta_w', 'delta_ret_decay': 'delta_w', 'delta_w_branch': 'delta_w', 'delta_w_out': 'delta_w', 'delta_w_ffn_in': 'delta_w', 'delta_w_dw': 'delta_w', 'delta_b_dw': 'delta_w', 'delta_w_ffn_out': 'delta_w', 'new_m_c_ctx': 'new_m', 'new_m_w_ada': 'new_m', 'new_m_b_ada': 'new_m', 'new_m_norm1_w': 'new_m', 'new_m_norm2_w': 'new_m', 'new_m_w_in': 'new_m', 'new_m_b_gate': 'new_m', 'new_m_mla_q_norm_a': 'new_m', 'new_m_mla_w_qb': 'new_m', 'new_m_mla_kv_norm_a': 'new_m', 'new_m_mla_w_kvb': 'new_m', 'new_m_mla_q_norm': 'new_m', 'new_m_mla_k_norm': 'new_m', 'new_m_gla_w_gk2': 'new_m', 'new_m_gla_b_gk': 'new_m', 'new_m_gla_o_norm': 'new_m', 'new_m_ret_decay': 'new_m', 'new_m_w_branch': 'new_m', 'new_m_w_out': 'new_m', 'new_m_w_ffn_in': 'new_m', 'new_m_w_dw': 'new_m', 'new_m_b_dw': 'new_m', 'new_m_w_ffn_out': 'new_m', 'new_v_c_ctx': 'new_v', 'new_v_w_ada': 'new_v', 'new_v_b_ada': 'new_v', 'new_v_norm1_w': 'new_v', 'new_v_norm2_w': 'new_v', 'new_v_w_in': 'new_v', 'new_v_b_gate': 'new_v', 'new_v_mla_q_norm_a': 'new_v', 'new_v_mla_w_qb': 'new_v', 'new_v_mla_kv_norm_a': 'new_v', 'new_v_mla_w_kvb': 'new_v', 'new_v_mla_q_norm': 'new_v', 'new_v_mla_k_norm': 'new_v', 'new_v_gla_w_gk2': 'new_v', 'new_v_gla_b_gk': 'new_v', 'new_v_gla_o_norm': 'new_v', 'new_v_ret_decay': 'new_v', 'new_v_w_branch': 'new_v', 'new_v_w_out': 'new_v', 'new_v_w_ffn_in': 'new_v', 'new_v_w_dw': 'new_v', 'new_v_b_dw': 'new_v', 'new_v_w_ffn_out': 'new_v'}


def _forward(args):
    return _fwd_reference(*[args[k] for k in FWD_PARAMS])


def _output_shape():
    def fwd():
        inp = _fwd_setup_inputs(0)
        return _fwd_reference(*[inp[k] for k in FWD_PARAMS])
    out = _jax.eval_shape(fwd)
    return out.shape, out.dtype

N_MICROBATCH = 1
ADAM_LR = 0.001
ADAM_B1 = 0.9
ADAM_B2 = 0.999
ADAM_EPS = 1e-08
ADAM_WD = 0.01
ADAM_STEP = 10
PER_EXAMPLE_BATCH_AXIS = {'x': 0, 'c': 0, 'ctx': 0, 'loss_target': 0}
SHARED_INPUTS = []
_WEIGHT_DTYPES = {'c_ctx': _jnp.float32, 'w_ada': _jnp.float32, 'b_ada': _jnp.float32, 'norm1_w': _jnp.float32, 'norm2_w': _jnp.float32, 'w_in': _jnp.float32, 'b_gate': _jnp.float32, 'mla_q_norm_a': _jnp.float32, 'mla_w_qb': _jnp.float32, 'mla_kv_norm_a': _jnp.float32, 'mla_w_kvb': _jnp.float32, 'mla_q_norm': _jnp.float32, 'mla_k_norm': _jnp.float32, 'gla_w_gk2': _jnp.float32, 'gla_b_gk': _jnp.float32, 'gla_o_norm': _jnp.float32, 'ret_decay': _jnp.float32, 'w_branch': _jnp.float32, 'w_out': _jnp.float32, 'w_ffn_in': _jnp.float32, 'w_dw': _jnp.float32, 'b_dw': _jnp.float32, 'w_ffn_out': _jnp.float32}
MOMENT_SCALE = {'c_ctx': 5.205665e-02, 'w_ada': 1.464071e+00, 'b_ada': 3.983660e+00, 'norm1_w': 1.615487e+00, 'norm2_w': 7.483640e+00, 'w_in': 7.947803e-02, 'b_gate': 2.013054e-01, 'mla_q_norm_a': 1.993333e-02, 'mla_w_qb': 1.137425e-02, 'mla_kv_norm_a': 1.024352e+00, 'mla_w_kvb': 1.061072e-01, 'mla_q_norm': 7.184980e-02, 'mla_k_norm': 7.210478e-02, 'gla_w_gk2': 8.936340e-03, 'gla_b_gk': 2.345496e-02, 'gla_o_norm': 6.088245e+00, 'ret_decay': 6.443602e-01, 'w_branch': 6.124153e-02, 'w_out': 1.031840e-01, 'w_ffn_in': 1.365099e-01, 'w_dw': 8.580770e-01, 'b_dw': 1.009862e+00, 'w_ffn_out': 1.451239e-01}


def _to_microbatches(a, axis):
    t = _jnp.moveaxis(a, axis, 0)
    t = t.reshape((N_MICROBATCH, t.shape[0] // N_MICROBATCH) + t.shape[1:])
    return _jnp.moveaxis(t, 1, axis + 1)


def setup_inputs(seed: int = 0) -> dict:
    inp = _fwd_setup_inputs(seed)
    key = _jax.random.fold_in(_jax.random.key(seed), 7919)
    shape, _ = _output_shape()
    out = dict(inp)
    out["loss_target"] = _jax.random.normal(_jax.random.fold_in(key, 0), shape, _jnp.float32)
    for i, name in enumerate(TWIN_WEIGHTS):
        w = inp[name].astype(_jnp.float32)
        if MOMENT_SCALE is None:
            s = _jnp.sqrt(_jnp.mean(_jnp.square(w)) + 1e-30)
        else:
            s = MOMENT_SCALE[name]
        km, kv = _jax.random.split(_jax.random.fold_in(key, i + 1))
        out[name] = w
        out["m_" + name] = s * _jax.random.normal(km, w.shape, _jnp.float32)
        out["v_" + name] = (s * s) * _jax.random.uniform(kv, w.shape, _jnp.float32, 0.5, 1.5)
    if N_MICROBATCH > 1:
        for name, axis in PER_EXAMPLE_BATCH_AXIS.items():
            out[name] = _to_microbatches(out[name], axis)
    return {'x': out['x'], 'c': out['c'], 'ctx': out['ctx'], 'c_ctx': out['c_ctx'], 'w_ada': out['w_ada'], 'b_ada': out['b_ada'], 'norm1_w': out['norm1_w'], 'norm2_w': out['norm2_w'], 'w_in': out['w_in'], 'b_gate': out['b_gate'], 'mla_q_norm_a': out['mla_q_norm_a'], 'mla_w_qb': out['mla_w_qb'], 'mla_kv_norm_a': out['mla_kv_norm_a'], 'mla_w_kvb': out['mla_w_kvb'], 'mla_q_norm': out['mla_q_norm'], 'mla_k_norm': out['mla_k_norm'], 'gla_w_gk2': out['gla_w_gk2'], 'gla_b_gk': out['gla_b_gk'], 'gla_o_norm': out['gla_o_norm'], 'ret_decay': out['ret_decay'], 'w_branch': out['w_branch'], 'w_out': out['w_out'], 'w_ffn_in': out['w_ffn_in'], 'w_dw': out['w_dw'], 'b_dw': out['b_dw'], 'w_ffn_out': out['w_ffn_out'], 'loss_target': out['loss_target'], 'm_c_ctx': out['m_c_ctx'], 'm_w_ada': out['m_w_ada'], 'm_b_ada': out['m_b_ada'], 'm_norm1_w': out['m_norm1_w'], 'm_norm2_w': out['m_norm2_w'], 'm_w_in': out['m_w_in'], 'm_b_gate': out['m_b_gate'], 'm_mla_q_norm_a': out['m_mla_q_norm_a'], 'm_mla_w_qb': out['m_mla_w_qb'], 'm_mla_kv_norm_a': out['m_mla_kv_norm_a'], 'm_mla_w_kvb': out['m_mla_w_kvb'], 'm_mla_q_norm': out['m_mla_q_norm'], 'm_mla_k_norm': out['m_mla_k_norm'], 'm_gla_w_gk2': out['m_gla_w_gk2'], 'm_gla_b_gk': out['m_gla_b_gk'], 'm_gla_o_norm': out['m_gla_o_norm'], 'm_ret_decay': out['m_ret_decay'], 'm_w_branch': out['m_w_branch'], 'm_w_out': out['m_w_out'], 'm_w_ffn_in': out['m_w_ffn_in'], 'm_w_dw': out['m_w_dw'], 'm_b_dw': out['m_b_dw'], 'm_w_ffn_out': out['m_w_ffn_out'], 'v_c_ctx': out['v_c_ctx'], 'v_w_ada': out['v_w_ada'], 'v_b_ada': out['v_b_ada'], 'v_norm1_w': out['v_norm1_w'], 'v_norm2_w': out['v_norm2_w'], 'v_w_in': out['v_w_in'], 'v_b_gate': out['v_b_gate'], 'v_mla_q_norm_a': out['v_mla_q_norm_a'], 'v_mla_w_qb': out['v_mla_w_qb'], 'v_mla_kv_norm_a': out['v_mla_kv_norm_a'], 'v_mla_w_kvb': out['v_mla_w_kvb'], 'v_mla_q_norm': out['v_mla_q_norm'], 'v_mla_k_norm': out['v_mla_k_norm'], 'v_gla_w_gk2': out['v_gla_w_gk2'], 'v_gla_b_gk': out['v_gla_b_gk'], 'v_gla_o_norm': out['v_gla_o_norm'], 'v_ret_decay': out['v_ret_decay'], 'v_w_branch': out['v_w_branch'], 'v_w_out': out['v_w_out'], 'v_w_ffn_in': out['v_w_ffn_in'], 'v_w_dw': out['v_w_dw'], 'v_b_dw': out['v_b_dw'], 'v_w_ffn_out': out['v_w_ffn_out']}


def _loss(weights, diff, rest, loss_target):
    with _jax.named_scope("forward"):
        args = {**rest, TWIN_DIFF_INPUT: diff, **{k: w.astype(_WEIGHT_DTYPES[k]) for k, w in weights.items()}}
        y = _forward(args)
    with _jax.named_scope("loss_head"):
        err = _jnp.square(y.astype(_jnp.float32) - loss_target)
        return 0.5 * _jnp.sum(_jnp.mean(err, axis=-1)) if err.ndim else 0.5 * err


def _adamw(w, g, m, v):
    m = ADAM_B1 * m + (1.0 - ADAM_B1) * g
    v = ADAM_B2 * v + (1.0 - ADAM_B2) * _jnp.square(g)
    m_hat = m / (1.0 - ADAM_B1 ** ADAM_STEP)
    v_hat = v / (1.0 - ADAM_B2 ** ADAM_STEP)
    delta = -ADAM_LR * (m_hat / (_jnp.sqrt(v_hat) + ADAM_EPS) + ADAM_WD * w)
    return delta, m, v


def reference(x, c, ctx, c_ctx, w_ada, b_ada, norm1_w, norm2_w, w_in, b_gate, mla_q_norm_a, mla_w_qb, mla_kv_norm_a, mla_w_kvb, mla_q_norm, mla_k_norm, gla_w_gk2, gla_b_gk, gla_o_norm, ret_decay, w_branch, w_out, w_ffn_in, w_dw, b_dw, w_ffn_out, loss_target, m_c_ctx, m_w_ada, m_b_ada, m_norm1_w, m_norm2_w, m_w_in, m_b_gate, m_mla_q_norm_a, m_mla_w_qb, m_mla_kv_norm_a, m_mla_w_kvb, m_mla_q_norm, m_mla_k_norm, m_gla_w_gk2, m_gla_b_gk, m_gla_o_norm, m_ret_decay, m_w_branch, m_w_out, m_w_ffn_in, m_w_dw, m_b_dw, m_w_ffn_out, v_c_ctx, v_w_ada, v_b_ada, v_norm1_w, v_norm2_w, v_w_in, v_b_gate, v_mla_q_norm_a, v_mla_w_qb, v_mla_kv_norm_a, v_mla_w_kvb, v_mla_q_norm, v_mla_k_norm, v_gla_w_gk2, v_gla_b_gk, v_gla_o_norm, v_ret_decay, v_w_branch, v_w_out, v_w_ffn_in, v_w_dw, v_b_dw, v_w_ffn_out):
    given = dict(x=x, c=c, ctx=ctx, c_ctx=c_ctx, w_ada=w_ada, b_ada=b_ada, norm1_w=norm1_w, norm2_w=norm2_w, w_in=w_in, b_gate=b_gate, mla_q_norm_a=mla_q_norm_a, mla_w_qb=mla_w_qb, mla_kv_norm_a=mla_kv_norm_a, mla_w_kvb=mla_w_kvb, mla_q_norm=mla_q_norm, mla_k_norm=mla_k_norm, gla_w_gk2=gla_w_gk2, gla_b_gk=gla_b_gk, gla_o_norm=gla_o_norm, ret_decay=ret_decay, w_branch=w_branch, w_out=w_out, w_ffn_in=w_ffn_in, w_dw=w_dw, b_dw=b_dw, w_ffn_out=w_ffn_out, loss_target=loss_target, m_c_ctx=m_c_ctx, m_w_ada=m_w_ada, m_b_ada=m_b_ada, m_norm1_w=m_norm1_w, m_norm2_w=m_norm2_w, m_w_in=m_w_in, m_b_gate=m_b_gate, m_mla_q_norm_a=m_mla_q_norm_a, m_mla_w_qb=m_mla_w_qb, m_mla_kv_norm_a=m_mla_kv_norm_a, m_mla_w_kvb=m_mla_w_kvb, m_mla_q_norm=m_mla_q_norm, m_mla_k_norm=m_mla_k_norm, m_gla_w_gk2=m_gla_w_gk2, m_gla_b_gk=m_gla_b_gk, m_gla_o_norm=m_gla_o_norm, m_ret_decay=m_ret_decay, m_w_branch=m_w_branch, m_w_out=m_w_out, m_w_ffn_in=m_w_ffn_in, m_w_dw=m_w_dw, m_b_dw=m_b_dw, m_w_ffn_out=m_w_ffn_out, v_c_ctx=v_c_ctx, v_w_ada=v_w_ada, v_b_ada=v_b_ada, v_norm1_w=v_norm1_w, v_norm2_w=v_norm2_w, v_w_in=v_w_in, v_b_gate=v_b_gate, v_mla_q_norm_a=v_mla_q_norm_a, v_mla_w_qb=v_mla_w_qb, v_mla_kv_norm_a=v_mla_kv_norm_a, v_mla_w_kvb=v_mla_w_kvb, v_mla_q_norm=v_mla_q_norm, v_mla_k_norm=v_mla_k_norm, v_gla_w_gk2=v_gla_w_gk2, v_gla_b_gk=v_gla_b_gk, v_gla_o_norm=v_gla_o_norm, v_ret_decay=v_ret_decay, v_w_branch=v_w_branch, v_w_out=v_w_out, v_w_ffn_in=v_w_ffn_in, v_w_dw=v_w_dw, v_b_dw=v_b_dw, v_w_ffn_out=v_w_ffn_out)
    weights = {n: given[n] for n in TWIN_WEIGHTS}
    shared = {n: given[n] for n in SHARED_INPUTS}
    per_example = {n: given[n] for n in ['x', 'c', 'ctx']}
    grad_fn = _jax.value_and_grad(_loss, argnums=(0, 1))

    def one_microbatch(ex, loss_target):
        ex = dict(ex)
        diff = ex.pop(TWIN_DIFF_INPUT)
        return grad_fn(weights, diff, {**shared, **ex}, loss_target)

    if N_MICROBATCH == 1:
        loss, (grad_w, grad_x) = one_microbatch(per_example, given["loss_target"])
    else:
        def body(carry, xs):
            loss_sum, grad_sum = carry
            l_k, (gw_k, gx_k) = one_microbatch(xs[0], xs[1])
            with _jax.named_scope("update"):
                return (loss_sum + l_k, _jax.tree.map(_jnp.add, grad_sum, gw_k)), gx_k

        init = (_jnp.zeros((), _jnp.float32), _jax.tree.map(_jnp.zeros_like, weights))
        (loss, grad_w), grad_x = _jax.lax.scan(body, init, (per_example, given["loss_target"]))
    with _jax.named_scope("update"):
        delta_w, new_m, new_v = {}, {}, {}
        for n in TWIN_WEIGHTS:
            delta_w[n], new_m[n], new_v[n] = _adamw(weights[n], grad_w[n], given["m_" + n], given["v_" + n])
    return (loss, grad_x, *[grad_w[n] for n in TWIN_WEIGHTS], *[delta_w[n] for n in TWIN_WEIGHTS],
            *[new_m[n] for n in TWIN_WEIGHTS], *[new_v[n] for n in TWIN_WEIGHTS])
```

```python
import functools

import jax
import jax.numpy as jnp
from jax import lax
from jax.experimental import pallas as pl
from jax.experimental.pallas import tpu as pltpu

F32 = jnp.float32
MXU_DTYPE = jnp.bfloat16

DEPTH = 2
D_MODEL = 1024
GRID_W = 64
CHUNK = 64
LANES = 128
MLA_HEADS = 8
MLA_NOPE = 64
MLA_ROPE = 32
MLA_QK = MLA_NOPE + MLA_ROPE
MLA_V = 64
GLA_HEADS = 4
GLA_DK = 128
GLA_GATE_NORMALIZER = 16.0
RET_HEADS = 4
RET_DK = 128
D_FF = 2816
ROPE_THETA = 10000.0
RET_THETA = 10000.0
EPS = 1e-6
ADAM_LR = 0.001
ADAM_B1 = 0.9
ADAM_B2 = 0.999
ADAM_EPS = 1e-08
ADAM_WD = 0.01
ADAM_STEP = 10
NEG_BIG = -1e30

VMEM_LIMIT_BYTES = 56 * 1024 * 1024
WEIGHT_BLOCK_BYTES = 8 * 1024 * 1024

SHARDED = (('w_ada', 2), ('w_in', 2), ('b_gate', 2), ('mla_w_qb', 2), ('mla_w_kvb', 2), ('gla_w_gk2', 3),
           ('gla_b_gk', 2), ('w_branch', 3), ('w_out', 1), ('w_ffn_in', 2), ('w_dw', 2), ('w_ffn_out', 1))
SHARDED_F32 = ('b_gate', 'gla_b_gk', 'w_dw')
REPLICATED = ('c_ctx', 'b_ada', 'norm1_w', 'norm2_w', 'mla_q_norm_a', 'mla_kv_norm_a', 'mla_q_norm', 'mla_k_norm',
              'gla_o_norm', 'ret_decay', 'b_dw')
WEIGHT_ORDER = ('c_ctx', 'w_ada', 'b_ada', 'norm1_w', 'norm2_w', 'w_in', 'b_gate', 'mla_q_norm_a', 'mla_w_qb',
                'mla_kv_norm_a', 'mla_w_kvb', 'mla_q_norm', 'mla_k_norm', 'gla_w_gk2', 'gla_b_gk', 'gla_o_norm',
                'ret_decay', 'w_branch', 'w_out', 'w_ffn_in', 'w_dw', 'b_dw', 'w_ffn_out')
N_CHIPS = 4
N_DEV = 8
MESH = pl.DeviceIdType.MESH


def _cparams(sem):
    return pltpu.CompilerParams(dimension_semantics=sem, vmem_limit_bytes=VMEM_LIMIT_BYTES)


def _tile(n, target, unit):
    best = None
    for t in range(unit, min(n, target) + 1, unit):
        if n % t == 0:
            best = t
    return n if best is None else best


_DN = {'nn': (((1,), (0,)), ((), ())), 'nt': (((1,), (1,)), ((), ())), 'tn': (((0,), (0,)), ((), ()))}


def _raw_mm(x, y, form):
    return lax.dot_general(x.astype(MXU_DTYPE), y.astype(MXU_DTYPE), _DN[form], preferred_element_type=F32)


@functools.partial(jax.custom_vjp, nondiff_argnums=(2,))
def _mm(x, y, form):
    return _raw_mm(x, y, form)


def _mm_fwd(x, y, form):
    return _raw_mm(x, y, form), (x, y)


def _mm_bwd(form, res, g):
    x, y = res
    if form == 'nn':
        dx, dy = _mm(g, y, 'nt'), _mm(x, g, 'tn')
    elif form == 'nt':
        dx, dy = _mm(g, y, 'nn'), _mm(g, x, 'tn')
    else:
        dx, dy = _mm(y, g, 'nt'), _mm(x, g, 'nn')
    return dx.astype(x.dtype), dy.astype(y.dtype)


_mm.defvjp(_mm_fwd, _mm_bwd)


@functools.partial(jax.custom_vjp, nondiff_argnums=(1, 2))
def _roll(x, shift, axis):
    return pltpu.roll(x, shift, axis)


def _roll_fwd(x, shift, axis):
    return pltpu.roll(x, shift, axis), None


def _roll_bwd(shift, axis, _, g):
    return (pltpu.roll(g, (g.shape[axis] - shift) % g.shape[axis], axis),)


_roll.defvjp(_roll_fwd, _roll_bwd)


def _tri(n, upper):
    i = lax.broadcasted_iota(jnp.int32, (n, n), 0)
    j = lax.broadcasted_iota(jnp.int32, (n, n), 1)
    return jnp.where((j >= i) if upper else (j <= i), 1.0, 0.0).astype(F32)


def _tri_mm(n, upper, x):
    return jnp.dot(_tri(n, upper), x, precision=lax.Precision.HIGHEST, preferred_element_type=F32)


@functools.partial(jax.custom_vjp, nondiff_argnums=(1,))
def _cumsum_rows(x, reverse):
    return _tri_mm(x.shape[0], reverse, x)


def _cumsum_fwd(x, reverse):
    return _tri_mm(x.shape[0], reverse, x), None


def _cumsum_bwd(reverse, _, g):
    return (_tri_mm(g.shape[0], not reverse, g),)


_cumsum_rows.defvjp(_cumsum_fwd, _cumsum_bwd)


def _rms(x, n=None):
    n = x.shape[-1] if n is None else n
    return x * lax.rsqrt(jnp.sum(x * x, axis=-1, keepdims=True) / n + EPS)


def _mod_row(i, mod16, b_ada):
    m = mod16[0:8] + b_ada
    return jnp.where(i == 0, m[0:1], m[1:2])


def _row_spec(tb, spec):
    arr, cb, width = spec
    return pl.BlockSpec((tb, width), lambda i, cb=cb: (i, cb))


def _whole_spec(arr):
    nd = arr.ndim
    return pl.BlockSpec(arr.shape, lambda i, nd=nd: (0,) * nd)


def _rw_fwd(name, fn, rows, params, outs, tb):
    t = rows[0][0].shape[0]
    nr, npar = len(rows), len(params)

    def body(*refs):
        i = pl.program_id(0)
        rv = [r[...] for r in refs[:nr]]
        pv = [p[...] for p in refs[nr:nr + npar]]
        res = fn(i, rv, pv)
        for o_ref, val in zip(refs[nr + npar:], res):
            o_ref[...] = val.astype(o_ref.dtype)

    return pl.pallas_call(
        body, name=name, grid=(t // tb,),
        in_specs=[_row_spec(tb, s) for s in rows] + [_whole_spec(p) for p in params],
        out_specs=[pl.BlockSpec((tb, w), lambda i: (i, 0)) for w, _ in outs],
        out_shape=[jax.ShapeDtypeStruct((t, w), dt) for w, dt in outs],
        compiler_params=_cparams(("arbitrary",)),
    )(*[s[0] for s in rows], *params)


def _rw_bwd(name, fn, rows, params, gouts, tb, diff_rows):
    t = rows[0][0].shape[0]
    nr, npar, ng, nd = len(rows), len(params), len(gouts), len(diff_rows)

    def body(*refs):
        i = pl.program_id(0)
        rv = [r[...] for r in refs[:nr]]
        pv = [p[...] for p in refs[nr:nr + npar]]
        gv = [g[...].astype(F32) for g in refs[nr + npar:nr + npar + ng]]
        out_refs = refs[nr + npar + ng:]

        def f(dr, pvals):
            vals = list(rv)
            for k, idx in enumerate(diff_rows):
                vals[idx] = dr[k]
            return tuple(fn(i, vals, pvals))

        _, vjp = jax.vjp(f, [rv[k].astype(F32) for k in diff_rows], pv)
        drows, dpars = vjp(tuple(gv))
        for k in range(nd):
            out_refs[k][...] = drows[k]

        @pl.when(i == 0)
        def _():
            for k in range(npar):
                out_refs[nd + k][...] = jnp.zeros_like(out_refs[nd + k])

        for k in range(npar):
            out_refs[nd + k][...] += dpars[k]

    res = pl.pallas_call(
        body, name=name, grid=(t // tb,),
        in_specs=([_row_spec(tb, s) for s in rows] + [_whole_spec(p) for p in params]
                  + [pl.BlockSpec((tb, g.shape[1]), lambda i: (i, 0)) for g in gouts]),
        out_specs=([pl.BlockSpec((tb, rows[k][2]), lambda i: (i, 0)) for k in diff_rows]
                   + [_whole_spec(p) for p in params]),
        out_shape=([jax.ShapeDtypeStruct((t, rows[k][2]), F32) for k in diff_rows]
                   + [jax.ShapeDtypeStruct(p.shape, F32) for p in params]),
        compiler_params=_cparams(("arbitrary",)),
    )(*[s[0] for s in rows], *params, *gouts)
    return list(res[:nd]), list(res[nd:])


def _full(arr):
    return (arr, 0, arr.shape[1])


def _make_rw_op(fn_factory, n_rows, diff_rows, out_widths):
    @functools.partial(jax.custom_vjp, nondiff_argnums=(0, 1))
    def op(cfg, tb, *args):
        return tuple(_rw_fwd(cfg[0] + '_fwd', fn_factory(cfg), [_full(a) for a in args[:n_rows]], list(args[n_rows:]),
                             [(w, F32) for w in out_widths(cfg, args)], tb))

    def fwd(cfg, tb, *args):
        return op(cfg, tb, *args), args

    def bwd(cfg, tb, args, g):
        drows, dpars = _rw_bwd(cfg[0] + '_bwd', fn_factory(cfg), [_full(a) for a in args[:n_rows]],
                               list(args[n_rows:]), list(g), tb, diff_rows)
        full = [jnp.zeros_like(a) for a in args[:n_rows]]
        for k, idx in enumerate(diff_rows):
            full[idx] = drows[k]
        return tuple(full) + tuple(dpars)

    op.defvjp(fwd, bwd)
    return op


def _silu_fn(cfg):
    return lambda i, rows, params: (jax.nn.silu(rows[0]),)


_silu_op = _make_rw_op(_silu_fn, 1, (0,), lambda cfg, args: (args[0].shape[1],))


def _normmod_fn(cfg):
    _, shift_at, scale_at = cfg

    def fn(i, rows, params):
        (h,) = rows
        nw, mod16, b_ada = params
        mr = _mod_row(i, mod16, b_ada)
        d = h.shape[1]
        return (_rms(h) * nw * (1.0 + mr[:, scale_at * d:(scale_at + 1) * d]) + mr[:, shift_at * d:(shift_at + 1) * d],)

    return fn


_normmod_op = _make_rw_op(_normmod_fn, 1, (0,), lambda cfg, args: (args[0].shape[1],))


def _resid_fn(cfg):
    _, gate_at = cfg

    def fn(i, rows, params):
        h, y = rows
        mod16, b_ada = params
        mr = _mod_row(i, mod16, b_ada)
        d = h.shape[1]
        return (h + mr[:, gate_at * d:(gate_at + 1) * d] * y,)

    return fn


_resid_op = _make_rw_op(_resid_fn, 2, (0, 1), lambda cfg, args: (args[0].shape[1],))


def _merge_fn(cfg):
    def fn(i, rows, params):
        z0, z1, z2, pg = rows
        (bg,) = params
        d = z0.shape[1]
        out = None
        for n, z in enumerate((z0, z1, z2)):
            term = jax.nn.sigmoid(pg[:, n * d:(n + 1) * d] + bg[:, n * d:(n + 1) * d]) * z
            out = term if out is None else out + term
        return (out,)

    return fn


_merge_op = _make_rw_op(_merge_fn, 4, (0, 1, 2, 3), lambda cfg, args: (args[0].shape[1],))


def _matmul(name, a, b, form):
    if form == 'nn':
        (m, k), (_, n) = a.shape, b.shape
        tm = _tile(m, 256, 8)
        tn = _tile(n, max(LANES, WEIGHT_BLOCK_BYTES // (k * b.dtype.itemsize)), LANES)

        def body(a_ref, b_ref, o_ref):
            o_ref[...] = _raw_mm(a_ref[...], b_ref[...], 'nn')

        return pl.pallas_call(
            body, name=name, grid=(n // tn, m // tm),
            in_specs=[pl.BlockSpec((tm, k), lambda j, i: (i, 0)), pl.BlockSpec((k, tn), lambda j, i: (0, j))],
            out_specs=pl.BlockSpec((tm, tn), lambda j, i: (i, j)),
            out_shape=jax.ShapeDtypeStruct((m, n), F32), compiler_params=_cparams(("arbitrary", "arbitrary")),
        )(a, b)
    if form == 'nt':
        (m, n), (k, _) = a.shape, b.shape
        tm = _tile(m, 256, 8)
        tk = _tile(k, max(LANES, WEIGHT_BLOCK_BYTES // (n * b.dtype.itemsize)), LANES)

        def body(a_ref, b_ref, o_ref):
            o_ref[...] = _raw_mm(a_ref[...], b_ref[...], 'nt')

        return pl.pallas_call(
            body, name=name, grid=(k // tk, m // tm),
            in_specs=[pl.BlockSpec((tm, n), lambda j, i: (i, 0)), pl.BlockSpec((tk, n), lambda j, i: (j, 0))],
            out_specs=pl.BlockSpec((tm, tk), lambda j, i: (i, j)),
            out_shape=jax.ShapeDtypeStruct((m, k), F32), compiler_params=_cparams(("arbitrary", "arbitrary")),
        )(a, b)
    (m, ka), (_, n) = a.shape, b.shape
    tka, tn, tmc = _tile(ka, 1024, LANES), _tile(n, 1024, LANES), _tile(m, 256, 8)

    def body(a_ref, b_ref, o_ref):
        @pl.when(pl.program_id(2) == 0)
        def _():
            o_ref[...] = jnp.zeros_like(o_ref)

        o_ref[...] += _raw_mm(a_ref[...], b_ref[...], 'tn')

    return pl.pallas_call(
        body, name=name, grid=(ka // tka, n // tn, m // tmc),
        in_specs=[pl.BlockSpec((tmc, tka), lambda i, j, s: (s, i)), pl.BlockSpec((tmc, tn), lambda i, j, s: (s, j))],
        out_specs=pl.BlockSpec((tka, tn), lambda i, j, s: (i, j)),
        out_shape=jax.ShapeDtypeStruct((ka, n), F32), compiler_params=_cparams(("arbitrary", "arbitrary", "arbitrary")),
    )(a, b)


@functools.partial(jax.custom_vjp, nondiff_argnums=(0,))
def _mm_op(name, a, w):
    return _matmul(name + '_fwd', a, w.astype(MXU_DTYPE), 'nn')


def _mm_op_fwd(name, a, w):
    wb = w.astype(MXU_DTYPE)
    return _matmul(name + '_fwd', a, wb, 'nn'), (a, wb)


def _mm_op_bwd(name, res, g):
    a, wb = res
    return _matmul(name + '_da', g, wb, 'nt'), _matmul(name + '_dw', a, g, 'tn')


_mm_op.defvjp(_mm_op_fwd, _mm_op_bwd)


def _rope128(x, c, a, b):
    return x * c + _roll(x, LANES - 8, 1) * a + _roll(x, 8, 1) * b


def _mla_prep_fn(i, rows, params):
    pm, c, a, b = rows
    qna, wqb, kvna, wkn, wv, qn, kn = params
    cq, ckv, kr_slot = pm[:, 0:256], pm[:, 256:384], pm[:, 384:512]
    q_all = _mm(_rms(cq) * qna, wqb, 'nn')
    ckvn = _rms(ckv) * kvna
    k_all = _mm(ckvn, wkn, 'nn')
    v_all = _mm(ckvn, wv, 'nn')
    qs, ks = [], []
    for h in range(MLA_HEADS):
        sl = slice(LANES * h, LANES * (h + 1))
        qs.append(_rope128(_rms(q_all[:, sl], MLA_QK) * qn, c, a, b))
        ks.append(_rope128(_rms(k_all[:, sl] + kr_slot, MLA_QK) * kn, c, a, b))
    return jnp.concatenate(qs, axis=1), jnp.concatenate(ks, axis=1), v_all


def _attn_fwd(name, q, k, v, tb, ctx_len):
    t = q.shape[0]
    scale = MLA_QK ** -0.5

    def body(q_ref, k_ref, v_ref, o_ref, lse_ref):
        qi = pl.program_id(1)
        s = _raw_mm(q_ref[...], k_ref[...], 'nt') * scale
        col = lax.broadcasted_iota(jnp.int32, s.shape, 1)
        s = jnp.where((qi == 0) & (col >= ctx_len), NEG_BIG, s)
        m = jnp.max(s, axis=-1, keepdims=True)
        p = jnp.exp(s - m)
        l = jnp.sum(p, axis=-1, keepdims=True)
        o_ref[...] = _raw_mm(p, v_ref[...], 'nn') / l
        lse_ref[...] = jnp.broadcast_to(m + jnp.log(l), lse_ref.shape)

    blk = pl.BlockSpec((tb, LANES), lambda h, i: (i, h))
    whole = pl.BlockSpec((t, LANES), lambda h, i: (0, h))
    return pl.pallas_call(
        body, name=name, grid=(MLA_HEADS, t // tb), in_specs=[blk, whole, whole], out_specs=[blk, blk],
        out_shape=[jax.ShapeDtypeStruct(q.shape, F32)] * 2, compiler_params=_cparams(("arbitrary", "arbitrary")),
    )(q, k, v)


def _attn_bwd(name, q, k, v, o, lse, do, tb, ctx_len):
    t = q.shape[0]
    scale = MLA_QK ** -0.5
    ck = _tile(t, 2816, 256)

    def body(q_ref, k_ref, v_ref, o_ref, lse_ref, do_ref, dq_ref, dk_ref, dv_ref):
        qi = pl.program_id(1)

        @pl.when(qi == 0)
        def _():
            dk_ref[...] = jnp.zeros_like(dk_ref)
            dv_ref[...] = jnp.zeros_like(dv_ref)

        q = q_ref[...]
        do = do_ref[...]
        lse1 = lse_ref[...][:, 0:1]
        delta = jnp.sum(do * o_ref[...], axis=-1, keepdims=True)
        dq = jnp.zeros((tb, LANES), F32)
        for c in range(t // ck):
            rows = pl.ds(c * ck, ck)
            ks, vs = k_ref[rows, :], v_ref[rows, :]
            s = _raw_mm(q, ks, 'nt') * scale
            col = lax.broadcasted_iota(jnp.int32, s.shape, 1) + c * ck
            s = jnp.where((qi == 0) & (col >= ctx_len), NEG_BIG, s)
            p = jnp.exp(s - lse1)
            ds = p * (_raw_mm(do, vs, 'nt') - delta) * scale
            dq = dq + _raw_mm(ds, ks, 'nn')
            dk_ref[rows, :] += _raw_mm(ds, q, 'tn')
            dv_ref[rows, :] += _raw_mm(p, do, 'tn')
        dq_ref[...] = dq

    blk = pl.BlockSpec((tb, LANES), lambda h, i: (i, h))
    whole = pl.BlockSpec((t, LANES), lambda h, i: (0, h))
    return pl.pallas_call(
        body, name=name, grid=(MLA_HEADS, t // tb), in_specs=[blk, whole, whole, blk, blk, blk],
        out_specs=[blk, whole, whole], out_shape=[jax.ShapeDtypeStruct(q.shape, F32)] * 3,
        compiler_params=_cparams(("arbitrary", "arbitrary")),
    )(q, k, v, o, lse, do)


def _mla_rows(pm, tabs):
    return [(pm, 0, pm.shape[1])] + [_full(x) for x in tabs]


@functools.partial(jax.custom_vjp, nondiff_argnums=(0, 1))
def _mla_branch(name, tb, pm, tabs, params):
    return _mla_branch_fwd(name, tb, pm, tabs, params)[0]


def _mla_branch_fwd(name, tb, pm, tabs, params):
    w = LANES * MLA_HEADS
    q, k, v = _rw_fwd(name + '_prep', _mla_prep_fn, _mla_rows(pm, tabs), list(params), [(w, MXU_DTYPE)] * 3, tb)
    o, lse = _attn_fwd(name + '_attn', q, k, v, tb, tb)
    return o, (pm, tabs, params, q, k, v, o, lse)


def _mla_branch_bwd(name, tb, res, do):
    pm, tabs, params, q, k, v, o, lse = res
    dq, dk, dv = _attn_bwd(name + '_attn_bwd', q, k, v, o, lse, do, tb, tb)
    (dpm,), dpars = _rw_bwd(name + '_prep_bwd', _mla_prep_fn, _mla_rows(pm, tabs), list(params), [dq, dk, dv], tb, (0,))
    return dpm, tuple(jnp.zeros_like(x) for x in tabs), tuple(dpars)


_mla_branch.defvjp(_mla_branch_fwd, _mla_branch_bwd)


def _chunk_masks(reverse):
    i = lax.broadcasted_iota(jnp.int32, (CHUNK, CHUNK), 0)
    j = lax.broadcasted_iota(jnp.int32, (CHUNK, CHUNK), 1)
    return i, j, ((j > i) if reverse else (j <= i))


def _gla_chunk(reverse, rows, params, st0):
    q, k, v, la = rows
    q = q * (GLA_DK ** -0.5)
    cum = _cumsum_rows(la, reverse)
    tot = cum[0:1] if reverse else cum[CHUNK - 1:CHUNK]
    st1 = st0 * jnp.exp(tot) + _mm(v, k * jnp.exp(tot - cum), 'tn')
    qd = q * jnp.exp(cum)
    _, _, mask = _chunk_masks(reverse)
    att = jnp.where(mask, _mm(qd, k * jnp.exp(-cum), 'nt'), 0.0)
    return _mm(att, v, 'nn') + _mm(qd, st0, 'nt'), st1


def _ret_chunk(reverse, rows, params, st0):
    q, k, v, cc, ss = rows
    (lg,) = params
    q = q * cc + _roll(q, RET_DK // 2, 1) * ss
    k = (k * cc + _roll(k, RET_DK // 2, 1) * ss) * (RET_DK ** -0.5)
    r = lax.broadcasted_iota(jnp.int32, (CHUNK, LANES), 0).astype(F32)
    zeta = jnp.exp((r if reverse else (CHUNK - 1.0 - r)) * lg)
    xi = jnp.exp(((CHUNK - r) if reverse else (r + 1.0)) * lg)
    st1 = st0 * jnp.exp(CHUNK * lg) + _mm(v, k * zeta, 'tn')
    i, j, mask = _chunk_masks(reverse)
    rel = jnp.where(mask, (j - i) if reverse else (i - j), 0).astype(F32)
    dmat = jnp.where(mask, jnp.exp(rel * lg[:, 0:CHUNK]), 0.0)
    att = _mm(q, k, 'nt') * dmat
    return _mm(att, v, 'nn') + _mm(q, st0, 'nt') * xi, st1


def _scan_order(reverse, nblk):
    if reverse:
        return lambda t: jnp.where(t == 0, 0, nblk - t)
    return lambda t: t


def _scan_fwd(name, chunk_fn, reverse, rows, params, tb, heads):
    t = rows[0][0].shape[0]
    nblk, cpb = t // tb, tb // CHUNK
    blk_of = _scan_order(reverse, nblk)
    nr, npar = len(rows), len(params)
    order = list(range(cpb))[::-1] if reverse else list(range(cpb))

    def body(*refs):
        row_refs, par_refs = refs[:nr], refs[nr:nr + npar]
        o_ref, st_out_ref, st_ref = refs[nr + npar:]

        @pl.when(pl.program_id(1) == 0)
        def _():
            st_ref[...] = jnp.zeros_like(st_ref)

        pv = [p[0] for p in par_refs]
        for c in order:
            sl = pl.ds(c * CHUNK, CHUNK)
            st0 = st_ref[...]
            st_out_ref[0, c] = st0
            o, st1 = chunk_fn(reverse, [r[sl, :] for r in row_refs], pv, st0)
            o_ref[sl, :] = o
            st_ref[...] = st1

    def rspec(spec):
        _, cb = spec
        if cb is None:
            return pl.BlockSpec((tb, LANES), lambda h, s: (blk_of(s), 0))
        return pl.BlockSpec((tb, LANES), lambda h, s, cb=cb: (blk_of(s), cb + h))

    return pl.pallas_call(
        body, name=name, grid=(heads, nblk),
        in_specs=[rspec(s) for s in rows] + [pl.BlockSpec((1, 1, LANES), lambda h, s: (h, 0, 0)) for _ in params],
        out_specs=[pl.BlockSpec((tb, LANES), lambda h, s: (blk_of(s), h)),
                   pl.BlockSpec((1, cpb, LANES, LANES), lambda h, s: (h, blk_of(s), 0, 0))],
        out_shape=[jax.ShapeDtypeStruct((t, heads * LANES), F32),
                   jax.ShapeDtypeStruct((heads, t // CHUNK, LANES, LANES), F32)],
        scratch_shapes=[pltpu.VMEM((LANES, LANES), F32)],
        compiler_params=_cparams(("arbitrary", "arbitrary")),
    )(*[s[0] for s in rows], *params)


def _scan_bwd(name, chunk_fn, reverse, rows, params, states, do, tb, heads, n_diff):
    t = rows[0][0].shape[0]
    nblk, cpb = t // tb, tb // CHUNK
    fwd_blk = _scan_order(reverse, nblk)
    blk_of = lambda s: fwd_blk(nblk - 1 - s)
    nr, npar = len(rows), len(params)
    order = list(range(cpb)) if reverse else list(range(cpb))[::-1]

    def body(*refs):
        row_refs, par_refs = refs[:nr], refs[nr:nr + npar]
        st_in_ref, do_ref = refs[nr + npar:nr + npar + 2]
        out_refs = refs[nr + npar + 2:-1]
        dst_ref = refs[-1]

        @pl.when(pl.program_id(1) == 0)
        def _():
            dst_ref[...] = jnp.zeros_like(dst_ref)
            for k in range(npar):
                out_refs[n_diff + k][...] = jnp.zeros_like(out_refs[n_diff + k])

        pv = [p[0] for p in par_refs]
        for c in order:
            sl = pl.ds(c * CHUNK, CHUNK)
            rv = [r[sl, :] for r in row_refs]

            def f(dr, pvals, st0):
                return chunk_fn(reverse, list(dr) + rv[n_diff:], pvals, st0)

            _, vjp = jax.vjp(f, rv[:n_diff], pv, st_in_ref[0, c])
            drows, dpars, dst0 = vjp((do_ref[sl, :], dst_ref[...]))
            for k in range(n_diff):
                out_refs[k][sl, :] = drows[k]
            for k in range(npar):
                out_refs[n_diff + k][0] += dpars[k]
            dst_ref[...] = dst0

    def rspec(spec):
        _, cb = spec
        if cb is None:
            return pl.BlockSpec((tb, LANES), lambda h, s: (blk_of(s), 0))
        return pl.BlockSpec((tb, LANES), lambda h, s, cb=cb: (blk_of(s), cb + h))

    hblk = pl.BlockSpec((tb, LANES), lambda h, s: (blk_of(s), h))
    pblk = pl.BlockSpec((1, 1, LANES), lambda h, s: (h, 0, 0))
    res = pl.pallas_call(
        body, name=name, grid=(heads, nblk),
        in_specs=([rspec(s) for s in rows] + [pblk for _ in params]
                  + [pl.BlockSpec((1, cpb, LANES, LANES), lambda h, s: (h, blk_of(s), 0, 0)), hblk]),
        out_specs=[hblk] * n_diff + [pblk for _ in params],
        out_shape=([jax.ShapeDtypeStruct((t, heads * LANES), F32)] * n_diff
                   + [jax.ShapeDtypeStruct(p.shape, F32) for p in params]),
        scratch_shapes=[pltpu.VMEM((LANES, LANES), F32)],
        compiler_params=_cparams(("arbitrary", "arbitrary")),
    )(*[s[0] for s in rows], *params, states, do)
    return list(res[:n_diff]), list(res[n_diff:])


def _gla_la_fn(i, rows, params):
    (r,) = rows
    w2f, w2b, bgk = params
    la_f = jax.nn.log_sigmoid(_mm(r, w2f, 'nn') + bgk[0:1]) / GLA_GATE_NORMALIZER
    la_b = jax.nn.log_sigmoid(_mm(r, w2b, 'nn') + bgk[1:2]) / GLA_GATE_NORMALIZER
    return la_f, la_b


def _headnorm_fn(heads, with_weight):
    def fn(i, rows, params):
        o_f, o_b, g = rows
        outs = []
        for h in range(heads):
            sl = slice(LANES * h, LANES * (h + 1))
            y = _rms(o_f[:, sl] + o_b[:, sl])
            outs.append(y * params[0] if with_weight else y)
        return (jnp.concatenate(outs, axis=1) * jax.nn.silu(g),)

    return fn


@functools.partial(jax.custom_vjp, nondiff_argnums=(0, 1))
def _gla_branch(name, tb, pg, params):
    return _gla_branch_fwd(name, tb, pg, params)[0]


def _gla_rows(pg, la):
    return [(pg, 0), (pg, GLA_HEADS), (pg, 2 * GLA_HEADS), (la, 0)]


def _gla_branch_fwd(name, tb, pg, params):
    w2f, w2b, bgk, onorm = params
    w = GLA_HEADS * LANES
    la_f, la_b = _rw_fwd(name + '_la', _gla_la_fn, [(pg, 4 * w // LANES, LANES)], [w2f, w2b, bgk], [(w, F32)] * 2, tb)
    o_f, st_f = _scan_fwd(name + '_scan_f', _gla_chunk, False, _gla_rows(pg, la_f), [], tb, GLA_HEADS)
    o_b, st_b = _scan_fwd(name + '_scan_b', _gla_chunk, True, _gla_rows(pg, la_b), [], tb, GLA_HEADS)
    (y,) = _rw_fwd(name + '_norm', _headnorm_fn(GLA_HEADS, True), [_full(o_f), _full(o_b), (pg, 3, w)], [onorm],
                   [(w, F32)], tb)
    return y, (pg, params, la_f, la_b, o_f, o_b, st_f, st_b)


def _gla_branch_bwd(name, tb, res, dy):
    pg, params, la_f, la_b, o_f, o_b, st_f, st_b = res
    w2f, w2b, bgk, onorm = params
    w = GLA_HEADS * LANES
    (do_f, do_b, dg), (donorm,) = _rw_bwd(name + '_norm_bwd', _headnorm_fn(GLA_HEADS, True),
                                          [_full(o_f), _full(o_b), (pg, 3, w)], [onorm], [dy], tb, (0, 1, 2))
    (dq_f, dk_f, dv_f, dla_f), _ = _scan_bwd(name + '_scan_f_bwd', _gla_chunk, False, _gla_rows(pg, la_f), [], st_f,
                                             do_f, tb, GLA_HEADS, 4)
    (dq_b, dk_b, dv_b, dla_b), _ = _scan_bwd(name + '_scan_b_bwd', _gla_chunk, True, _gla_rows(pg, la_b), [], st_b,
                                             do_b, tb, GLA_HEADS, 4)
    (dr,), (dw2f, dw2b, dbgk) = _rw_bwd(name + '_la_bwd', _gla_la_fn, [(pg, 4 * w // LANES, LANES)], [w2f, w2b, bgk],
                                        [dla_f, dla_b], tb, (0,))
    dpg = jnp.concatenate([dq_f + dq_b, dk_f + dk_b, dv_f + dv_b, dg, dr], axis=1)
    return dpg, (dw2f, dw2b, dbgk, donorm)


_gla_branch.defvjp(_gla_branch_fwd, _gla_branch_bwd)


@functools.partial(jax.custom_vjp, nondiff_argnums=(0, 1))
def _ret_branch(name, tb, pr, tabs, lg):
    return _ret_branch_fwd(name, tb, pr, tabs, lg)[0]


def _ret_rows(pr, tabs):
    return [(pr, 0), (pr, RET_HEADS), (pr, 2 * RET_HEADS), (tabs[0], None), (tabs[1], None)]


def _ret_branch_fwd(name, tb, pr, tabs, lg):
    w = RET_HEADS * LANES
    o_f, st_f = _scan_fwd(name + '_scan_f', _ret_chunk, False, _ret_rows(pr, tabs), [lg[0]], tb, RET_HEADS)
    o_b, st_b = _scan_fwd(name + '_scan_b', _ret_chunk, True, _ret_rows(pr, tabs), [lg[1]], tb, RET_HEADS)
    (y,) = _rw_fwd(name + '_norm', _headnorm_fn(RET_HEADS, False), [_full(o_f), _full(o_b), (pr, 3, w)], [],
                   [(w, F32)], tb)
    return y, (pr, tabs, lg, o_f, o_b, st_f, st_b)


def _ret_branch_bwd(name, tb, res, dy):
    pr, tabs, lg, o_f, o_b, st_f, st_b = res
    w = RET_HEADS * LANES
    (do_f, do_b, dg), _ = _rw_bwd(name + '_norm_bwd', _headnorm_fn(RET_HEADS, False),
                                  [_full(o_f), _full(o_b), (pr, 3, w)], [], [dy], tb, (0, 1, 2))
    (dq_f, dk_f, dv_f), (dlg_f,) = _scan_bwd(name + '_scan_f_bwd', _ret_chunk, False, _ret_rows(pr, tabs), [lg[0]],
                                             st_f, do_f, tb, RET_HEADS, 3)
    (dq_b, dk_b, dv_b), (dlg_b,) = _scan_bwd(name + '_scan_b_bwd', _ret_chunk, True, _ret_rows(pr, tabs), [lg[1]],
                                             st_b, do_b, tb, RET_HEADS, 3)
    dpr = jnp.concatenate([dq_f + dq_b, dk_f + dk_b, dv_f + dv_b, dg], axis=1)
    return dpr, tuple(jnp.zeros_like(x) for x in tabs), jnp.stack([dlg_f, dlg_b])


_ret_branch.defvjp(_ret_branch_fwd, _ret_branch_bwd)


HALO = 8


def _halo_specs(tb, nblk, width, col_block):
    r = tb // HALO
    prev = pl.BlockSpec((HALO, width), lambda i: (jnp.maximum(i * r - 1, 0), col_block))
    nxt = pl.BlockSpec((HALO, width), lambda i: (jnp.minimum((i + 1) * r, nblk * r - 1), col_block))
    return prev, nxt


def _shifted(x, prev_blk, next_blk, i, nblk):
    tb = x.shape[0]
    row = lax.broadcasted_iota(jnp.int32, x.shape, 0)
    prev_row = jnp.where(i >= 2, prev_blk[HALO - 1:HALO], 0.0)
    next_row = jnp.where((i >= 1) & (i < nblk - 1), next_blk[0:1], 0.0)
    down = jnp.where(row == 0, prev_row, pltpu.roll(x, 1, 0))
    up = jnp.where(row == tb - 1, next_row, pltpu.roll(x, tb - 1, 0))
    return down, up


def _gelu_up(c, up):
    return jax.nn.gelu(c) * up


def _convact_fwd_call(name, gu, w_dw, b_dw, tb):
    t = gu.shape[0]
    nblk = t // tb
    prev_spec, next_spec = _halo_specs(tb, nblk, D_FF, 0)

    def body(g_ref, up_ref, prev_ref, next_ref, w_ref, b_ref, o_ref):
        i = pl.program_id(0)
        g = g_ref[...]
        down, upw = _shifted(g, prev_ref[...], next_ref[...], i, nblk)
        w = w_ref[...]
        c = w[0:1] * down + w[1:2] * g + w[2:3] * upw + b_ref[...]
        o_ref[...] = _gelu_up(c, up_ref[...])

    return pl.pallas_call(
        body, name=name, grid=(nblk,),
        in_specs=[pl.BlockSpec((tb, D_FF), lambda i: (i, 0)), pl.BlockSpec((tb, D_FF), lambda i: (i, 1)), prev_spec,
                  next_spec, _whole_spec(w_dw), _whole_spec(b_dw)],
        out_specs=pl.BlockSpec((tb, D_FF), lambda i: (i, 0)), out_shape=jax.ShapeDtypeStruct((t, D_FF), F32),
        compiler_params=_cparams(("arbitrary",)),
    )(gu, gu, gu, gu, w_dw, b_dw)


def _convact_bwd_calls(name, gu, w_dw, b_dw, dact, tb):
    t = gu.shape[0]
    nblk = t // tb
    prev_spec, next_spec = _halo_specs(tb, nblk, D_FF, 0)

    def body1(g_ref, up_ref, prev_ref, next_ref, w_ref, b_ref, da_ref, dc_ref, dup_ref, dw_ref, db_ref):
        i = pl.program_id(0)
        g = g_ref[...]
        down, upw = _shifted(g, prev_ref[...], next_ref[...], i, nblk)
        w = w_ref[...]
        c = w[0:1] * down + w[1:2] * g + w[2:3] * upw + b_ref[...]
        _, vjp = jax.vjp(_gelu_up, c, up_ref[...])
        dc, dup = vjp(da_ref[...])
        dc_ref[...] = dc
        dup_ref[...] = dup

        @pl.when(i == 0)
        def _():
            dw_ref[...] = jnp.zeros_like(dw_ref)
            db_ref[...] = jnp.zeros_like(db_ref)

        dw_ref[0:1, :] += jnp.sum(dc * down, axis=0, keepdims=True)
        dw_ref[1:2, :] += jnp.sum(dc * g, axis=0, keepdims=True)
        dw_ref[2:3, :] += jnp.sum(dc * upw, axis=0, keepdims=True)
        db_ref[...] += jnp.sum(dc, axis=0, keepdims=True)

    blk = pl.BlockSpec((tb, D_FF), lambda i: (i, 0))
    dc, dup, dw, db = pl.pallas_call(
        body1, name=name + '_a', grid=(nblk,),
        in_specs=[blk, pl.BlockSpec((tb, D_FF), lambda i: (i, 1)), prev_spec, next_spec, _whole_spec(w_dw),
                  _whole_spec(b_dw), blk],
        out_specs=[blk, blk, _whole_spec(w_dw), _whole_spec(b_dw)],
        out_shape=[jax.ShapeDtypeStruct((t, D_FF), F32)] * 2 + [jax.ShapeDtypeStruct(w_dw.shape, F32),
                                                                jax.ShapeDtypeStruct(b_dw.shape, F32)],
        compiler_params=_cparams(("arbitrary",)),
    )(gu, gu, gu, gu, w_dw, b_dw, dact)

    def body2(dc_ref, prev_ref, next_ref, dup_ref, w_ref, o_ref):
        i = pl.program_id(0)
        dc_blk = dc_ref[...]
        down, upw = _shifted(dc_blk, prev_ref[...], next_ref[...], i, nblk)
        w = w_ref[...]
        o_ref[:, 0:D_FF] = w[0:1] * upw + w[1:2] * dc_blk + w[2:3] * down
        o_ref[:, D_FF:2 * D_FF] = dup_ref[...]

    dgu = pl.pallas_call(
        body2, name=name + '_b', grid=(nblk,),
        in_specs=[blk, prev_spec, next_spec, blk, _whole_spec(w_dw)],
        out_specs=pl.BlockSpec((tb, 2 * D_FF), lambda i: (i, 0)), out_shape=jax.ShapeDtypeStruct((t, 2 * D_FF), F32),
        compiler_params=_cparams(("arbitrary",)),
    )(dc, dc, dc, dup, w_dw)
    return dgu, dw, db


@functools.partial(jax.custom_vjp, nondiff_argnums=(0, 1))
def _convact(name, tb, gu, w_dw, b_dw):
    return _convact_fwd_call(name + '_fwd', gu, w_dw, b_dw, tb)


def _convact_fwd(name, tb, gu, w_dw, b_dw):
    return _convact_fwd_call(name + '_fwd', gu, w_dw, b_dw, tb), (gu, w_dw, b_dw)


def _convact_bwd(name, tb, res, dact):
    gu, w_dw, b_dw = res
    return _convact_bwd_calls(name + '_bwd', gu, w_dw, b_dw, dact, tb)


_convact.defvjp(_convact_fwd, _convact_bwd)


def _loss_fwd_call(h, target, tb):
    nlat = target.shape[0] // tb

    def body(h_ref, t_ref, o_ref):
        @pl.when(pl.program_id(0) == 0)
        def _():
            o_ref[...] = jnp.zeros_like(o_ref)

        e = h_ref[...] - t_ref[...]
        o_ref[...] += jnp.sum(e * e, axis=0, keepdims=True)

    cols = pl.pallas_call(
        body, name='loss_fwd', grid=(nlat,),
        in_specs=[pl.BlockSpec((tb, D_MODEL), lambda i: (i + 1, 0)), pl.BlockSpec((tb, D_MODEL), lambda i: (i, 0))],
        out_specs=pl.BlockSpec((1, D_MODEL), lambda i: (0, 0)), out_shape=jax.ShapeDtypeStruct((1, D_MODEL), F32),
        compiler_params=_cparams(("arbitrary",)),
    )(h, target)
    return (0.5 / D_MODEL) * jnp.sum(cols)


def _loss_bwd_call(h, target, gbar, tb):
    def body(g_ref, h_ref, t_ref, o_ref):
        live = jnp.where(pl.program_id(0) == 0, 0.0, g_ref[...] * (1.0 / D_MODEL))
        o_ref[...] = live * (h_ref[...] - t_ref[...])

    return pl.pallas_call(
        body, name='loss_bwd', grid=(h.shape[0] // tb,),
        in_specs=[pl.BlockSpec((1, 1), lambda i: (0, 0)), pl.BlockSpec((tb, D_MODEL), lambda i: (i, 0)),
                  pl.BlockSpec((tb, D_MODEL), lambda i: (jnp.maximum(i - 1, 0), 0))],
        out_specs=pl.BlockSpec((tb, D_MODEL), lambda i: (i, 0)), out_shape=jax.ShapeDtypeStruct(h.shape, F32),
        compiler_params=_cparams(("arbitrary",)),
    )(gbar.reshape(1, 1), h, target)


@functools.partial(jax.custom_vjp, nondiff_argnums=(0,))
def _loss_op(tb, h, target):
    return _loss_fwd_call(h, target, tb)


def _loss_op_fwd(tb, h, target):
    return _loss_fwd_call(h, target, tb), (h, target)


def _loss_op_bwd(tb, res, gbar):
    h, target = res
    return _loss_bwd_call(h, target, gbar, tb), jnp.zeros_like(target)


_loss_op.defvjp(_loss_op_fwd, _loss_op_bwd)


def _rope_tables(pos, dim, theta):
    inv = theta ** (-jnp.arange(dim // 2, dtype=F32) * 2.0 / dim)
    ang = pos.astype(F32)[:, None] * inv[None, :]
    return jnp.cos(ang), jnp.sin(ang)


def _mla_tables(seq, ctx_len):
    rows = seq // GRID_W
    row_pos = jnp.repeat(jnp.arange(rows), GRID_W)
    col_pos = jnp.tile(jnp.arange(GRID_W), rows)
    cos_r, sin_r = _rope_tables(row_pos, MLA_ROPE // 2, ROPE_THETA)
    cos_c, sin_c = _rope_tables(col_pos, MLA_ROPE // 2, ROPE_THETA)
    one = jnp.ones((seq, MLA_NOPE), F32)
    z8 = jnp.zeros((seq, 8), F32)
    pad1 = jnp.ones((seq, LANES - MLA_QK), F32)
    pad0 = jnp.zeros((seq, LANES - MLA_QK), F32)
    z64 = jnp.zeros((seq, MLA_NOPE), F32)
    c = jnp.concatenate([one, cos_r, cos_r, cos_c, cos_c, pad1], axis=1)
    a = jnp.concatenate([z64, -sin_r, z8, -sin_c, z8, pad0], axis=1)
    b = jnp.concatenate([z64, z8, sin_r, z8, sin_c, pad0], axis=1)
    ctx_rows = lambda fill: jnp.full((ctx_len, LANES), fill, F32)
    return (jnp.concatenate([ctx_rows(1.0), c]), jnp.concatenate([ctx_rows(0.0), a]),
            jnp.concatenate([ctx_rows(0.0), b]))


def _ret_tables(total):
    inv = 1.0 / (RET_THETA ** jnp.linspace(0.0, 1.0, RET_DK // 2, dtype=F32))
    ang = jnp.arange(total).astype(F32)[:, None] * inv[None, :]
    cos, sin = jnp.cos(ang), jnp.sin(ang)
    return jnp.concatenate([cos, cos], axis=1), jnp.concatenate([-sin, sin], axis=1)


def _head_slots(w, heads, width):
    k = w.shape[0]
    return jnp.pad(w.reshape(k, heads, width), ((0, 0), (0, 0), (0, LANES - width))).reshape(k, heads * LANES)


def _pad_lanes(v, width):
    return jnp.pad(v, (0, LANES - width)).reshape(1, LANES)


def _layer(l, w, h, mod16, tabs_mla, tabs_ret, tb):
    nm = 'l%d_' % l
    row = lambda v: v.reshape(1, -1)
    b_ada = row(w['b_ada'][l])
    w_in = w['w_in'][l]
    zc = lambda n: jnp.zeros((D_MODEL, n), F32)
    w_mla = jnp.concatenate([w_in[:, 0:384], zc(MLA_NOPE), w_in[:, 384:416], zc(LANES - MLA_QK)], axis=1)
    w_gla = jnp.concatenate([w_in[:, 416:2464], w_in[:, 2464:2496], zc(LANES - 32)], axis=1)
    w_ret = w_in[:, 2496:4544]
    w_gates = w_in[:, 4544:7616]

    a = _normmod_op((nm + 'norm1', 0, 1), tb, h, row(w['norm1_w'][l]), mod16, b_ada)[0]
    pm = _mm_op(nm + 'in_mla', a, w_mla)
    pg = _mm_op(nm + 'in_gla', a, w_gla)
    pr = _mm_op(nm + 'in_ret', a, w_ret)
    pgate = _mm_op(nm + 'in_gate', a, w_gates)

    kvb = w['mla_w_kvb'][l].reshape(-1, MLA_HEADS, MLA_NOPE + MLA_V)
    kdim = kvb.shape[0]
    mla_params = (row(w['mla_q_norm_a'][l]), _head_slots(w['mla_w_qb'][l], MLA_HEADS, MLA_QK),
                  row(w['mla_kv_norm_a'][l]), _head_slots(kvb[:, :, :MLA_NOPE].reshape(kdim, -1), MLA_HEADS, MLA_NOPE),
                  _head_slots(kvb[:, :, MLA_NOPE:].reshape(kdim, -1), MLA_HEADS, MLA_V),
                  _pad_lanes(w['mla_q_norm'][l], MLA_QK), _pad_lanes(w['mla_k_norm'][l], MLA_QK))
    y_mla = _mla_branch(nm + 'mla', tb, pm, tabs_mla, mla_params)

    gk2 = w['gla_w_gk2'][l]
    rank = gk2.shape[1]
    w2f = jnp.pad(gk2[0], ((0, LANES - rank), (0, 0)))
    w2b = jnp.pad(gk2[1], ((rank, LANES - 2 * rank), (0, 0)))
    y_gla = _gla_branch(nm + 'gla', tb, pg, (w2f, w2b, w['gla_b_gk'][l], row(w['gla_o_norm'][l])))

    log_g = -jnp.exp(w['ret_decay'][l])
    lg = jnp.broadcast_to(log_g[:, :, None, None], (2, RET_HEADS, 1, LANES))
    y_ret = _ret_branch(nm + 'ret', tb, pr, tabs_ret, lg)

    wbr = w['w_branch'][l]
    wb_mla = jnp.pad(wbr[0].reshape(MLA_HEADS, MLA_V, D_MODEL), ((0, 0), (0, LANES - MLA_V), (0, 0)))
    z0 = _mm_op(nm + 'br_mla', y_mla, wb_mla.reshape(MLA_HEADS * LANES, D_MODEL))
    z1 = _mm_op(nm + 'br_gla', y_gla, wbr[1])
    z2 = _mm_op(nm + 'br_ret', y_ret, wbr[2])
    u = _merge_op((nm + 'merge',), tb, z0, z1, z2, pgate, row(w['b_gate'][l]))[0]
    y = _mm_op(nm + 'out', u, w['w_out'][l])
    h = _resid_op((nm + 'res1', 2), tb, h, y, mod16, b_ada)[0]

    a2 = _normmod_op((nm + 'norm2', 3, 4), tb, h, row(w['norm2_w'][l]), mod16, b_ada)[0]
    gu = _mm_op(nm + 'ffn_in', a2, w['w_ffn_in'][l])
    act = _convact(nm + 'convact', tb, gu, w['w_dw'][l], row(w['b_dw'][l]))
    f = _mm_op(nm + 'ffn_out', act, w['w_ffn_out'][l])
    return _resid_op((nm + 'res2', 5), tb, h, f, mod16, b_ada)[0]


def _local_loss(w, x, c, ctx, target):
    seq, tb = x.shape[1], ctx.shape[1]
    h = jnp.concatenate([ctx[0], x[0]], axis=0)
    cond_in = jnp.concatenate([w['c_ctx'].reshape(1, -1), c, jnp.zeros((14, D_MODEL), F32)], axis=0)
    cond16 = _silu_op(('cond_silu',), 16, cond_in)[0]
    tabs_mla = _mla_tables(seq, tb)
    tabs_ret = _ret_tables(seq + tb)
    for l in range(DEPTH):
        mod16 = _mm_op('l%d_ada' % l, cond16, w['w_ada'][l])
        h = _layer(l, w, h, mod16, tabs_mla, tabs_ret, tb)
    return _loss_op(tb, h, target[0])


ANY = pl.BlockSpec(memory_space=pl.ANY)
FLAT_W = 1024


def _my_place():
    return lax.axis_index('x'), lax.axis_index('y'), lax.axis_index('c')


def _other_chips(x, y):
    return [(1 - x, y), (x, 1 - y), (1 - x, 1 - y)]


def _gather_chips(name, arrs):
    n = len(arrs)

    def body(*refs):
        ins, outs = refs[:n], refs[n:2 * n]
        send_sems, recv_sems, local_sems = refs[2 * n:]
        x, y, c = _my_place()
        me = 2 * x + y
        copies = []
        for a in range(n):
            cp = pltpu.make_async_copy(ins[a], outs[a].at[me], local_sems.at[a])
            cp.start()
            copies.append(cp)
        sends = []
        for j, (px, py) in enumerate(_other_chips(x, y)):
            for a in range(n):
                s = pltpu.make_async_remote_copy(src_ref=ins[a], dst_ref=outs[a].at[me], send_sem=send_sems.at[j, a],
                                                 recv_sem=recv_sems.at[j, a], device_id=(px, py, c), device_id_type=MESH)
                s.start()
                sends.append(s)
        for j, (px, py) in enumerate(_other_chips(x, y)):
            for a in range(n):
                pltpu.make_async_remote_copy(src_ref=ins[a], dst_ref=outs[a].at[2 * px + py], send_sem=send_sems.at[j, a],
                                             recv_sem=recv_sems.at[j, a], device_id=(px, py, c),
                                             device_id_type=MESH).wait_recv()
        for s in sends:
            s.wait_send()
        for cp in copies:
            cp.wait()

    return pl.pallas_call(
        body, name=name, in_specs=[ANY] * n, out_specs=[ANY] * n,
        out_shape=[jax.ShapeDtypeStruct((N_CHIPS,) + a.shape, a.dtype) for a in arrs],
        scratch_shapes=[pltpu.SemaphoreType.DMA((3, n)), pltpu.SemaphoreType.DMA((3, n)), pltpu.SemaphoreType.DMA((n,))],
    )(*arrs)


def _sibling_send(name, arr):
    def body(in_ref, out_ref, send_sem, recv_sem):
        x, y, c = _my_place()
        cp = pltpu.make_async_remote_copy(src_ref=in_ref, dst_ref=out_ref, send_sem=send_sem, recv_sem=recv_sem,
                                          device_id=(x, y, 1 - c), device_id_type=MESH)
        cp.start()
        cp.wait()

    return pl.pallas_call(
        body, name=name, in_specs=[ANY], out_specs=ANY, out_shape=jax.ShapeDtypeStruct(arr.shape, arr.dtype),
        scratch_shapes=[pltpu.SemaphoreType.DMA, pltpu.SemaphoreType.DMA],
    )(arr)


def _chip_all_to_all(name, arr):
    def body(in_ref, out_ref, send_sems, recv_sems, local_sem):
        x, y, c = _my_place()
        me = 2 * x + y
        mine = pltpu.make_async_copy(in_ref.at[me], out_ref.at[me], local_sem)
        mine.start()
        sends = []
        for j, (px, py) in enumerate(_other_chips(x, y)):
            s = pltpu.make_async_remote_copy(src_ref=in_ref.at[2 * px + py], dst_ref=out_ref.at[me],
                                             send_sem=send_sems.at[j], recv_sem=recv_sems.at[j],
                                             device_id=(px, py, c), device_id_type=MESH)
            s.start()
            sends.append(s)
        for j, (px, py) in enumerate(_other_chips(x, y)):
            pltpu.make_async_remote_copy(src_ref=in_ref.at[me], dst_ref=out_ref.at[2 * px + py],
                                         send_sem=send_sems.at[j], recv_sem=recv_sems.at[j],
                                         device_id=(px, py, c), device_id_type=MESH).wait_recv()
        for s in sends:
            s.wait_send()
        mine.wait()

    return pl.pallas_call(
        body, name=name, in_specs=[ANY], out_specs=ANY, out_shape=jax.ShapeDtypeStruct(arr.shape, arr.dtype),
        scratch_shapes=[pltpu.SemaphoreType.DMA((3,)), pltpu.SemaphoreType.DMA((3,)), pltpu.SemaphoreType.DMA],
    )(arr)


def _gather_all(name, arr):
    def body(in_ref, out_ref, send_sems, recv_sems, local_sem):
        x, y, c = _my_place()
        me = 4 * x + 2 * y + c
        mine = pltpu.make_async_copy(in_ref, out_ref.at[me], local_sem)
        mine.start()
        peers = []
        for k in range(1, N_DEV):
            px = (1 - x) if k & 4 else x
            py = (1 - y) if k & 2 else y
            pc = (1 - c) if k & 1 else c
            peers.append((px, py, pc))
        sends = []
        for k, peer in enumerate(peers):
            s = pltpu.make_async_remote_copy(src_ref=in_ref, dst_ref=out_ref.at[me], send_sem=send_sems.at[k],
                                             recv_sem=recv_sems.at[k], device_id=peer, device_id_type=MESH)
            s.start()
            sends.append(s)
        for k, (px, py, pc) in enumerate(peers):
            pltpu.make_async_remote_copy(src_ref=in_ref, dst_ref=out_ref.at[4 * px + 2 * py + pc],
                                         send_sem=send_sems.at[k], recv_sem=recv_sems.at[k], device_id=(px, py, pc),
                                         device_id_type=MESH).wait_recv()
        for s in sends:
            s.wait_send()
        mine.wait()

    return pl.pallas_call(
        body, name=name, in_specs=[ANY], out_specs=ANY, out_shape=jax.ShapeDtypeStruct((N_DEV,) + arr.shape, arr.dtype),
        scratch_shapes=[pltpu.SemaphoreType.DMA((N_DEV - 1,)), pltpu.SemaphoreType.DMA((N_DEV - 1,)),
                        pltpu.SemaphoreType.DMA],
    )(arr)


def _flat_rows(r):
    return _tile(r, 512, 8)


def _add2(name, a, b):
    _, r, wd = a.shape
    tr = _flat_rows(r)

    def body(a_ref, b_ref, o_ref):
        o_ref[...] = a_ref[...] + b_ref[...]

    spec = pl.BlockSpec((1, tr, wd), lambda s, i: (s, i, 0))
    return pl.pallas_call(body, name=name, grid=(a.shape[0], r // tr), in_specs=[spec, spec], out_specs=spec,
                          out_shape=jax.ShapeDtypeStruct(a.shape, F32), compiler_params=_cparams(("arbitrary",) * 2))(a, b)


def _sum_rows(name, a):
    n, r, wd = a.shape
    tr = _flat_rows(r)

    def body(a_ref, o_ref):
        acc = a_ref[0]
        for k in range(1, n):
            acc = acc + a_ref[k]
        o_ref[...] = acc

    return pl.pallas_call(body, name=name, grid=(r // tr,), in_specs=[pl.BlockSpec((n, tr, wd), lambda i: (0, i, 0))],
                          out_specs=pl.BlockSpec((tr, wd), lambda i: (i, 0)), out_shape=jax.ShapeDtypeStruct((r, wd), F32),
                          compiler_params=_cparams(("arbitrary",)))(a)


def _adamw(name, w, g, m, v):
    r, wd = w.shape
    tr = _flat_rows(r)

    def body(w_ref, g_ref, m_ref, v_ref, d_ref, nm_ref, nv_ref):
        g_ = g_ref[...]
        m_ = ADAM_B1 * m_ref[...] + (1.0 - ADAM_B1) * g_
        v_ = ADAM_B2 * v_ref[...] + (1.0 - ADAM_B2) * (g_ * g_)
        m_hat = m_ / (1.0 - ADAM_B1 ** ADAM_STEP)
        v_hat = v_ / (1.0 - ADAM_B2 ** ADAM_STEP)
        d_ref[...] = -ADAM_LR * (m_hat / (jnp.sqrt(v_hat) + ADAM_EPS) + ADAM_WD * w_ref[...])
        nm_ref[...] = m_
        nv_ref[...] = v_

    spec = pl.BlockSpec((tr, wd), lambda i: (i, 0))
    return pl.pallas_call(body, name=name, grid=(r // tr,), in_specs=[spec] * 4, out_specs=[spec] * 3,
                          out_shape=[jax.ShapeDtypeStruct((r, wd), F32)] * 3,
                          compiler_params=_cparams(("arbitrary",)))(w, g, m, v)


def _to_flat(parts, dtype, row_multiple):
    flat = jnp.concatenate([p.astype(dtype).reshape(-1) for p in parts])
    unit = FLAT_W * row_multiple
    total = -(-flat.shape[0] // unit) * unit
    return jnp.pad(flat, (0, total - flat.shape[0])).reshape(total // FLAT_W, FLAT_W)


def _from_flat(flat, shapes):
    flat = flat.reshape(-1)
    out, at = [], 0
    for shp in shapes:
        n = 1
        for d in shp:
            n *= d
        out.append(flat[at:at + n].reshape(shp))
        at += n
    return out


def _shard_piece(a, axis, s):
    n = a.shape[axis] // N_CHIPS
    return lax.slice_in_dim(a, s * n, (s + 1) * n, axis=axis)


def kernel(x, c, ctx, c_ctx, w_ada, b_ada, norm1_w, norm2_w, w_in, b_gate, mla_q_norm_a, mla_w_qb, mla_kv_norm_a, mla_w_kvb, mla_q_norm, mla_k_norm, gla_w_gk2, gla_b_gk, gla_o_norm, ret_decay, w_branch, w_out, w_ffn_in, w_dw, b_dw, w_ffn_out, loss_target, m_c_ctx, m_w_ada, m_b_ada, m_norm1_w, m_norm2_w, m_w_in, m_b_gate, m_mla_q_norm_a, m_mla_w_qb, m_mla_kv_norm_a, m_mla_w_kvb, m_mla_q_norm, m_mla_k_norm, m_gla_w_gk2, m_gla_b_gk, m_gla_o_norm, m_ret_decay, m_w_branch, m_w_out, m_w_ffn_in, m_w_dw, m_b_dw, m_w_ffn_out, v_c_ctx, v_w_ada, v_b_ada, v_norm1_w, v_norm2_w, v_w_in, v_b_gate, v_mla_q_norm_a, v_mla_w_qb, v_mla_kv_norm_a, v_mla_w_kvb, v_mla_q_norm, v_mla_k_norm, v_gla_w_gk2, v_gla_b_gk, v_gla_o_norm, v_ret_decay, v_w_branch, v_w_out, v_w_ffn_in, v_w_dw, v_b_dw, v_w_ffn_out):
    local = dict(c_ctx=c_ctx, w_ada=w_ada, b_ada=b_ada, norm1_w=norm1_w, norm2_w=norm2_w, w_in=w_in, b_gate=b_gate,
                 mla_q_norm_a=mla_q_norm_a, mla_w_qb=mla_w_qb, mla_kv_norm_a=mla_kv_norm_a, mla_w_kvb=mla_w_kvb,
                 mla_q_norm=mla_q_norm, mla_k_norm=mla_k_norm, gla_w_gk2=gla_w_gk2, gla_b_gk=gla_b_gk,
                 gla_o_norm=gla_o_norm, ret_decay=ret_decay, w_branch=w_branch, w_out=w_out, w_ffn_in=w_ffn_in,
                 w_dw=w_dw, b_dw=b_dw, w_ffn_out=w_ffn_out)
    mom_m = dict(c_ctx=m_c_ctx, w_ada=m_w_ada, b_ada=m_b_ada, norm1_w=m_norm1_w, norm2_w=m_norm2_w, w_in=m_w_in,
                 b_gate=m_b_gate, mla_q_norm_a=m_mla_q_norm_a, mla_w_qb=m_mla_w_qb, mla_kv_norm_a=m_mla_kv_norm_a,
                 mla_w_kvb=m_mla_w_kvb, mla_q_norm=m_mla_q_norm, mla_k_norm=m_mla_k_norm, gla_w_gk2=m_gla_w_gk2,
                 gla_b_gk=m_gla_b_gk, gla_o_norm=m_gla_o_norm, ret_decay=m_ret_decay, w_branch=m_w_branch,
                 w_out=m_w_out, w_ffn_in=m_w_ffn_in, w_dw=m_w_dw, b_dw=m_b_dw, w_ffn_out=m_w_ffn_out)
    mom_v = dict(c_ctx=v_c_ctx, w_ada=v_w_ada, b_ada=v_b_ada, norm1_w=v_norm1_w, norm2_w=v_norm2_w, w_in=v_w_in,
                 b_gate=v_b_gate, mla_q_norm_a=v_mla_q_norm_a, mla_w_qb=v_mla_w_qb, mla_kv_norm_a=v_mla_kv_norm_a,
                 mla_w_kvb=v_mla_w_kvb, mla_q_norm=v_mla_q_norm, mla_k_norm=v_mla_k_norm, gla_w_gk2=v_gla_w_gk2,
                 gla_b_gk=v_gla_b_gk, gla_o_norm=v_gla_o_norm, ret_decay=v_ret_decay, w_branch=v_w_branch,
                 w_out=v_w_out, w_ffn_in=v_w_ffn_in, w_dw=v_w_dw, b_dw=v_b_dw, w_ffn_out=v_w_ffn_out)
    sharded = [n for n, _ in SHARDED]
    axis_of = dict(SHARDED)
    narrow = [n for n in sharded if n not in SHARDED_F32]

    gathered_b, gathered_f = _gather_chips('gather_weights', [
        _to_flat([local[n] for n in narrow], MXU_DTYPE, 16), _to_flat([local[n] for n in SHARDED_F32], F32, 8)])
    full = {n: local[n] for n in REPLICATED}
    for names, gathered in ((narrow, gathered_b), (SHARDED_F32, gathered_f)):
        pieces = [_from_flat(gathered[s], [local[n].shape for n in names]) for s in range(N_CHIPS)]
        for k, n in enumerate(names):
            full[n] = jnp.concatenate([pieces[s][k] for s in range(N_CHIPS)], axis=axis_of[n]).astype(F32)

    loss_local, (grad_full, grad_x) = jax.value_and_grad(_local_loss, argnums=(0, 1))(full, x, c, ctx, loss_target)
    loss = lax.psum(loss_local, ('x', 'y', 'c'))

    my_c = lax.axis_index('c')
    g_flat = jnp.stack([_to_flat([_shard_piece(grad_full[n], axis_of[n], s) for n in sharded], F32, 16)
                        for s in range(N_CHIPS)])
    rows = g_flat.shape[1]
    half = rows // 2
    keep = lax.dynamic_slice_in_dim(g_flat, my_c * half, half, axis=1)
    give = lax.dynamic_slice_in_dim(g_flat, (1 - my_c) * half, half, axis=1)
    chip_sum = _add2('grad_add_sibling', keep, _sibling_send('grad_swap_halves', give))
    mine_half = _sum_rows('grad_sum_chips', _chip_all_to_all('grad_all_to_all', chip_sum))
    other_half = _sibling_send('grad_swap_result', mine_half)
    lo = jnp.where(my_c == 0, mine_half, other_half)
    hi = jnp.where(my_c == 0, other_half, mine_half)
    g_shard = jnp.concatenate([lo, hi], axis=0)

    g_rep = _sum_rows('grad_sum_replicated', _gather_all('grad_gather_replicated',
                                                          _to_flat([grad_full[n] for n in REPLICATED], F32, 8)))

    upd_s = _adamw('adamw_sharded', *[_to_flat([d[n] for n in sharded], F32, 16) for d in (local,)], g_shard,
                   *[_to_flat([d[n] for n in sharded], F32, 16) for d in (mom_m, mom_v)])
    upd_r = _adamw('adamw_replicated', *[_to_flat([d[n] for n in REPLICATED], F32, 8) for d in (local,)], g_rep,
                   *[_to_flat([d[n] for n in REPLICATED], F32, 8) for d in (mom_m, mom_v)])

    results = {}
    for names, grads, upd in ((sharded, g_shard, upd_s), (REPLICATED, g_rep, upd_r)):
        shapes = [local[n].shape for n in names]
        for kind, flat in zip(('grad', 'delta', 'new_m', 'new_v'), (grads,) + tuple(upd)):
            for n, val in zip(names, _from_flat(flat, shapes)):
                results[kind, n] = val
    out = [loss, grad_x]
    for kind in ('grad', 'delta', 'new_m', 'new_v'):
        out += [results[kind, n] for n in WEIGHT_ORDER]
    return tuple(out)
```

```python
import functools

import jax
import jax.numpy as jnp
from jax import lax
from jax.experimental import pallas as pl
from jax.experimental.pallas import tpu as pltpu

F32 = jnp.float32
MXU_DTYPE = jnp.bfloat16

DEPTH = 2
D_MODEL = 1024
GRID_W = 64
CHUNK = 64
LANES = 128
MLA_HEADS = 8
MLA_NOPE = 64
MLA_ROPE = 32
MLA_QK = MLA_NOPE + MLA_ROPE
MLA_V = 64
GLA_HEADS = 4
GLA_DK = 128
GLA_GATE_NORMALIZER = 16.0
RET_HEADS = 4
RET_DK = 128
D_FF = 2816
ROPE_THETA = 10000.0
RET_THETA = 10000.0
EPS = 1e-6
ADAM_LR = 0.001
ADAM_B1 = 0.9
ADAM_B2 = 0.999
ADAM_EPS = 1e-08
ADAM_WD = 0.01
ADAM_STEP = 10
NEG_BIG = -1e30

VMEM_LIMIT_BYTES = 56 * 1024 * 1024
WEIGHT_BLOCK_BYTES = 8 * 1024 * 1024
ACC_BLOCK_BYTES = 13 * 1024 * 1024
ACC_MAX_ROWS = 2816
MM_ROWS = 512
LOG2E = 1.4426950408889634
LN2 = 0.6931471805599453

SHARDED = (('w_ada', 2), ('w_in', 2), ('b_gate', 2), ('mla_w_qb', 2), ('mla_w_kvb', 2), ('gla_w_gk2', 3),
           ('gla_b_gk', 2), ('w_branch', 3), ('w_out', 1), ('w_ffn_in', 2), ('w_dw', 2), ('w_ffn_out', 1))
SHARDED_F32 = ('b_gate', 'gla_b_gk', 'w_dw')
REPLICATED = ('c_ctx', 'b_ada', 'norm1_w', 'norm2_w', 'mla_q_norm_a', 'mla_kv_norm_a', 'mla_q_norm', 'mla_k_norm',
              'gla_o_norm', 'ret_decay', 'b_dw')
WEIGHT_ORDER = ('c_ctx', 'w_ada', 'b_ada', 'norm1_w', 'norm2_w', 'w_in', 'b_gate', 'mla_q_norm_a', 'mla_w_qb',
                'mla_kv_norm_a', 'mla_w_kvb', 'mla_q_norm', 'mla_k_norm', 'gla_w_gk2', 'gla_b_gk', 'gla_o_norm',
                'ret_decay', 'w_branch', 'w_out', 'w_ffn_in', 'w_dw', 'b_dw', 'w_ffn_out')
N_CHIPS = 4
N_DEV = 8
MESH = pl.DeviceIdType.MESH


def _cparams(sem):
    return pltpu.CompilerParams(dimension_semantics=sem, vmem_limit_bytes=VMEM_LIMIT_BYTES)


def _tile(n, target, unit):
    best = None
    for t in range(unit, min(n, target) + 1, unit):
        if n % t == 0:
            best = t
    return n if best is None else best


_DN = {'nn': (((1,), (0,)), ((), ())), 'nt': (((1,), (1,)), ((), ())), 'tn': (((0,), (0,)), ((), ()))}


def _raw_mm(x, y, form):
    return lax.dot_general(x.astype(MXU_DTYPE), y.astype(MXU_DTYPE), _DN[form], preferred_element_type=F32)


@functools.partial(jax.custom_vjp, nondiff_argnums=(2,))
def _mm(x, y, form):
    return _raw_mm(x, y, form)


def _mm_fwd(x, y, form):
    return _raw_mm(x, y, form), (x, y)


def _mm_bwd(form, res, g):
    x, y = res
    if form == 'nn':
        dx, dy = _mm(g, y, 'nt'), _mm(x, g, 'tn')
    elif form == 'nt':
        dx, dy = _mm(g, y, 'nn'), _mm(g, x, 'tn')
    else:
        dx, dy = _mm(y, g, 'nt'), _mm(x, g, 'nn')
    return dx.astype(x.dtype), dy.astype(y.dtype)


_mm.defvjp(_mm_fwd, _mm_bwd)


@functools.partial(jax.custom_vjp, nondiff_argnums=(1, 2))
def _roll(x, shift, axis):
    return pltpu.roll(x, shift, axis)


def _roll_fwd(x, shift, axis):
    return pltpu.roll(x, shift, axis), None


def _roll_bwd(shift, axis, _, g):
    return (pltpu.roll(g, (g.shape[axis] - shift) % g.shape[axis], axis),)


_roll.defvjp(_roll_fwd, _roll_bwd)


def _tri(n, upper):
    i = lax.broadcasted_iota(jnp.int32, (n, n), 0)
    j = lax.broadcasted_iota(jnp.int32, (n, n), 1)
    return jnp.where((j >= i) if upper else (j <= i), 1.0, 0.0).astype(F32)


def _tri_mm(n, upper, x):
    return jnp.dot(_tri(n, upper), x, precision=lax.Precision.HIGHEST, preferred_element_type=F32)


@functools.partial(jax.custom_vjp, nondiff_argnums=(1,))
def _cumsum_rows(x, reverse):
    return _tri_mm(x.shape[0], reverse, x)


def _cumsum_fwd(x, reverse):
    return _tri_mm(x.shape[0], reverse, x), None


def _cumsum_bwd(reverse, _, g):
    return (_tri_mm(g.shape[0], not reverse, g),)


_cumsum_rows.defvjp(_cumsum_fwd, _cumsum_bwd)


def _rms(x, n=None):
    n = x.shape[-1] if n is None else n
    return x * lax.rsqrt(jnp.sum(x * x, axis=-1, keepdims=True) / n + EPS)


def _mod_row(i, mod16, b_ada):
    m = mod16[0:8] + b_ada
    return jnp.where(i == 0, m[0:1], m[1:2])


def _row_spec(tb, spec):
    arr, cb, width = spec
    return pl.BlockSpec((tb, width), lambda i, cb=cb: (i, cb))


def _whole_spec(arr):
    nd = arr.ndim
    return pl.BlockSpec(arr.shape, lambda i, nd=nd: (0,) * nd)


def _rw_fwd(name, fn, rows, params, outs, tb):
    t = rows[0][0].shape[0]
    nr, npar = len(rows), len(params)

    def body(*refs):
        i = pl.program_id(0)
        rv = [r[...] for r in refs[:nr]]
        pv = [p[...] for p in refs[nr:nr + npar]]
        res = fn(i, rv, pv)
        for o_ref, val in zip(refs[nr + npar:], res):
            o_ref[...] = val.astype(o_ref.dtype)

    return pl.pallas_call(
        body, name=name, grid=(t // tb,),
        in_specs=[_row_spec(tb, s) for s in rows] + [_whole_spec(p) for p in params],
        out_specs=[pl.BlockSpec((tb, w), lambda i: (i, 0)) for w, _ in outs],
        out_shape=[jax.ShapeDtypeStruct((t, w), dt) for w, dt in outs],
        compiler_params=_cparams(("arbitrary",)),
    )(*[s[0] for s in rows], *params)


def _rw_bwd(name, fn, rows, params, gouts, tb, diff_rows):
    t = rows[0][0].shape[0]
    nr, npar, ng, nd = len(rows), len(params), len(gouts), len(diff_rows)

    def body(*refs):
        i = pl.program_id(0)
        rv = [r[...] for r in refs[:nr]]
        pv = [p[...] for p in refs[nr:nr + npar]]
        gv = [g[...].astype(F32) for g in refs[nr + npar:nr + npar + ng]]
        out_refs = refs[nr + npar + ng:]

        def f(dr, pvals):
            vals = list(rv)
            for k, idx in enumerate(diff_rows):
                vals[idx] = dr[k]
            return tuple(fn(i, vals, pvals))

        _, vjp = jax.vjp(f, [rv[k].astype(F32) for k in diff_rows], pv)
        drows, dpars = vjp(tuple(gv))
        for k in range(nd):
            out_refs[k][...] = drows[k]

        @pl.when(i == 0)
        def _():
            for k in range(npar):
                out_refs[nd + k][...] = jnp.zeros_like(out_refs[nd + k])

        for k in range(npar):
            out_refs[nd + k][...] += dpars[k]

    res = pl.pallas_call(
        body, name=name, grid=(t // tb,),
        in_specs=([_row_spec(tb, s) for s in rows] + [_whole_spec(p) for p in params]
                  + [pl.BlockSpec((tb, g.shape[1]), lambda i: (i, 0)) for g in gouts]),
        out_specs=([pl.BlockSpec((tb, rows[k][2]), lambda i: (i, 0)) for k in diff_rows]
                   + [_whole_spec(p) for p in params]),
        out_shape=([jax.ShapeDtypeStruct((t, rows[k][2]), F32) for k in diff_rows]
                   + [jax.ShapeDtypeStruct(p.shape, F32) for p in params]),
        compiler_params=_cparams(("arbitrary",)),
    )(*[s[0] for s in rows], *params, *gouts)
    return list(res[:nd]), list(res[nd:])


def _full(arr):
    return (arr, 0, arr.shape[1])


def _make_rw_op(fn_factory, n_rows, diff_rows, out_widths):
    @functools.partial(jax.custom_vjp, nondiff_argnums=(0, 1))
    def op(cfg, tb, *args):
        return tuple(_rw_fwd(cfg[0] + '_fwd', fn_factory(cfg), [_full(a) for a in args[:n_rows]], list(args[n_rows:]),
                             [(w, F32) for w in out_widths(cfg, args)], tb))

    def fwd(cfg, tb, *args):
        return op(cfg, tb, *args), args

    def bwd(cfg, tb, args, g):
        drows, dpars = _rw_bwd(cfg[0] + '_bwd', fn_factory(cfg), [_full(a) for a in args[:n_rows]],
                               list(args[n_rows:]), list(g), tb, diff_rows)
        full = [jnp.zeros_like(a) for a in args[:n_rows]]
        for k, idx in enumerate(diff_rows):
            full[idx] = drows[k]
        return tuple(full) + tuple(dpars)

    op.defvjp(fwd, bwd)
    return op


def _silu_fn(cfg):
    return lambda i, rows, params: (jax.nn.silu(rows[0]),)


_silu_op = _make_rw_op(_silu_fn, 1, (0,), lambda cfg, args: (args[0].shape[1],))


def _normmod_fn(cfg):
    _, shift_at, scale_at = cfg

    def fn(i, rows, params):
        (h,) = rows
        nw, mod16, b_ada = params
        mr = _mod_row(i, mod16, b_ada)
        d = h.shape[1]
        return (_rms(h) * nw * (1.0 + mr[:, scale_at * d:(scale_at + 1) * d]) + mr[:, shift_at * d:(shift_at + 1) * d],)

    return fn


_normmod_op = _make_rw_op(_normmod_fn, 1, (0,), lambda cfg, args: (args[0].shape[1],))


def _resid_fn(cfg):
    _, gate_at = cfg

    def fn(i, rows, params):
        h, y = rows
        mod16, b_ada = params
        mr = _mod_row(i, mod16, b_ada)
        d = h.shape[1]
        return (h + mr[:, gate_at * d:(gate_at + 1) * d] * y,)

    return fn


_resid_op = _make_rw_op(_resid_fn, 2, (0, 1), lambda cfg, args: (args[0].shape[1],))


def _merge_fn(cfg):
    def fn(i, rows, params):
        z0, z1, z2, pg = rows
        (bg,) = params
        d = z0.shape[1]
        out = None
        for n, z in enumerate((z0, z1, z2)):
            term = jax.nn.sigmoid(pg[:, n * d:(n + 1) * d] + bg[:, n * d:(n + 1) * d]) * z
            out = term if out is None else out + term
        return (out,)

    return fn


_merge_op = _make_rw_op(_merge_fn, 4, (0, 1, 2, 3), lambda cfg, args: (args[0].shape[1],))


def _matmul(name, a, b, form):
    if form == 'nn':
        (m, k), (_, n) = a.shape, b.shape
        tm = _tile(m, MM_ROWS, 8)
        tn = _tile(n, max(LANES, WEIGHT_BLOCK_BYTES // (k * b.dtype.itemsize)), LANES)

        def body(a_ref, b_ref, o_ref):
            o_ref[...] = _raw_mm(a_ref[...], b_ref[...], 'nn')

        return pl.pallas_call(
            body, name=name, grid=(n // tn, m // tm),
            in_specs=[pl.BlockSpec((tm, k), lambda j, i: (i, 0)), pl.BlockSpec((k, tn), lambda j, i: (0, j))],
            out_specs=pl.BlockSpec((tm, tn), lambda j, i: (i, j)),
            out_shape=jax.ShapeDtypeStruct((m, n), F32), compiler_params=_cparams(("arbitrary", "arbitrary")),
        )(a, b)
    if form == 'nt':
        (m, n), (k, _) = a.shape, b.shape
        tm = _tile(m, MM_ROWS, 8)
        tk = _tile(k, max(LANES, WEIGHT_BLOCK_BYTES // (n * b.dtype.itemsize)), LANES)

        def body(a_ref, b_ref, o_ref):
            o_ref[...] = _raw_mm(a_ref[...], b_ref[...], 'nt')

        return pl.pallas_call(
            body, name=name, grid=(k // tk, m // tm),
            in_specs=[pl.BlockSpec((tm, n), lambda j, i: (i, 0)), pl.BlockSpec((tk, n), lambda j, i: (j, 0))],
            out_specs=pl.BlockSpec((tm, tk), lambda j, i: (i, j)),
            out_shape=jax.ShapeDtypeStruct((m, k), F32), compiler_params=_cparams(("arbitrary", "arbitrary")),
        )(a, b)
    (m, ka), (_, n) = a.shape, b.shape
    tka = _tile(ka, ACC_MAX_ROWS, LANES)
    tn, tmc = _tile(n, max(LANES, ACC_BLOCK_BYTES // (4 * tka)), LANES), _tile(m, MM_ROWS, 8)

    def body(a_ref, b_ref, o_ref):
        @pl.when(pl.program_id(2) == 0)
        def _():
            o_ref[...] = jnp.zeros_like(o_ref)

        o_ref[...] += _raw_mm(a_ref[...], b_ref[...], 'tn')

    return pl.pallas_call(
        body, name=name, grid=(ka // tka, n // tn, m // tmc),
        in_specs=[pl.BlockSpec((tmc, tka), lambda i, j, s: (s, i)), pl.BlockSpec((tmc, tn), lambda i, j, s: (s, j))],
        out_specs=pl.BlockSpec((tka, tn), lambda i, j, s: (i, j)),
        out_shape=jax.ShapeDtypeStruct((ka, n), F32), compiler_params=_cparams(("arbitrary", "arbitrary", "arbitrary")),
    )(a, b)


@functools.partial(jax.custom_vjp, nondiff_argnums=(0,))
def _mm_op(name, a, w):
    return _matmul(name + '_fwd', a, w.astype(MXU_DTYPE), 'nn')


def _mm_op_fwd(name, a, w):
    wb = w.astype(MXU_DTYPE)
    return _matmul(name + '_fwd', a, wb, 'nn'), (a, wb)


def _mm_op_bwd(name, res, g):
    a, wb = res
    return _matmul(name + '_da', g, wb, 'nt'), _matmul(name + '_dw', a, g, 'tn')


_mm_op.defvjp(_mm_op_fwd, _mm_op_bwd)


def _rope128(x, c, a, b):
    return x * c + _roll(x, LANES - 8, 1) * a + _roll(x, 8, 1) * b


def _mla_prep_fn(i, rows, params):
    pm, c, a, b = rows
    qna, wqb, kvna, wkn, wv, qn, kn = params
    cq, ckv, kr_slot = pm[:, 0:256], pm[:, 256:384], pm[:, 384:512]
    q_all = _mm(_rms(cq) * qna, wqb, 'nn')
    ckvn = _rms(ckv) * kvna
    k_all = _mm(ckvn, wkn, 'nn')
    v_all = _mm(ckvn, wv, 'nn')
    qs, ks = [], []
    for h in range(MLA_HEADS):
        sl = slice(LANES * h, LANES * (h + 1))
        qs.append(_rope128(_rms(q_all[:, sl], MLA_QK) * qn, c, a, b))
        ks.append(_rope128(_rms(k_all[:, sl] + kr_slot, MLA_QK) * kn, c, a, b))
    return jnp.concatenate(qs, axis=1), jnp.concatenate(ks, axis=1), v_all


def _attn_fwd(name, q, k, v, tb, ctx_len):
    t = q.shape[0]
    scale = MLA_QK ** -0.5

    def body(q_ref, k_ref, v_ref, o_ref, lse_ref):
        qi = pl.program_id(1)

        def attend(k, v):
            s2 = _raw_mm(q_ref[...], k, 'nt') * (scale * LOG2E)
            m2 = jnp.max(s2, axis=-1, keepdims=True)
            p = jnp.exp2(s2 - m2)
            l = jnp.sum(p, axis=-1, keepdims=True)
            o_ref[...] = _raw_mm(p, v, 'nn') / l
            lse_ref[...] = jnp.broadcast_to((m2 + jnp.log2(l)) * LN2, lse_ref.shape)

        @pl.when(qi == 0)
        def _():
            attend(k_ref[0:ctx_len, :], v_ref[0:ctx_len, :])

        @pl.when(qi != 0)
        def _():
            attend(k_ref[...], v_ref[...])

    blk = pl.BlockSpec((tb, LANES), lambda h, i: (i, h))
    whole = pl.BlockSpec((t, LANES), lambda h, i: (0, h))
    return pl.pallas_call(
        body, name=name, grid=(MLA_HEADS, t // tb), in_specs=[blk, whole, whole], out_specs=[blk, blk],
        out_shape=[jax.ShapeDtypeStruct(q.shape, F32)] * 2, compiler_params=_cparams(("arbitrary", "arbitrary")),
    )(q, k, v)


def _attn_bwd(name, q, k, v, o, lse, do, tb, ctx_len):
    t = q.shape[0]
    scale = MLA_QK ** -0.5
    ck = _tile(t, 2816, 256)

    def body(q_ref, k_ref, v_ref, o_ref, lse_ref, do_ref, dq_ref, dk_ref, dv_ref):
        qi = pl.program_id(1)

        @pl.when(qi == 0)
        def _():
            dk_ref[...] = jnp.zeros_like(dk_ref)
            dv_ref[...] = jnp.zeros_like(dv_ref)

        q = q_ref[...]
        do = do_ref[...].astype(MXU_DTYPE)
        lse2 = lse_ref[...][:, 0:1] * LOG2E
        delta = jnp.sum(do_ref[...] * o_ref[...], axis=-1, keepdims=True)

        def part(rows):
            ks, vs = k_ref[rows, :], v_ref[rows, :]
            p = jnp.exp2(_raw_mm(q, ks, 'nt') * (scale * LOG2E) - lse2)
            ds = p * ((_raw_mm(do, vs, 'nt') - delta) * scale)
            dk_ref[rows, :] += _raw_mm(ds, q, 'tn')
            dv_ref[rows, :] += _raw_mm(p, do, 'tn')
            return _raw_mm(ds, ks, 'nn')

        @pl.when(qi == 0)
        def _():
            dq_ref[...] = part(pl.ds(0, ctx_len))

        @pl.when(qi != 0)
        def _():
            dq = part(pl.ds(0, ck))
            for c in range(1, t // ck):
                dq = dq + part(pl.ds(c * ck, ck))
            dq_ref[...] = dq

    blk = pl.BlockSpec((tb, LANES), lambda h, i: (i, h))
    whole = pl.BlockSpec((t, LANES), lambda h, i: (0, h))
    return pl.pallas_call(
        body, name=name, grid=(MLA_HEADS, t // tb), in_specs=[blk, whole, whole, blk, blk, blk],
        out_specs=[blk, whole, whole], out_shape=[jax.ShapeDtypeStruct(q.shape, F32)] * 3,
        compiler_params=_cparams(("arbitrary", "arbitrary")),
    )(q, k, v, o, lse, do)


def _mla_rows(pm, tabs):
    return [(pm, 0, pm.shape[1])] + [_full(x) for x in tabs]


@functools.partial(jax.custom_vjp, nondiff_argnums=(0, 1))
def _mla_branch(name, tb, pm, tabs, params):
    return _mla_branch_fwd(name, tb, pm, tabs, params)[0]


def _mla_branch_fwd(name, tb, pm, tabs, params):
    w = LANES * MLA_HEADS
    q, k, v = _rw_fwd(name + '_prep', _mla_prep_fn, _mla_rows(pm, tabs), list(params), [(w, MXU_DTYPE)] * 3, tb)
    o, lse = _attn_fwd(name + '_attn', q, k, v, tb, tb)
    return o, (pm, tabs, params, q, k, v, o, lse)


def _mla_branch_bwd(name, tb, res, do):
    pm, tabs, params, q, k, v, o, lse = res
    dq, dk, dv = _attn_bwd(name + '_attn_bwd', q, k, v, o, lse, do, tb, tb)
    (dpm,), dpars = _rw_bwd(name + '_prep_bwd', _mla_prep_fn, _mla_rows(pm, tabs), list(params), [dq, dk, dv], tb, (0,))
    return dpm, tuple(jnp.zeros_like(x) for x in tabs), tuple(dpars)


_mla_branch.defvjp(_mla_branch_fwd, _mla_branch_bwd)


def _chunk_masks(reverse):
    i = lax.broadcasted_iota(jnp.int32, (CHUNK, CHUNK), 0)
    j = lax.broadcasted_iota(jnp.int32, (CHUNK, CHUNK), 1)
    return i, j, ((j > i) if reverse else (j <= i))


def _gla_chunk(reverse, rows, params, st0):
    q, k, v, la = rows
    q = q * (GLA_DK ** -0.5)
    cum = _cumsum_rows(la, reverse)
    tot = cum[0:1] if reverse else cum[CHUNK - 1:CHUNK]
    st1 = st0 * jnp.exp(tot) + _mm(v, k * jnp.exp(tot - cum), 'tn')
    qd = q * jnp.exp(cum)
    _, _, mask = _chunk_masks(reverse)
    att = jnp.where(mask, _mm(qd, k * jnp.exp(-cum), 'nt'), 0.0)
    return _mm(att, v, 'nn') + _mm(qd, st0, 'nt'), st1


def _ret_chunk(reverse, rows, params, st0):
    q, k, v, cc, ss = rows
    (lg,) = params
    q = q * cc + _roll(q, RET_DK // 2, 1) * ss
    k = (k * cc + _roll(k, RET_DK // 2, 1) * ss) * (RET_DK ** -0.5)
    r = lax.broadcasted_iota(jnp.int32, (CHUNK, LANES), 0).astype(F32)
    zeta = jnp.exp((r if reverse else (CHUNK - 1.0 - r)) * lg)
    xi = jnp.exp(((CHUNK - r) if reverse else (r + 1.0)) * lg)
    st1 = st0 * jnp.exp(CHUNK * lg) + _mm(v, k * zeta, 'tn')
    i, j, mask = _chunk_masks(reverse)
    rel = jnp.where(mask, (j - i) if reverse else (i - j), 0).astype(F32)
    dmat = jnp.where(mask, jnp.exp(rel * lg[:, 0:CHUNK]), 0.0)
    att = _mm(q, k, 'nt') * dmat
    return _mm(att, v, 'nn') + _mm(q, st0, 'nt') * xi, st1


def _scan_order(reverse, nblk):
    if reverse:
        return lambda t: jnp.where(t == 0, 0, nblk - t)
    return lambda t: t


def _scan_specs(rows, params, tb, heads, blk_of):
    def rspec(spec):
        _, cb = spec
        if cb is None:
            return pl.BlockSpec((tb, LANES), lambda s: (blk_of(s), 0))
        return pl.BlockSpec((tb, heads * LANES), lambda s, cb=cb: (blk_of(s), cb // heads))

    return [rspec(s) for s in rows] + [pl.BlockSpec((heads, 1, LANES), lambda s: (0, 0, 0)) for _ in params]


def _head_rows(row_refs, rows, sl, h):
    lanes = pl.ds(h * LANES, LANES)
    return [r[sl, :] if spec[1] is None else r[sl, lanes] for r, spec in zip(row_refs, rows)]


def _scan_fwd(name, chunk_fn, reverse, rows, params, tb, heads):
    t = rows[0][0].shape[0]
    nblk, cpb = t // tb, tb // CHUNK
    blk_of = _scan_order(reverse, nblk)
    nr, npar = len(rows), len(params)
    order = list(range(cpb))[::-1] if reverse else list(range(cpb))

    def body(*refs):
        row_refs, par_refs = refs[:nr], refs[nr:nr + npar]
        o_ref, st_out_ref, st_ref = refs[nr + npar:]

        @pl.when(pl.program_id(0) == 0)
        def _():
            st_ref[...] = jnp.zeros_like(st_ref)

        for c in order:
            sl = pl.ds(c * CHUNK, CHUNK)
            for h in range(heads):
                st0 = st_ref[h]
                st_out_ref[h, c] = st0
                o, st1 = chunk_fn(reverse, _head_rows(row_refs, rows, sl, h), [p[h] for p in par_refs], st0)
                o_ref[sl, pl.ds(h * LANES, LANES)] = o
                st_ref[h] = st1

    return pl.pallas_call(
        body, name=name, grid=(nblk,), in_specs=_scan_specs(rows, params, tb, heads, blk_of),
        out_specs=[pl.BlockSpec((tb, heads * LANES), lambda s: (blk_of(s), 0)),
                   pl.BlockSpec((heads, cpb, LANES, LANES), lambda s: (0, blk_of(s), 0, 0))],
        out_shape=[jax.ShapeDtypeStruct((t, heads * LANES), F32),
                   jax.ShapeDtypeStruct((heads, t // CHUNK, LANES, LANES), F32)],
        scratch_shapes=[pltpu.VMEM((heads, LANES, LANES), F32)],
        compiler_params=_cparams(("arbitrary",)),
    )(*[s[0] for s in rows], *params)


def _scan_bwd(name, chunk_fn, reverse, rows, params, states, do, tb, heads, n_diff):
    t = rows[0][0].shape[0]
    nblk, cpb = t // tb, tb // CHUNK
    fwd_blk = _scan_order(reverse, nblk)
    blk_of = lambda s: fwd_blk(nblk - 1 - s)
    nr, npar = len(rows), len(params)
    order = list(range(cpb)) if reverse else list(range(cpb))[::-1]

    def body(*refs):
        row_refs, par_refs = refs[:nr], refs[nr:nr + npar]
        st_in_ref, do_ref = refs[nr + npar:nr + npar + 2]
        out_refs = refs[nr + npar + 2:-1]
        dst_ref = refs[-1]

        @pl.when(pl.program_id(0) == 0)
        def _():
            dst_ref[...] = jnp.zeros_like(dst_ref)
            for k in range(npar):
                out_refs[n_diff + k][...] = jnp.zeros_like(out_refs[n_diff + k])

        for c in order:
            sl = pl.ds(c * CHUNK, CHUNK)
            for h in range(heads):
                lanes = pl.ds(h * LANES, LANES)
                rv = _head_rows(row_refs, rows, sl, h)

                def f(dr, pvals, st0, rv=rv):
                    return chunk_fn(reverse, list(dr) + rv[n_diff:], pvals, st0)

                _, vjp = jax.vjp(f, rv[:n_diff], [p[h] for p in par_refs], st_in_ref[h, c])
                drows, dpars, dst0 = vjp((do_ref[sl, lanes], dst_ref[h]))
                for k in range(n_diff):
                    out_refs[k][sl, lanes] = drows[k]
                for k in range(npar):
                    out_refs[n_diff + k][h] += dpars[k]
                dst_ref[h] = dst0

    wide = pl.BlockSpec((tb, heads * LANES), lambda s: (blk_of(s), 0))
    pblk = pl.BlockSpec((heads, 1, LANES), lambda s: (0, 0, 0))
    res = pl.pallas_call(
        body, name=name, grid=(nblk,),
        in_specs=(_scan_specs(rows, params, tb, heads, blk_of)
                  + [pl.BlockSpec((heads, cpb, LANES, LANES), lambda s: (0, blk_of(s), 0, 0)), wide]),
        out_specs=[wide] * n_diff + [pblk for _ in params],
        out_shape=([jax.ShapeDtypeStruct((t, heads * LANES), F32)] * n_diff
                   + [jax.ShapeDtypeStruct(p.shape, F32) for p in params]),
        scratch_shapes=[pltpu.VMEM((heads, LANES, LANES), F32)],
        compiler_params=_cparams(("arbitrary",)),
    )(*[s[0] for s in rows], *params, states, do)
    return list(res[:n_diff]), list(res[n_diff:])


def _gla_la_fn(i, rows, params):
    (r,) = rows
    w2f, w2b, bgk = params
    la_f = jax.nn.log_sigmoid(_mm(r, w2f, 'nn') + bgk[0:1]) / GLA_GATE_NORMALIZER
    la_b = jax.nn.log_sigmoid(_mm(r, w2b, 'nn') + bgk[1:2]) / GLA_GATE_NORMALIZER
    return la_f, la_b


def _headnorm_fn(heads, with_weight):
    def fn(i, rows, params):
        o_f, o_b, g = rows
        outs = []
        for h in range(heads):
            sl = slice(LANES * h, LANES * (h + 1))
            y = _rms(o_f[:, sl] + o_b[:, sl])
            outs.append(y * params[0] if with_weight else y)
        return (jnp.concatenate(outs, axis=1) * jax.nn.silu(g),)

    return fn


@functools.partial(jax.custom_vjp, nondiff_argnums=(0, 1))
def _gla_branch(name, tb, pg, params):
    return _gla_branch_fwd(name, tb, pg, params)[0]


def _gla_rows(pg, la):
    return [(pg, 0), (pg, GLA_HEADS), (pg, 2 * GLA_HEADS), (la, 0)]


def _gla_branch_fwd(name, tb, pg, params):
    w2f, w2b, bgk, onorm = params
    w = GLA_HEADS * LANES
    la_f, la_b = _rw_fwd(name + '_la', _gla_la_fn, [(pg, 4 * w // LANES, LANES)], [w2f, w2b, bgk], [(w, F32)] * 2, tb)
    o_f, st_f = _scan_fwd(name + '_scan_f', _gla_chunk, False, _gla_rows(pg, la_f), [], tb, GLA_HEADS)
    o_b, st_b = _scan_fwd(name + '_scan_b', _gla_chunk, True, _gla_rows(pg, la_b), [], tb, GLA_HEADS)
    (y,) = _rw_fwd(name + '_norm', _headnorm_fn(GLA_HEADS, True), [_full(o_f), _full(o_b), (pg, 3, w)], [onorm],
                   [(w, F32)], tb)
    return y, (pg, params, la_f, la_b, o_f, o_b, st_f, st_b)


def _gla_branch_bwd(name, tb, res, dy):
    pg, params, la_f, la_b, o_f, o_b, st_f, st_b = res
    w2f, w2b, bgk, onorm = params
    w = GLA_HEADS * LANES
    (do_f, do_b, dg), (donorm,) = _rw_bwd(name + '_norm_bwd', _headnorm_fn(GLA_HEADS, True),
                                          [_full(o_f), _full(o_b), (pg, 3, w)], [onorm], [dy], tb, (0, 1, 2))
    (dq_f, dk_f, dv_f, dla_f), _ = _scan_bwd(name + '_scan_f_bwd', _gla_chunk, False, _gla_rows(pg, la_f), [], st_f,
                                             do_f, tb, GLA_HEADS, 4)
    (dq_b, dk_b, dv_b, dla_b), _ = _scan_bwd(name + '_scan_b_bwd', _gla_chunk, True, _gla_rows(pg, la_b), [], st_b,
                                             do_b, tb, GLA_HEADS, 4)
    (dr,), (dw2f, dw2b, dbgk) = _rw_bwd(name + '_la_bwd', _gla_la_fn, [(pg, 4 * w // LANES, LANES)], [w2f, w2b, bgk],
                                        [dla_f, dla_b], tb, (0,))
    dpg = jnp.concatenate([dq_f + dq_b, dk_f + dk_b, dv_f + dv_b, dg, dr], axis=1)
    return dpg, (dw2f, dw2b, dbgk, donorm)


_gla_branch.defvjp(_gla_branch_fwd, _gla_branch_bwd)


@functools.partial(jax.custom_vjp, nondiff_argnums=(0, 1))
def _ret_branch(name, tb, pr, tabs, lg):
    return _ret_branch_fwd(name, tb, pr, tabs, lg)[0]


def _ret_rows(pr, tabs):
    return [(pr, 0), (pr, RET_HEADS), (pr, 2 * RET_HEADS), (tabs[0], None), (tabs[1], None)]


def _ret_branch_fwd(name, tb, pr, tabs, lg):
    w = RET_HEADS * LANES
    o_f, st_f = _scan_fwd(name + '_scan_f', _ret_chunk, False, _ret_rows(pr, tabs), [lg[0]], tb, RET_HEADS)
    o_b, st_b = _scan_fwd(name + '_scan_b', _ret_chunk, True, _ret_rows(pr, tabs), [lg[1]], tb, RET_HEADS)
    (y,) = _rw_fwd(name + '_norm', _headnorm_fn(RET_HEADS, False), [_full(o_f), _full(o_b), (pr, 3, w)], [],
                   [(w, F32)], tb)
    return y, (pr, tabs, lg, o_f, o_b, st_f, st_b)


def _ret_branch_bwd(name, tb, res, dy):
    pr, tabs, lg, o_f, o_b, st_f, st_b = res
    w = RET_HEADS * LANES
    (do_f, do_b, dg), _ = _rw_bwd(name + '_norm_bwd', _headnorm_fn(RET_HEADS, False),
                                  [_full(o_f), _full(o_b), (pr, 3, w)], [], [dy], tb, (0, 1, 2))
    (dq_f, dk_f, dv_f), (dlg_f,) = _scan_bwd(name + '_scan_f_bwd', _ret_chunk, False, _ret_rows(pr, tabs), [lg[0]],
                                             st_f, do_f, tb, RET_HEADS, 3)
    (dq_b, dk_b, dv_b), (dlg_b,) = _scan_bwd(name + '_scan_b_bwd', _ret_chunk, True, _ret_rows(pr, tabs), [lg[1]],
                                             st_b, do_b, tb, RET_HEADS, 3)
    dpr = jnp.concatenate([dq_f + dq_b, dk_f + dk_b, dv_f + dv_b, dg], axis=1)
    return dpr, tuple(jnp.zeros_like(x) for x in tabs), jnp.stack([dlg_f, dlg_b])


_ret_branch.defvjp(_ret_branch_fwd, _ret_branch_bwd)


HALO = 8


def _halo_specs(tb, nblk, width, col_block):
    r = tb // HALO
    prev = pl.BlockSpec((HALO, width), lambda i: (jnp.maximum(i * r - 1, 0), col_block))
    nxt = pl.BlockSpec((HALO, width), lambda i: (jnp.minimum((i + 1) * r, nblk * r - 1), col_block))
    return prev, nxt


def _shifted(x, prev_blk, next_blk, i, nblk):
    tb = x.shape[0]
    row = lax.broadcasted_iota(jnp.int32, x.shape, 0)
    prev_row = jnp.where(i >= 2, prev_blk[HALO - 1:HALO], 0.0)
    next_row = jnp.where((i >= 1) & (i < nblk - 1), next_blk[0:1], 0.0)
    down = jnp.where(row == 0, prev_row, pltpu.roll(x, 1, 0))
    up = jnp.where(row == tb - 1, next_row, pltpu.roll(x, tb - 1, 0))
    return down, up


def _gelu_up(c, up):
    return jax.nn.gelu(c) * up


def _convact_fwd_call(name, gu, w_dw, b_dw, tb):
    t = gu.shape[0]
    nblk = t // tb
    prev_spec, next_spec = _halo_specs(tb, nblk, D_FF, 0)

    def body(g_ref, up_ref, prev_ref, next_ref, w_ref, b_ref, o_ref):
        i = pl.program_id(0)
        g = g_ref[...]
        down, upw = _shifted(g, prev_ref[...], next_ref[...], i, nblk)
        w = w_ref[...]
        c = w[0:1] * down + w[1:2] * g + w[2:3] * upw + b_ref[...]
        o_ref[...] = _gelu_up(c, up_ref[...])

    return pl.pallas_call(
        body, name=name, grid=(nblk,),
        in_specs=[pl.BlockSpec((tb, D_FF), lambda i: (i, 0)), pl.BlockSpec((tb, D_FF), lambda i: (i, 1)), prev_spec,
                  next_spec, _whole_spec(w_dw), _whole_spec(b_dw)],
        out_specs=pl.BlockSpec((tb, D_FF), lambda i: (i, 0)), out_shape=jax.ShapeDtypeStruct((t, D_FF), F32),
        compiler_params=_cparams(("arbitrary",)),
    )(gu, gu, gu, gu, w_dw, b_dw)


def _convact_bwd_calls(name, gu, w_dw, b_dw, dact, tb):
    t = gu.shape[0]
    nblk = t // tb
    prev_spec, next_spec = _halo_specs(tb, nblk, D_FF, 0)

    def body1(g_ref, up_ref, prev_ref, next_ref, w_ref, b_ref, da_ref, dc_ref, dup_ref, dw_ref, db_ref):
        i = pl.program_id(0)
        g = g_ref[...]
        down, upw = _shifted(g, prev_ref[...], next_ref[...], i, nblk)
        w = w_ref[...]
        c = w[0:1] * down + w[1:2] * g + w[2:3] * upw + b_ref[...]
        _, vjp = jax.vjp(_gelu_up, c, up_ref[...])
        dc, dup = vjp(da_ref[...])
        dc_ref[...] = dc
        dup_ref[...] = dup

        @pl.when(i == 0)
        def _():
            dw_ref[...] = jnp.zeros_like(dw_ref)
            db_ref[...] = jnp.zeros_like(db_ref)

        dw_ref[0:1, :] += jnp.sum(dc * down, axis=0, keepdims=True)
        dw_ref[1:2, :] += jnp.sum(dc * g, axis=0, keepdims=True)
        dw_ref[2:3, :] += jnp.sum(dc * upw, axis=0, keepdims=True)
        db_ref[...] += jnp.sum(dc, axis=0, keepdims=True)

    blk = pl.BlockSpec((tb, D_FF), lambda i: (i, 0))
    dc, dup, dw, db = pl.pallas_call(
        body1, name=name + '_a', grid=(nblk,),
        in_specs=[blk, pl.BlockSpec((tb, D_FF), lambda i: (i, 1)), prev_spec, next_spec, _whole_spec(w_dw),
                  _whole_spec(b_dw), blk],
        out_specs=[blk, blk, _whole_spec(w_dw), _whole_spec(b_dw)],
        out_shape=[jax.ShapeDtypeStruct((t, D_FF), F32)] * 2 + [jax.ShapeDtypeStruct(w_dw.shape, F32),
                                                                jax.ShapeDtypeStruct(b_dw.shape, F32)],
        compiler_params=_cparams(("arbitrary",)),
    )(gu, gu, gu, gu, w_dw, b_dw, dact)

    def body2(dc_ref, prev_ref, next_ref, dup_ref, w_ref, o_ref):
        i = pl.program_id(0)
        dc_blk = dc_ref[...]
        down, upw = _shifted(dc_blk, prev_ref[...], next_ref[...], i, nblk)
        w = w_ref[...]
        o_ref[:, 0:D_FF] = w[0:1] * upw + w[1:2] * dc_blk + w[2:3] * down
        o_ref[:, D_FF:2 * D_FF] = dup_ref[...]

    dgu = pl.pallas_call(
        body2, name=name + '_b', grid=(nblk,),
        in_specs=[blk, prev_spec, next_spec, blk, _whole_spec(w_dw)],
        out_specs=pl.BlockSpec((tb, 2 * D_FF), lambda i: (i, 0)), out_shape=jax.ShapeDtypeStruct((t, 2 * D_FF), F32),
        compiler_params=_cparams(("arbitrary",)),
    )(dc, dc, dc, dup, w_dw)
    return dgu, dw, db


@functools.partial(jax.custom_vjp, nondiff_argnums=(0, 1))
def _convact(name, tb, gu, w_dw, b_dw):
    return _convact_fwd_call(name + '_fwd', gu, w_dw, b_dw, tb)


def _convact_fwd(name, tb, gu, w_dw, b_dw):
    return _convact_fwd_call(name + '_fwd', gu, w_dw, b_dw, tb), (gu, w_dw, b_dw)


def _convact_bwd(name, tb, res, dact):
    gu, w_dw, b_dw = res
    return _convact_bwd_calls(name + '_bwd', gu, w_dw, b_dw, dact, tb)


_convact.defvjp(_convact_fwd, _convact_bwd)


def _loss_fwd_call(h, target, tb):
    nlat = target.shape[0] // tb

    def body(h_ref, t_ref, o_ref):
        @pl.when(pl.program_id(0) == 0)
        def _():
            o_ref[...] = jnp.zeros_like(o_ref)

        e = h_ref[...] - t_ref[...]
        o_ref[...] += jnp.sum(e * e, axis=0, keepdims=True)

    cols = pl.pallas_call(
        body, name='loss_fwd', grid=(nlat,),
        in_specs=[pl.BlockSpec((tb, D_MODEL), lambda i: (i + 1, 0)), pl.BlockSpec((tb, D_MODEL), lambda i: (i, 0))],
        out_specs=pl.BlockSpec((1, D_MODEL), lambda i: (0, 0)), out_shape=jax.ShapeDtypeStruct((1, D_MODEL), F32),
        compiler_params=_cparams(("arbitrary",)),
    )(h, target)
    return (0.5 / D_MODEL) * jnp.sum(cols)


def _loss_bwd_call(h, target, gbar, tb):
    def body(g_ref, h_ref, t_ref, o_ref):
        live = jnp.where(pl.program_id(0) == 0, 0.0, g_ref[...] * (1.0 / D_MODEL))
        o_ref[...] = live * (h_ref[...] - t_ref[...])

    return pl.pallas_call(
        body, name='loss_bwd', grid=(h.shape[0] // tb,),
        in_specs=[pl.BlockSpec((1, 1), lambda i: (0, 0)), pl.BlockSpec((tb, D_MODEL), lambda i: (i, 0)),
                  pl.BlockSpec((tb, D_MODEL), lambda i: (jnp.maximum(i - 1, 0), 0))],
        out_specs=pl.BlockSpec((tb, D_MODEL), lambda i: (i, 0)), out_shape=jax.ShapeDtypeStruct(h.shape, F32),
        compiler_params=_cparams(("arbitrary",)),
    )(gbar.reshape(1, 1), h, target)


@functools.partial(jax.custom_vjp, nondiff_argnums=(0,))
def _loss_op(tb, h, target):
    return _loss_fwd_call(h, target, tb)


def _loss_op_fwd(tb, h, target):
    return _loss_fwd_call(h, target, tb), (h, target)


def _loss_op_bwd(tb, res, gbar):
    h, target = res
    return _loss_bwd_call(h, target, gbar, tb), jnp.zeros_like(target)


_loss_op.defvjp(_loss_op_fwd, _loss_op_bwd)


def _rope_tables(pos, dim, theta):
    inv = theta ** (-jnp.arange(dim // 2, dtype=F32) * 2.0 / dim)
    ang = pos.astype(F32)[:, None] * inv[None, :]
    return jnp.cos(ang), jnp.sin(ang)


def _mla_tables(seq, ctx_len):
    rows = seq // GRID_W
    row_pos = jnp.repeat(jnp.arange(rows), GRID_W)
    col_pos = jnp.tile(jnp.arange(GRID_W), rows)
    cos_r, sin_r = _rope_tables(row_pos, MLA_ROPE // 2, ROPE_THETA)
    cos_c, sin_c = _rope_tables(col_pos, MLA_ROPE // 2, ROPE_THETA)
    one = jnp.ones((seq, MLA_NOPE), F32)
    z8 = jnp.zeros((seq, 8), F32)
    pad1 = jnp.ones((seq, LANES - MLA_QK), F32)
    pad0 = jnp.zeros((seq, LANES - MLA_QK), F32)
    z64 = jnp.zeros((seq, MLA_NOPE), F32)
    c = jnp.concatenate([one, cos_r, cos_r, cos_c, cos_c, pad1], axis=1)
    a = jnp.concatenate([z64, -sin_r, z8, -sin_c, z8, pad0], axis=1)
    b = jnp.concatenate([z64, z8, sin_r, z8, sin_c, pad0], axis=1)
    ctx_rows = lambda fill: jnp.full((ctx_len, LANES), fill, F32)
    return (jnp.concatenate([ctx_rows(1.0), c]), jnp.concatenate([ctx_rows(0.0), a]),
            jnp.concatenate([ctx_rows(0.0), b]))


def _ret_tables(total):
    inv = 1.0 / (RET_THETA ** jnp.linspace(0.0, 1.0, RET_DK // 2, dtype=F32))
    ang = jnp.arange(total).astype(F32)[:, None] * inv[None, :]
    cos, sin = jnp.cos(ang), jnp.sin(ang)
    return jnp.concatenate([cos, cos], axis=1), jnp.concatenate([-sin, sin], axis=1)


def _head_slots(w, heads, width):
    k = w.shape[0]
    return jnp.pad(w.reshape(k, heads, width), ((0, 0), (0, 0), (0, LANES - width))).reshape(k, heads * LANES)


def _pad_lanes(v, width):
    return jnp.pad(v, (0, LANES - width)).reshape(1, LANES)


def _layer(l, w, h, mod16, tabs_mla, tabs_ret, tb):
    nm = 'l%d_' % l
    row = lambda v: v.reshape(1, -1)
    b_ada = row(w['b_ada'][l])
    w_in = w['w_in'][l]
    zc = lambda n: jnp.zeros((D_MODEL, n), F32)
    w_mla = jnp.concatenate([w_in[:, 0:384], zc(MLA_NOPE), w_in[:, 384:416], zc(LANES - MLA_QK)], axis=1)
    w_gla = jnp.concatenate([w_in[:, 416:2464], w_in[:, 2464:2496], zc(LANES - 32)], axis=1)
    w_ret = w_in[:, 2496:4544]
    w_gates = w_in[:, 4544:7616]

    a = _normmod_op((nm + 'norm1', 0, 1), tb, h, row(w['norm1_w'][l]), mod16, b_ada)[0]
    pm = _mm_op(nm + 'in_mla', a, w_mla)
    pg = _mm_op(nm + 'in_gla', a, w_gla)
    pr = _mm_op(nm + 'in_ret', a, w_ret)
    pgate = _mm_op(nm + 'in_gate', a, w_gates)

    kvb = w['mla_w_kvb'][l].reshape(-1, MLA_HEADS, MLA_NOPE + MLA_V)
    kdim = kvb.shape[0]
    mla_params = (row(w['mla_q_norm_a'][l]), _head_slots(w['mla_w_qb'][l], MLA_HEADS, MLA_QK),
                  row(w['mla_kv_norm_a'][l]), _head_slots(kvb[:, :, :MLA_NOPE].reshape(kdim, -1), MLA_HEADS, MLA_NOPE),
                  _head_slots(kvb[:, :, MLA_NOPE:].reshape(kdim, -1), MLA_HEADS, MLA_V),
                  _pad_lanes(w['mla_q_norm'][l], MLA_QK), _pad_lanes(w['mla_k_norm'][l], MLA_QK))
    y_mla = _mla_branch(nm + 'mla', tb, pm, tabs_mla, mla_params)

    gk2 = w['gla_w_gk2'][l]
    rank = gk2.shape[1]
    w2f = jnp.pad(gk2[0], ((0, LANES - rank), (0, 0)))
    w2b = jnp.pad(gk2[1], ((rank, LANES - 2 * rank), (0, 0)))
    y_gla = _gla_branch(nm + 'gla', tb, pg, (w2f, w2b, w['gla_b_gk'][l], row(w['gla_o_norm'][l])))

    log_g = -jnp.exp(w['ret_decay'][l])
    lg = jnp.broadcast_to(log_g[:, :, None, None], (2, RET_HEADS, 1, LANES))
    y_ret = _ret_branch(nm + 'ret', tb, pr, tabs_ret, lg)

    wbr = w['w_branch'][l]
    wb_mla = jnp.pad(wbr[0].reshape(MLA_HEADS, MLA_V, D_MODEL), ((0, 0), (0, LANES - MLA_V), (0, 0)))
    z0 = _mm_op(nm + 'br_mla', y_mla, wb_mla.reshape(MLA_HEADS * LANES, D_MODEL))
    z1 = _mm_op(nm + 'br_gla', y_gla, wbr[1])
    z2 = _mm_op(nm + 'br_ret', y_ret, wbr[2])
    u = _merge_op((nm + 'merge',), tb, z0, z1, z2, pgate, row(w['b_gate'][l]))[0]
    y = _mm_op(nm + 'out', u, w['w_out'][l])
    h = _resid_op((nm + 'res1', 2), tb, h, y, mod16, b_ada)[0]

    a2 = _normmod_op((nm + 'norm2', 3, 4), tb, h, row(w['norm2_w'][l]), mod16, b_ada)[0]
    gu = _mm_op(nm + 'ffn_in', a2, w['w_ffn_in'][l])
    act = _convact(nm + 'convact', tb, gu, w['w_dw'][l], row(w['b_dw'][l]))
    f = _mm_op(nm + 'ffn_out', act, w['w_ffn_out'][l])
    return _resid_op((nm + 'res2', 5), tb, h, f, mod16, b_ada)[0]


def _local_loss(w, x, c, ctx, target):
    seq, tb = x.shape[1], ctx.shape[1]
    h = jnp.concatenate([ctx[0], x[0]], axis=0)
    cond_in = jnp.concatenate([w['c_ctx'].reshape(1, -1), c, jnp.zeros((14, D_MODEL), F32)], axis=0)
    cond16 = _silu_op(('cond_silu',), 16, cond_in)[0]
    tabs_mla = _mla_tables(seq, tb)
    tabs_ret = _ret_tables(seq + tb)
    for l in range(DEPTH):
        mod16 = _mm_op('l%d_ada' % l, cond16, w['w_ada'][l])
        h = _layer(l, w, h, mod16, tabs_mla, tabs_ret, tb)
    return _loss_op(tb, h, target[0])


ANY = pl.BlockSpec(memory_space=pl.ANY)
FLAT_W = 1024
GRAD_ROW_MULTIPLE = 256


def _my_place():
    return lax.axis_index('x'), lax.axis_index('y'), lax.axis_index('c')


def _other_chips(x, y):
    return [(1 - x, y), (x, 1 - y), (1 - x, 1 - y)]


def _gather_chips(name, arrs):
    n = len(arrs)

    def body(*refs):
        ins, outs = refs[:n], refs[n:2 * n]
        send_sems, recv_sems, pass_send_sems, pass_recv_sems, local_sems = refs[2 * n:]
        x, y, c = _my_place()
        me = 2 * x + y
        chips = _other_chips(x, y)

        def half(a, which):
            h = arrs[a].shape[0] // 2
            return pl.ds(which * h, h)

        def ici(j, a, chip_slot, to):
            src = ins[a].at[half(a, c)] if chip_slot is None else outs[a].at[chip_slot, half(a, c)]
            return pltpu.make_async_remote_copy(
                src_ref=src, dst_ref=outs[a].at[me if chip_slot is None else chip_slot, half(a, c)],
                send_sem=send_sems.at[j, a], recv_sem=recv_sems.at[j, a], device_id=to, device_id_type=MESH)

        def passed(j, a, chip_slot, which):
            rows = outs[a].at[chip_slot, half(a, which)]
            return pltpu.make_async_remote_copy(src_ref=rows, dst_ref=rows, send_sem=pass_send_sems.at[j, a],
                                                recv_sem=pass_recv_sems.at[j, a], device_id=(x, y, 1 - c),
                                                device_id_type=MESH)

        copies = [pltpu.make_async_copy(ins[a], outs[a].at[me], local_sems.at[a]) for a in range(n)]
        sends = [ici(j, a, None, (px, py, c)) for j, (px, py) in enumerate(chips) for a in range(n)]
        for cp in copies + sends:
            cp.start()
        passes = []
        for j, (px, py) in enumerate(chips):
            for a in range(n):
                ici(j, a, 2 * px + py, (px, py, c)).wait_recv()
                p = passed(j, a, 2 * px + py, c)
                p.start()
                passes.append(p)
        for j, (px, py) in enumerate(chips):
            for a in range(n):
                passed(j, a, 2 * px + py, 1 - c).wait_recv()
        for s in sends + passes:
            s.wait_send()
        for cp in copies:
            cp.wait()

    sems = pltpu.SemaphoreType.DMA((3, n))
    return pl.pallas_call(
        body, name=name, in_specs=[ANY] * n, out_specs=[ANY] * n,
        out_shape=[jax.ShapeDtypeStruct((N_CHIPS,) + a.shape, a.dtype) for a in arrs],
        scratch_shapes=[sems, sems, sems, sems, pltpu.SemaphoreType.DMA((n,))],
    )(*arrs)


def _swap_halves(name, arr):
    half = arr.shape[1] // 2

    def body(in_ref, out_ref, send_sem, recv_sem):
        x, y, c = _my_place()
        cp = pltpu.make_async_remote_copy(src_ref=in_ref.at[:, pl.ds((1 - c) * half, half)], dst_ref=out_ref,
                                          send_sem=send_sem, recv_sem=recv_sem, device_id=(x, y, 1 - c),
                                          device_id_type=MESH)
        cp.start()
        cp.wait()

    return pl.pallas_call(
        body, name=name, in_specs=[ANY], out_specs=ANY,
        out_shape=jax.ShapeDtypeStruct((arr.shape[0], half, arr.shape[2]), arr.dtype),
        scratch_shapes=[pltpu.SemaphoreType.DMA, pltpu.SemaphoreType.DMA],
    )(arr)


def _share_halves(name, arr):
    def body(in_ref, out_ref, send_sem, recv_sem, local_sem):
        x, y, c = _my_place()
        mine = pltpu.make_async_copy(in_ref, out_ref.at[c], local_sem)
        mine.start()
        cp = pltpu.make_async_remote_copy(src_ref=in_ref, dst_ref=out_ref.at[c], send_sem=send_sem, recv_sem=recv_sem,
                                          device_id=(x, y, 1 - c), device_id_type=MESH)
        cp.start()
        pltpu.make_async_remote_copy(src_ref=in_ref, dst_ref=out_ref.at[1 - c], send_sem=send_sem, recv_sem=recv_sem,
                                     device_id=(x, y, 1 - c), device_id_type=MESH).wait_recv()
        cp.wait_send()
        mine.wait()

    return pl.pallas_call(
        body, name=name, in_specs=[ANY], out_specs=ANY, out_shape=jax.ShapeDtypeStruct((2,) + arr.shape, arr.dtype),
        scratch_shapes=[pltpu.SemaphoreType.DMA, pltpu.SemaphoreType.DMA, pltpu.SemaphoreType.DMA],
    )(arr)


def _chip_all_to_all(name, arr):
    def body(in_ref, out_ref, send_sems, recv_sems, local_sem):
        x, y, c = _my_place()
        me = 2 * x + y
        mine = pltpu.make_async_copy(in_ref.at[me], out_ref.at[me], local_sem)
        mine.start()
        sends = []
        for j, (px, py) in enumerate(_other_chips(x, y)):
            s = pltpu.make_async_remote_copy(src_ref=in_ref.at[2 * px + py], dst_ref=out_ref.at[me],
                                             send_sem=send_sems.at[j], recv_sem=recv_sems.at[j],
                                             device_id=(px, py, c), device_id_type=MESH)
            s.start()
            sends.append(s)
        for j, (px, py) in enumerate(_other_chips(x, y)):
            pltpu.make_async_remote_copy(src_ref=in_ref.at[me], dst_ref=out_ref.at[2 * px + py],
                                         send_sem=send_sems.at[j], recv_sem=recv_sems.at[j],
                                         device_id=(px, py, c), device_id_type=MESH).wait_recv()
        for s in sends:
            s.wait_send()
        mine.wait()

    return pl.pallas_call(
        body, name=name, in_specs=[ANY], out_specs=ANY, out_shape=jax.ShapeDtypeStruct(arr.shape, arr.dtype),
        scratch_shapes=[pltpu.SemaphoreType.DMA((3,)), pltpu.SemaphoreType.DMA((3,)), pltpu.SemaphoreType.DMA],
    )(arr)


def _gather_all(name, arr):
    def body(in_ref, out_ref, send_sems, recv_sems, local_sem):
        x, y, c = _my_place()
        me = 4 * x + 2 * y + c
        mine = pltpu.make_async_copy(in_ref, out_ref.at[me], local_sem)
        mine.start()
        peers = []
        for k in range(1, N_DEV):
            px = (1 - x) if k & 4 else x
            py = (1 - y) if k & 2 else y
            pc = (1 - c) if k & 1 else c
            peers.append((px, py, pc))
        sends = []
        for k, peer in enumerate(peers):
            s = pltpu.make_async_remote_copy(src_ref=in_ref, dst_ref=out_ref.at[me], send_sem=send_sems.at[k],
                                             recv_sem=recv_sems.at[k], device_id=peer, device_id_type=MESH)
            s.start()
            sends.append(s)
        for k, (px, py, pc) in enumerate(peers):
            pltpu.make_async_remote_copy(src_ref=in_ref, dst_ref=out_ref.at[4 * px + 2 * py + pc],
                                         send_sem=send_sems.at[k], recv_sem=recv_sems.at[k], device_id=(px, py, pc),
                                         device_id_type=MESH).wait_recv()
        for s in sends:
            s.wait_send()
        mine.wait()

    return pl.pallas_call(
        body, name=name, in_specs=[ANY], out_specs=ANY, out_shape=jax.ShapeDtypeStruct((N_DEV,) + arr.shape, arr.dtype),
        scratch_shapes=[pltpu.SemaphoreType.DMA((N_DEV - 1,)), pltpu.SemaphoreType.DMA((N_DEV - 1,)),
                        pltpu.SemaphoreType.DMA],
    )(arr)


def _flat_rows(r):
    return _tile(r, 512, 16)


def _add_my_half(name, whole, other, my_c, out_dtype):
    n, half, wd = other.shape
    tr = _flat_rows(half)
    nb = half // tr

    def body(c_ref, a_ref, b_ref, o_ref):
        o_ref[...] = (a_ref[...] + b_ref[...]).astype(o_ref.dtype)

    spec = pl.BlockSpec((1, tr, wd), lambda s, i, c: (s, i, 0))
    grid_spec = pltpu.PrefetchScalarGridSpec(
        num_scalar_prefetch=1, grid=(n, nb),
        in_specs=[pl.BlockSpec((1, tr, wd), lambda s, i, c: (s, c[0] * nb + i, 0)), spec], out_specs=spec)
    return pl.pallas_call(body, name=name, grid_spec=grid_spec, out_shape=jax.ShapeDtypeStruct(other.shape, out_dtype),
                          compiler_params=_cparams(("arbitrary",) * 2))(my_c.astype(jnp.int32).reshape(1), whole, other)


def _sum_rows(name, a):
    n, r, wd = a.shape
    tr = _flat_rows(r)

    def body(a_ref, o_ref):
        acc = a_ref[0].astype(F32)
        for k in range(1, n):
            acc = acc + a_ref[k].astype(F32)
        o_ref[...] = acc

    return pl.pallas_call(body, name=name, grid=(r // tr,), in_specs=[pl.BlockSpec((n, tr, wd), lambda i: (0, i, 0))],
                          out_specs=pl.BlockSpec((tr, wd), lambda i: (i, 0)), out_shape=jax.ShapeDtypeStruct((r, wd), F32),
                          compiler_params=_cparams(("arbitrary",)))(a)


def _adamw(name, w, g, m, v):
    r, wd = w.shape
    tr = _tile(r, 256, 8)

    def body(w_ref, g_ref, m_ref, v_ref, d_ref, nm_ref, nv_ref):
        g_ = g_ref[...]
        m_ = ADAM_B1 * m_ref[...] + (1.0 - ADAM_B1) * g_
        v_ = ADAM_B2 * v_ref[...] + (1.0 - ADAM_B2) * (g_ * g_)
        m_hat = m_ / (1.0 - ADAM_B1 ** ADAM_STEP)
        v_hat = v_ / (1.0 - ADAM_B2 ** ADAM_STEP)
        d_ref[...] = -ADAM_LR * (m_hat / (jnp.sqrt(v_hat) + ADAM_EPS) + ADAM_WD * w_ref[...])
        nm_ref[...] = m_
        nv_ref[...] = v_

    spec = pl.BlockSpec((tr, wd), lambda i: (i, 0))
    return pl.pallas_call(body, name=name, grid=(r // tr,), in_specs=[spec] * 4, out_specs=[spec] * 3,
                          out_shape=[jax.ShapeDtypeStruct((r, wd), F32)] * 3,
                          compiler_params=_cparams(("arbitrary",)))(w, g, m, v)


def _to_flat(parts, dtype, row_multiple):
    flat = jnp.concatenate([p.astype(dtype).reshape(-1) for p in parts])
    unit = FLAT_W * row_multiple
    total = -(-flat.shape[0] // unit) * unit
    return jnp.pad(flat, (0, total - flat.shape[0])).reshape(total // FLAT_W, FLAT_W)


def _from_flat(flat, shapes):
    flat = flat.reshape(-1)
    out, at = [], 0
    for shp in shapes:
        n = 1
        for d in shp:
            n *= d
        out.append(flat[at:at + n].reshape(shp))
        at += n
    return out


def _shard_piece(a, axis, s):
    n = a.shape[axis] // N_CHIPS
    return lax.slice_in_dim(a, s * n, (s + 1) * n, axis=axis)


def kernel(x, c, ctx, c_ctx, w_ada, b_ada, norm1_w, norm2_w, w_in, b_gate, mla_q_norm_a, mla_w_qb, mla_kv_norm_a, mla_w_kvb, mla_q_norm, mla_k_norm, gla_w_gk2, gla_b_gk, gla_o_norm, ret_decay, w_branch, w_out, w_ffn_in, w_dw, b_dw, w_ffn_out, loss_target, m_c_ctx, m_w_ada, m_b_ada, m_norm1_w, m_norm2_w, m_w_in, m_b_gate, m_mla_q_norm_a, m_mla_w_qb, m_mla_kv_norm_a, m_mla_w_kvb, m_mla_q_norm, m_mla_k_norm, m_gla_w_gk2, m_gla_b_gk, m_gla_o_norm, m_ret_decay, m_w_branch, m_w_out, m_w_ffn_in, m_w_dw, m_b_dw, m_w_ffn_out, v_c_ctx, v_w_ada, v_b_ada, v_norm1_w, v_norm2_w, v_w_in, v_b_gate, v_mla_q_norm_a, v_mla_w_qb, v_mla_kv_norm_a, v_mla_w_kvb, v_mla_q_norm, v_mla_k_norm, v_gla_w_gk2, v_gla_b_gk, v_gla_o_norm, v_ret_decay, v_w_branch, v_w_out, v_w_ffn_in, v_w_dw, v_b_dw, v_w_ffn_out):
    local = dict(c_ctx=c_ctx, w_ada=w_ada, b_ada=b_ada, norm1_w=norm1_w, norm2_w=norm2_w, w_in=w_in, b_gate=b_gate,
                 mla_q_norm_a=mla_q_norm_a, mla_w_qb=mla_w_qb, mla_kv_norm_a=mla_kv_norm_a, mla_w_kvb=mla_w_kvb,
                 mla_q_norm=mla_q_norm, mla_k_norm=mla_k_norm, gla_w_gk2=gla_w_gk2, gla_b_gk=gla_b_gk,
                 gla_o_norm=gla_o_norm, ret_decay=ret_decay, w_branch=w_branch, w_out=w_out, w_ffn_in=w_ffn_in,
                 w_dw=w_dw, b_dw=b_dw, w_ffn_out=w_ffn_out)
    mom_m = dict(c_ctx=m_c_ctx, w_ada=m_w_ada, b_ada=m_b_ada, norm1_w=m_norm1_w, norm2_w=m_norm2_w, w_in=m_w_in,
                 b_gate=m_b_gate, mla_q_norm_a=m_mla_q_norm_a, mla_w_qb=m_mla_w_qb, mla_kv_norm_a=m_mla_kv_norm_a,
                 mla_w_kvb=m_mla_w_kvb, mla_q_norm=m_mla_q_norm, mla_k_norm=m_mla_k_norm, gla_w_gk2=m_gla_w_gk2,
                 gla_b_gk=m_gla_b_gk, gla_o_norm=m_gla_o_norm, ret_decay=m_ret_decay, w_branch=m_w_branch,
                 w_out=m_w_out, w_ffn_in=m_w_ffn_in, w_dw=m_w_dw, b_dw=m_b_dw, w_ffn_out=m_w_ffn_out)
    mom_v = dict(c_ctx=v_c_ctx, w_ada=v_w_ada, b_ada=v_b_ada, norm1_w=v_norm1_w, norm2_w=v_norm2_w, w_in=v_w_in,
                 b_gate=v_b_gate, mla_q_norm_a=v_mla_q_norm_a, mla_w_qb=v_mla_w_qb, mla_kv_norm_a=v_mla_kv_norm_a,
                 mla_w_kvb=v_mla_w_kvb, mla_q_norm=v_mla_q_norm, mla_k_norm=v_mla_k_norm, gla_w_gk2=v_gla_w_gk2,
                 gla_b_gk=v_gla_b_gk, gla_o_norm=v_gla_o_norm, ret_decay=v_ret_decay, w_branch=v_w_branch,
                 w_out=v_w_out, w_ffn_in=v_w_ffn_in, w_dw=v_w_dw, b_dw=v_b_dw, w_ffn_out=v_w_ffn_out)
    sharded = [n for n, _ in SHARDED]
    axis_of = dict(SHARDED)
    narrow = [n for n in sharded if n not in SHARDED_F32]

    gathered_b, gathered_f = _gather_chips('gather_weights', [
        _to_flat([local[n] for n in narrow], MXU_DTYPE, 32), _to_flat([local[n] for n in SHARDED_F32], F32, 16)])
    full = {n: local[n] for n in REPLICATED}
    for names, gathered in ((narrow, gathered_b), (SHARDED_F32, gathered_f)):
        pieces = [_from_flat(gathered[s], [local[n].shape for n in names]) for s in range(N_CHIPS)]
        for k, n in enumerate(names):
            full[n] = jnp.concatenate([pieces[s][k] for s in range(N_CHIPS)], axis=axis_of[n]).astype(F32)

    loss_local, (grad_full, grad_x) = jax.value_and_grad(_local_loss, argnums=(0, 1))(full, x, c, ctx, loss_target)
    loss = lax.psum(loss_local, ('x', 'y', 'c'))

    my_c = lax.axis_index('c')
    g_flat = jnp.stack([_to_flat([_shard_piece(grad_full[n], axis_of[n], s) for n in sharded], F32, GRAD_ROW_MULTIPLE)
                        for s in range(N_CHIPS)])
    chip_sum = _add_my_half('grad_add_sibling', g_flat, _swap_halves('grad_swap_halves', g_flat), my_c, MXU_DTYPE)
    mine_half = _sum_rows('grad_sum_chips', _chip_all_to_all('grad_all_to_all', chip_sum))
    g_shard = _share_halves('grad_share_result', mine_half)

    g_rep = _sum_rows('grad_sum_replicated', _gather_all('grad_gather_replicated',
                                                          _to_flat([grad_full[n] for n in REPLICATED], F32, 8)))

    results = {}
    for n, g in zip(sharded, _from_flat(g_shard, [local[n].shape for n in sharded])):
        as_rows = lambda t: t.reshape(-1, t.shape[-1])
        upd = _adamw('adamw_' + n, as_rows(local[n]), as_rows(g), as_rows(mom_m[n]), as_rows(mom_v[n]))
        for kind, val in zip(('grad', 'delta', 'new_m', 'new_v'), (g,) + tuple(upd)):
            results[kind, n] = val.reshape(local[n].shape)
    upd_r = _adamw('adamw_replicated', *[_to_flat([d[n] for n in REPLICATED], F32, 8) for d in (local,)], g_rep,
                   *[_to_flat([d[n] for n in REPLICATED], F32, 8) for d in (mom_m, mom_v)])
    shapes = [local[n].shape for n in REPLICATED]
    for kind, flat in zip(('grad', 'delta', 'new_m', 'new_v'), (g_rep,) + tuple(upd_r)):
        for n, val in zip(REPLICATED, _from_flat(flat, shapes)):
            results[kind, n] = val
    out = [loss, grad_x]
    for kind in ('grad', 'delta', 'new_m', 'new_v'):
        out += [results[kind, n] for n in WEIGHT_ORDER]
    return tuple(out)
```

```python
import functools

import jax
import jax.numpy as jnp
from jax import lax
from jax.experimental import pallas as pl
from jax.experimental.pallas import tpu as pltpu

F32 = jnp.float32
MXU_DTYPE = jnp.bfloat16

DEPTH = 2
D_MODEL = 1024
GRID_W = 64
CHUNK = 64
LANES = 128
MLA_HEADS = 8
MLA_NOPE = 64
MLA_ROPE = 32
MLA_QK = MLA_NOPE + MLA_ROPE
MLA_V = 64
GLA_HEADS = 4
GLA_DK = 128
GLA_GATE_NORMALIZER = 16.0
RET_HEADS = 4
RET_DK = 128
D_FF = 2816
ROPE_THETA = 10000.0
RET_THETA = 10000.0
EPS = 1e-6
ADAM_LR = 0.001
ADAM_B1 = 0.9
ADAM_B2 = 0.999
ADAM_EPS = 1e-08
ADAM_WD = 0.01
ADAM_STEP = 10
NEG_BIG = -1e30

VMEM_LIMIT_BYTES = 56 * 1024 * 1024
WEIGHT_BLOCK_BYTES = 8 * 1024 * 1024
ACC_BLOCK_BYTES = 13 * 1024 * 1024
ACC_MAX_ROWS = 2816
MM_ROWS = 512
LOG2E = 1.4426950408889634
LN2 = 0.6931471805599453

SHARDED = (('w_ada', 2), ('w_in', 2), ('b_gate', 2), ('mla_w_qb', 2), ('mla_w_kvb', 2), ('gla_w_gk2', 3),
           ('gla_b_gk', 2), ('w_branch', 3), ('w_out', 1), ('w_ffn_in', 2), ('w_dw', 2), ('w_ffn_out', 1))
SHARDED_F32 = ('b_gate', 'gla_b_gk', 'w_dw')
REPLICATED = ('c_ctx', 'b_ada', 'norm1_w', 'norm2_w', 'mla_q_norm_a', 'mla_kv_norm_a', 'mla_q_norm', 'mla_k_norm',
              'gla_o_norm', 'ret_decay', 'b_dw')
WEIGHT_ORDER = ('c_ctx', 'w_ada', 'b_ada', 'norm1_w', 'norm2_w', 'w_in', 'b_gate', 'mla_q_norm_a', 'mla_w_qb',
                'mla_kv_norm_a', 'mla_w_kvb', 'mla_q_norm', 'mla_k_norm', 'gla_w_gk2', 'gla_b_gk', 'gla_o_norm',
                'ret_decay', 'w_branch', 'w_out', 'w_ffn_in', 'w_dw', 'b_dw', 'w_ffn_out')
N_CHIPS = 4
N_DEV = 8
MESH = pl.DeviceIdType.MESH


def _cparams(sem):
    return pltpu.CompilerParams(dimension_semantics=sem, vmem_limit_bytes=VMEM_LIMIT_BYTES)


def _tile(n, target, unit):
    best = None
    for t in range(unit, min(n, target) + 1, unit):
        if n % t == 0:
            best = t
    return n if best is None else best


_DN = {'nn': (((1,), (0,)), ((), ())), 'nt': (((1,), (1,)), ((), ())), 'tn': (((0,), (0,)), ((), ()))}


def _raw_mm(x, y, form):
    return lax.dot_general(x.astype(MXU_DTYPE), y.astype(MXU_DTYPE), _DN[form], preferred_element_type=F32)


@functools.partial(jax.custom_vjp, nondiff_argnums=(2,))
def _mm(x, y, form):
    return _raw_mm(x, y, form)


def _mm_fwd(x, y, form):
    return _raw_mm(x, y, form), (x, y)


def _mm_bwd(form, res, g):
    x, y = res
    if form == 'nn':
        dx, dy = _mm(g, y, 'nt'), _mm(x, g, 'tn')
    elif form == 'nt':
        dx, dy = _mm(g, y, 'nn'), _mm(g, x, 'tn')
    else:
        dx, dy = _mm(y, g, 'nt'), _mm(x, g, 'nn')
    return dx.astype(x.dtype), dy.astype(y.dtype)


_mm.defvjp(_mm_fwd, _mm_bwd)


@functools.partial(jax.custom_vjp, nondiff_argnums=(1, 2))
def _roll(x, shift, axis):
    return pltpu.roll(x, shift, axis)


def _roll_fwd(x, shift, axis):
    return pltpu.roll(x, shift, axis), None


def _roll_bwd(shift, axis, _, g):
    return (pltpu.roll(g, (g.shape[axis] - shift) % g.shape[axis], axis),)


_roll.defvjp(_roll_fwd, _roll_bwd)


def _tri(n, upper):
    i = lax.broadcasted_iota(jnp.int32, (n, n), 0)
    j = lax.broadcasted_iota(jnp.int32, (n, n), 1)
    return jnp.where((j >= i) if upper else (j <= i), 1.0, 0.0).astype(F32)


def _tri_mm(n, upper, x):
    return jnp.dot(_tri(n, upper), x, precision=lax.Precision.HIGHEST, preferred_element_type=F32)


@functools.partial(jax.custom_vjp, nondiff_argnums=(1,))
def _cumsum_rows(x, reverse):
    return _tri_mm(x.shape[0], reverse, x)


def _cumsum_fwd(x, reverse):
    return _tri_mm(x.shape[0], reverse, x), None


def _cumsum_bwd(reverse, _, g):
    return (_tri_mm(g.shape[0], not reverse, g),)


_cumsum_rows.defvjp(_cumsum_fwd, _cumsum_bwd)


def _rms(x, n=None):
    n = x.shape[-1] if n is None else n
    return x * lax.rsqrt(jnp.sum(x * x, axis=-1, keepdims=True) / n + EPS)


def _mod_row(i, mod16, b_ada):
    m = mod16[0:8] + b_ada
    return jnp.where(i == 0, m[0:1], m[1:2])


def _row_spec(tb, spec):
    arr, cb, width = spec
    return pl.BlockSpec((tb, width), lambda i, cb=cb: (i, cb))


def _whole_spec(arr):
    nd = arr.ndim
    return pl.BlockSpec(arr.shape, lambda i, nd=nd: (0,) * nd)


def _rw_fwd(name, fn, rows, params, outs, tb):
    t = rows[0][0].shape[0]
    nr, npar = len(rows), len(params)

    def body(*refs):
        i = pl.program_id(0)
        rv = [r[...] for r in refs[:nr]]
        pv = [p[...] for p in refs[nr:nr + npar]]
        res = fn(i, rv, pv)
        for o_ref, val in zip(refs[nr + npar:], res):
            o_ref[...] = val.astype(o_ref.dtype)

    return pl.pallas_call(
        body, name=name, grid=(t // tb,),
        in_specs=[_row_spec(tb, s) for s in rows] + [_whole_spec(p) for p in params],
        out_specs=[pl.BlockSpec((tb, w), lambda i: (i, 0)) for w, _ in outs],
        out_shape=[jax.ShapeDtypeStruct((t, w), dt) for w, dt in outs],
        compiler_params=_cparams(("arbitrary",)),
    )(*[s[0] for s in rows], *params)


def _rw_bwd(name, fn, rows, params, gouts, tb, diff_rows):
    t = rows[0][0].shape[0]
    nr, npar, ng, nd = len(rows), len(params), len(gouts), len(diff_rows)

    def body(*refs):
        i = pl.program_id(0)
        rv = [r[...] for r in refs[:nr]]
        pv = [p[...] for p in refs[nr:nr + npar]]
        gv = [g[...].astype(F32) for g in refs[nr + npar:nr + npar + ng]]
        out_refs = refs[nr + npar + ng:]

        def f(dr, pvals):
            vals = list(rv)
            for k, idx in enumerate(diff_rows):
                vals[idx] = dr[k]
            return tuple(fn(i, vals, pvals))

        _, vjp = jax.vjp(f, [rv[k].astype(F32) for k in diff_rows], pv)
        drows, dpars = vjp(tuple(gv))
        for k in range(nd):
            out_refs[k][...] = drows[k]

        @pl.when(i == 0)
        def _():
            for k in range(npar):
                out_refs[nd + k][...] = jnp.zeros_like(out_refs[nd + k])

        for k in range(npar):
            out_refs[nd + k][...] += dpars[k]

    res = pl.pallas_call(
        body, name=name, grid=(t // tb,),
        in_specs=([_row_spec(tb, s) for s in rows] + [_whole_spec(p) for p in params]
                  + [pl.BlockSpec((tb, g.shape[1]), lambda i: (i, 0)) for g in gouts]),
        out_specs=([pl.BlockSpec((tb, rows[k][2]), lambda i: (i, 0)) for k in diff_rows]
                   + [_whole_spec(p) for p in params]),
        out_shape=([jax.ShapeDtypeStruct((t, rows[k][2]), F32) for k in diff_rows]
                   + [jax.ShapeDtypeStruct(p.shape, F32) for p in params]),
        compiler_params=_cparams(("arbitrary",)),
    )(*[s[0] for s in rows], *params, *gouts)
    return list(res[:nd]), list(res[nd:])


def _full(arr):
    return (arr, 0, arr.shape[1])


def _make_rw_op(fn_factory, n_rows, diff_rows, out_widths):
    @functools.partial(jax.custom_vjp, nondiff_argnums=(0, 1))
    def op(cfg, tb, *args):
        return tuple(_rw_fwd(cfg[0] + '_fwd', fn_factory(cfg), [_full(a) for a in args[:n_rows]], list(args[n_rows:]),
                             [(w, F32) for w in out_widths(cfg, args)], tb))

    def fwd(cfg, tb, *args):
        return op(cfg, tb, *args), args

    def bwd(cfg, tb, args, g):
        drows, dpars = _rw_bwd(cfg[0] + '_bwd', fn_factory(cfg), [_full(a) for a in args[:n_rows]],
                               list(args[n_rows:]), list(g), tb, diff_rows)
        full = [jnp.zeros_like(a) for a in args[:n_rows]]
        for k, idx in enumerate(diff_rows):
            full[idx] = drows[k]
        return tuple(full) + tuple(dpars)

    op.defvjp(fwd, bwd)
    return op


def _silu_fn(cfg):
    return lambda i, rows, params: (jax.nn.silu(rows[0]),)


_silu_op = _make_rw_op(_silu_fn, 1, (0,), lambda cfg, args: (args[0].shape[1],))


def _normmod_fn(cfg):
    _, shift_at, scale_at = cfg

    def fn(i, rows, params):
        (h,) = rows
        nw, mod16, b_ada = params
        mr = _mod_row(i, mod16, b_ada)
        d = h.shape[1]
        return (_rms(h) * nw * (1.0 + mr[:, scale_at * d:(scale_at + 1) * d]) + mr[:, shift_at * d:(shift_at + 1) * d],)

    return fn


_normmod_op = _make_rw_op(_normmod_fn, 1, (0,), lambda cfg, args: (args[0].shape[1],))


def _resid_fn(cfg):
    _, gate_at = cfg

    def fn(i, rows, params):
        h, y = rows
        mod16, b_ada = params
        mr = _mod_row(i, mod16, b_ada)
        d = h.shape[1]
        return (h + mr[:, gate_at * d:(gate_at + 1) * d] * y,)

    return fn


_resid_op = _make_rw_op(_resid_fn, 2, (0, 1), lambda cfg, args: (args[0].shape[1],))


def _merge_fn(cfg):
    def fn(i, rows, params):
        z0, z1, z2, pg = rows
        (bg,) = params
        d = z0.shape[1]
        out = None
        for n, z in enumerate((z0, z1, z2)):
            term = jax.nn.sigmoid(pg[:, n * d:(n + 1) * d] + bg[:, n * d:(n + 1) * d]) * z
            out = term if out is None else out + term
        return (out,)

    return fn


_merge_op = _make_rw_op(_merge_fn, 4, (0, 1, 2, 3), lambda cfg, args: (args[0].shape[1],))


def _matmul(name, a, b, form):
    if form == 'nn':
        (m, k), (_, n) = a.shape, b.shape
        tm = _tile(m, MM_ROWS, 8)
        tn = _tile(n, max(LANES, WEIGHT_BLOCK_BYTES // (k * b.dtype.itemsize)), LANES)

        def body(a_ref, b_ref, o_ref):
            o_ref[...] = _raw_mm(a_ref[...], b_ref[...], 'nn')

        return pl.pallas_call(
            body, name=name, grid=(n // tn, m // tm),
            in_specs=[pl.BlockSpec((tm, k), lambda j, i: (i, 0)), pl.BlockSpec((k, tn), lambda j, i: (0, j))],
            out_specs=pl.BlockSpec((tm, tn), lambda j, i: (i, j)),
            out_shape=jax.ShapeDtypeStruct((m, n), F32), compiler_params=_cparams(("arbitrary", "arbitrary")),
        )(a, b)
    if form == 'nt':
        (m, n), (k, _) = a.shape, b.shape
        tm = _tile(m, MM_ROWS, 8)
        tk = _tile(k, max(LANES, WEIGHT_BLOCK_BYTES // (n * b.dtype.itemsize)), LANES)

        def body(a_ref, b_ref, o_ref):
            o_ref[...] = _raw_mm(a_ref[...], b_ref[...], 'nt')

        return pl.pallas_call(
            body, name=name, grid=(k // tk, m // tm),
            in_specs=[pl.BlockSpec((tm, n), lambda j, i: (i, 0)), pl.BlockSpec((tk, n), lambda j, i: (j, 0))],
            out_specs=pl.BlockSpec((tm, tk), lambda j, i: (i, j)),
            out_shape=jax.ShapeDtypeStruct((m, k), F32), compiler_params=_cparams(("arbitrary", "arbitrary")),
        )(a, b)
    (m, ka), (_, n) = a.shape, b.shape
    tka = _tile(ka, ACC_MAX_ROWS, LANES)
    tn, tmc = _tile(n, max(LANES, ACC_BLOCK_BYTES // (4 * tka)), LANES), _tile(m, MM_ROWS, 8)

    def body(a_ref, b_ref, o_ref):
        @pl.when(pl.program_id(2) == 0)
        def _():
            o_ref[...] = jnp.zeros_like(o_ref)

        o_ref[...] += _raw_mm(a_ref[...], b_ref[...], 'tn')

    return pl.pallas_call(
        body, name=name, grid=(ka // tka, n // tn, m // tmc),
        in_specs=[pl.BlockSpec((tmc, tka), lambda i, j, s: (s, i)), pl.BlockSpec((tmc, tn), lambda i, j, s: (s, j))],
        out_specs=pl.BlockSpec((tka, tn), lambda i, j, s: (i, j)),
        out_shape=jax.ShapeDtypeStruct((ka, n), F32), compiler_params=_cparams(("arbitrary", "arbitrary", "arbitrary")),
    )(a, b)


@functools.partial(jax.custom_vjp, nondiff_argnums=(0,))
def _mm_op(name, a, w):
    return _matmul(name + '_fwd', a, w.astype(MXU_DTYPE), 'nn')


def _mm_op_fwd(name, a, w):
    wb = w.astype(MXU_DTYPE)
    return _matmul(name + '_fwd', a, wb, 'nn'), (a, wb)


def _mm_op_bwd(name, res, g):
    a, wb = res
    return _matmul(name + '_da', g, wb, 'nt'), _matmul(name + '_dw', a, g, 'tn')


_mm_op.defvjp(_mm_op_fwd, _mm_op_bwd)


def _rope128(x, c, a, b):
    return x * c + _roll(x, LANES - 8, 1) * a + _roll(x, 8, 1) * b


def _mla_prep_fn(i, rows, params):
    pm, c, a, b = rows
    qna, wqb, kvna, wkn, wv, qn, kn = params
    cq, ckv, kr_slot = pm[:, 0:256], pm[:, 256:384], pm[:, 384:512]
    q_all = _mm(_rms(cq) * qna, wqb, 'nn')
    ckvn = _rms(ckv) * kvna
    k_all = _mm(ckvn, wkn, 'nn')
    v_all = _mm(ckvn, wv, 'nn')
    qs, ks = [], []
    for h in range(MLA_HEADS):
        sl = slice(LANES * h, LANES * (h + 1))
        qs.append(_rope128(_rms(q_all[:, sl], MLA_QK) * qn, c, a, b))
        ks.append(_rope128(_rms(k_all[:, sl] + kr_slot, MLA_QK) * kn, c, a, b))
    return jnp.concatenate(qs, axis=1), jnp.concatenate(ks, axis=1), v_all


def _attn_fwd(name, q, k, v, tb, ctx_len):
    t = q.shape[0]
    scale = MLA_QK ** -0.5

    def body(q_ref, k_ref, v_ref, o_ref, lse_ref):
        qi = pl.program_id(1)

        def attend(k, v):
            s2 = _raw_mm(q_ref[...], k, 'nt') * (scale * LOG2E)
            m2 = jnp.max(s2, axis=-1, keepdims=True)
            p = jnp.exp2(s2 - m2)
            l = jnp.sum(p, axis=-1, keepdims=True)
            o_ref[...] = _raw_mm(p, v, 'nn') / l
            lse_ref[...] = jnp.broadcast_to((m2 + jnp.log2(l)) * LN2, lse_ref.shape)

        @pl.when(qi == 0)
        def _():
            attend(k_ref[0:ctx_len, :], v_ref[0:ctx_len, :])

        @pl.when(qi != 0)
        def _():
            attend(k_ref[...], v_ref[...])

    blk = pl.BlockSpec((tb, LANES), lambda h, i: (i, h))
    whole = pl.BlockSpec((t, LANES), lambda h, i: (0, h))
    return pl.pallas_call(
        body, name=name, grid=(MLA_HEADS, t // tb), in_specs=[blk, whole, whole], out_specs=[blk, blk],
        out_shape=[jax.ShapeDtypeStruct(q.shape, F32)] * 2, compiler_params=_cparams(("arbitrary", "arbitrary")),
    )(q, k, v)


def _attn_bwd(name, q, k, v, o, lse, do, tb, ctx_len):
    t = q.shape[0]
    scale = MLA_QK ** -0.5
    ck = _tile(t, 2816, 256)

    def body(q_ref, k_ref, v_ref, o_ref, lse_ref, do_ref, dq_ref, dk_ref, dv_ref):
        qi = pl.program_id(1)

        @pl.when(qi == 0)
        def _():
            dk_ref[...] = jnp.zeros_like(dk_ref)
            dv_ref[...] = jnp.zeros_like(dv_ref)

        q = q_ref[...]
        do = do_ref[...].astype(MXU_DTYPE)
        lse2 = lse_ref[...][:, 0:1] * LOG2E
        delta = jnp.sum(do_ref[...] * o_ref[...], axis=-1, keepdims=True)

        def part(rows):
            ks, vs = k_ref[rows, :], v_ref[rows, :]
            p = jnp.exp2(_raw_mm(q, ks, 'nt') * (scale * LOG2E) - lse2)
            ds = p * ((_raw_mm(do, vs, 'nt') - delta) * scale)
            dk_ref[rows, :] += _raw_mm(ds, q, 'tn')
            dv_ref[rows, :] += _raw_mm(p, do, 'tn')
            return _raw_mm(ds, ks, 'nn')

        @pl.when(qi == 0)
        def _():
            dq_ref[...] = part(pl.ds(0, ctx_len))

        @pl.when(qi != 0)
        def _():
            dq = part(pl.ds(0, ck))
            for c in range(1, t // ck):
                dq = dq + part(pl.ds(c * ck, ck))
            dq_ref[...] = dq

    blk = pl.BlockSpec((tb, LANES), lambda h, i: (i, h))
    whole = pl.BlockSpec((t, LANES), lambda h, i: (0, h))
    return pl.pallas_call(
        body, name=name, grid=(MLA_HEADS, t // tb), in_specs=[blk, whole, whole, blk, blk, blk],
        out_specs=[blk, whole, whole], out_shape=[jax.ShapeDtypeStruct(q.shape, F32)] * 3,
        compiler_params=_cparams(("arbitrary", "arbitrary")),
    )(q, k, v, o, lse, do)


def _mla_rows(pm, tabs):
    return [(pm, 0, pm.shape[1])] + [_full(x) for x in tabs]


@functools.partial(jax.custom_vjp, nondiff_argnums=(0, 1))
def _mla_branch(name, tb, pm, tabs, params):
    return _mla_branch_fwd(name, tb, pm, tabs, params)[0]


def _mla_branch_fwd(name, tb, pm, tabs, params):
    w = LANES * MLA_HEADS
    q, k, v = _rw_fwd(name + '_prep', _mla_prep_fn, _mla_rows(pm, tabs), list(params), [(w, MXU_DTYPE)] * 3, tb)
    o, lse = _attn_fwd(name + '_attn', q, k, v, tb, tb)
    return o, (pm, tabs, params, q, k, v, o, lse)


def _mla_branch_bwd(name, tb, res, do):
    pm, tabs, params, q, k, v, o, lse = res
    dq, dk, dv = _attn_bwd(name + '_attn_bwd', q, k, v, o, lse, do, tb, tb)
    (dpm,), dpars = _rw_bwd(name + '_prep_bwd', _mla_prep_fn, _mla_rows(pm, tabs), list(params), [dq, dk, dv], tb, (0,))
    return dpm, tuple(jnp.zeros_like(x) for x in tabs), tuple(dpars)


_mla_branch.defvjp(_mla_branch_fwd, _mla_branch_bwd)


def _chunk_masks(reverse):
    i = lax.broadcasted_iota(jnp.int32, (CHUNK, CHUNK), 0)
    j = lax.broadcasted_iota(jnp.int32, (CHUNK, CHUNK), 1)
    return i, j, ((j > i) if reverse else (j <= i))


def _gla_chunk(reverse, rows, params, st0):
    q, k, v, la = rows
    q = q * (GLA_DK ** -0.5)
    cum = _cumsum_rows(la, reverse)
    tot = cum[0:1] if reverse else cum[CHUNK - 1:CHUNK]
    st1 = st0 * jnp.exp(tot) + _mm(v, k * jnp.exp(tot - cum), 'tn')
    qd = q * jnp.exp(cum)
    _, _, mask = _chunk_masks(reverse)
    att = jnp.where(mask, _mm(qd, k * jnp.exp(-cum), 'nt'), 0.0)
    return _mm(att, v, 'nn') + _mm(qd, st0, 'nt'), st1


def _ret_chunk(reverse, rows, params, st0):
    q, k, v, cc, ss = rows
    (lg,) = params
    q = q * cc + _roll(q, RET_DK // 2, 1) * ss
    k = (k * cc + _roll(k, RET_DK // 2, 1) * ss) * (RET_DK ** -0.5)
    r = lax.broadcasted_iota(jnp.int32, (CHUNK, LANES), 0).astype(F32)
    zeta = jnp.exp((r if reverse else (CHUNK - 1.0 - r)) * lg)
    xi = jnp.exp(((CHUNK - r) if reverse else (r + 1.0)) * lg)
    st1 = st0 * jnp.exp(CHUNK * lg) + _mm(v, k * zeta, 'tn')
    i, j, mask = _chunk_masks(reverse)
    rel = jnp.where(mask, (j - i) if reverse else (i - j), 0).astype(F32)
    dmat = jnp.where(mask, jnp.exp(rel * lg[:, 0:CHUNK]), 0.0)
    att = _mm(q, k, 'nt') * dmat
    return _mm(att, v, 'nn') + _mm(q, st0, 'nt') * xi, st1


def _scan_order(reverse, nblk):
    if reverse:
        return lambda t: jnp.where(t == 0, 0, nblk - t)
    return lambda t: t


def _scan_specs(rows, params, tb, heads, blk_of):
    def rspec(spec):
        _, cb = spec
        if cb is None:
            return pl.BlockSpec((tb, LANES), lambda s: (blk_of(s), 0))
        return pl.BlockSpec((tb, heads * LANES), lambda s, cb=cb: (blk_of(s), cb // heads))

    return [rspec(s) for s in rows] + [pl.BlockSpec((heads, 1, LANES), lambda s: (0, 0, 0)) for _ in params]


def _head_rows(row_refs, rows, sl, h):
    lanes = pl.ds(h * LANES, LANES)
    return [r[sl, :] if spec[1] is None else r[sl, lanes] for r, spec in zip(row_refs, rows)]


def _scan_fwd(name, chunk_fn, reverse, rows, params, tb, heads):
    t = rows[0][0].shape[0]
    nblk, cpb = t // tb, tb // CHUNK
    blk_of = _scan_order(reverse, nblk)
    nr, npar = len(rows), len(params)
    order = list(range(cpb))[::-1] if reverse else list(range(cpb))

    def body(*refs):
        row_refs, par_refs = refs[:nr], refs[nr:nr + npar]
        o_ref, st_out_ref, st_ref = refs[nr + npar:]

        @pl.when(pl.program_id(0) == 0)
        def _():
            st_ref[...] = jnp.zeros_like(st_ref)

        for c in order:
            sl = pl.ds(c * CHUNK, CHUNK)
            for h in range(heads):
                st0 = st_ref[h]
                st_out_ref[h, c] = st0
                o, st1 = chunk_fn(reverse, _head_rows(row_refs, rows, sl, h), [p[h] for p in par_refs], st0)
                o_ref[sl, pl.ds(h * LANES, LANES)] = o
                st_ref[h] = st1

    return pl.pallas_call(
        body, name=name, grid=(nblk,), in_specs=_scan_specs(rows, params, tb, heads, blk_of),
        out_specs=[pl.BlockSpec((tb, heads * LANES), lambda s: (blk_of(s), 0)),
                   pl.BlockSpec((heads, cpb, LANES, LANES), lambda s: (0, blk_of(s), 0, 0))],
        out_shape=[jax.ShapeDtypeStruct((t, heads * LANES), F32),
                   jax.ShapeDtypeStruct((heads, t // CHUNK, LANES, LANES), F32)],
        scratch_shapes=[pltpu.VMEM((heads, LANES, LANES), F32)],
        compiler_params=_cparams(("arbitrary",)),
    )(*[s[0] for s in rows], *params)


def _scan_bwd(name, chunk_fn, reverse, rows, params, states, do, tb, heads, n_diff):
    t = rows[0][0].shape[0]
    nblk, cpb = t // tb, tb // CHUNK
    fwd_blk = _scan_order(reverse, nblk)
    blk_of = lambda s: fwd_blk(nblk - 1 - s)
    nr, npar = len(rows), len(params)
    order = list(range(cpb)) if reverse else list(range(cpb))[::-1]

    def body(*refs):
        row_refs, par_refs = refs[:nr], refs[nr:nr + npar]
        st_in_ref, do_ref = refs[nr + npar:nr + npar + 2]
        out_refs = refs[nr + npar + 2:-1]
        dst_ref = refs[-1]

        @pl.when(pl.program_id(0) == 0)
        def _():
            dst_ref[...] = jnp.zeros_like(dst_ref)
            for k in range(npar):
                out_refs[n_diff + k][...] = jnp.zeros_like(out_refs[n_diff + k])

        for c in order:
            sl = pl.ds(c * CHUNK, CHUNK)
            for h in range(heads):
                lanes = pl.ds(h * LANES, LANES)
                rv = _head_rows(row_refs, rows, sl, h)

                def f(dr, pvals, st0, rv=rv):
                    return chunk_fn(reverse, list(dr) + rv[n_diff:], pvals, st0)

                _, vjp = jax.vjp(f, rv[:n_diff], [p[h] for p in par_refs], st_in_ref[h, c])
                drows, dpars, dst0 = vjp((do_ref[sl, lanes], dst_ref[h]))
                for k in range(n_diff):
                    out_refs[k][sl, lanes] = drows[k]
                for k in range(npar):
                    out_refs[n_diff + k][h] += dpars[k]
                dst_ref[h] = dst0

    wide = pl.BlockSpec((tb, heads * LANES), lambda s: (blk_of(s), 0))
    pblk = pl.BlockSpec((heads, 1, LANES), lambda s: (0, 0, 0))
    res = pl.pallas_call(
        body, name=name, grid=(nblk,),
        in_specs=(_scan_specs(rows, params, tb, heads, blk_of)
                  + [pl.BlockSpec((heads, cpb, LANES, LANES), lambda s: (0, blk_of(s), 0, 0)), wide]),
        out_specs=[wide] * n_diff + [pblk for _ in params],
        out_shape=([jax.ShapeDtypeStruct((t, heads * LANES), F32)] * n_diff
                   + [jax.ShapeDtypeStruct(p.shape, F32) for p in params]),
        scratch_shapes=[pltpu.VMEM((heads, LANES, LANES), F32)],
        compiler_params=_cparams(("arbitrary",)),
    )(*[s[0] for s in rows], *params, states, do)
    return list(res[:n_diff]), list(res[n_diff:])


def _gla_la_fn(i, rows, params):
    (r,) = rows
    w2f, w2b, bgk = params
    la_f = jax.nn.log_sigmoid(_mm(r, w2f, 'nn') + bgk[0:1]) / GLA_GATE_NORMALIZER
    la_b = jax.nn.log_sigmoid(_mm(r, w2b, 'nn') + bgk[1:2]) / GLA_GATE_NORMALIZER
    return la_f, la_b


def _headnorm_fn(heads, with_weight):
    def fn(i, rows, params):
        o_f, o_b, g = rows
        outs = []
        for h in range(heads):
            sl = slice(LANES * h, LANES * (h + 1))
            y = _rms(o_f[:, sl] + o_b[:, sl])
            outs.append(y * params[0] if with_weight else y)
        return (jnp.concatenate(outs, axis=1) * jax.nn.silu(g),)

    return fn


@functools.partial(jax.custom_vjp, nondiff_argnums=(0, 1))
def _gla_branch(name, tb, pg, params):
    return _gla_branch_fwd(name, tb, pg, params)[0]


def _gla_rows(pg, la):
    return [(pg, 0), (pg, GLA_HEADS), (pg, 2 * GLA_HEADS), (la, 0)]


def _gla_branch_fwd(name, tb, pg, params):
    w2f, w2b, bgk, onorm = params
    w = GLA_HEADS * LANES
    la_f, la_b = _rw_fwd(name + '_la', _gla_la_fn, [(pg, 4 * w // LANES, LANES)], [w2f, w2b, bgk], [(w, F32)] * 2, tb)
    o_f, st_f = _scan_fwd(name + '_scan_f', _gla_chunk, False, _gla_rows(pg, la_f), [], tb, GLA_HEADS)
    o_b, st_b = _scan_fwd(name + '_scan_b', _gla_chunk, True, _gla_rows(pg, la_b), [], tb, GLA_HEADS)
    (y,) = _rw_fwd(name + '_norm', _headnorm_fn(GLA_HEADS, True), [_full(o_f), _full(o_b), (pg, 3, w)], [onorm],
                   [(w, F32)], tb)
    return y, (pg, params, la_f, la_b, o_f, o_b, st_f, st_b)


def _gla_branch_bwd(name, tb, res, dy):
    pg, params, la_f, la_b, o_f, o_b, st_f, st_b = res
    w2f, w2b, bgk, onorm = params
    w = GLA_HEADS * LANES
    (do_f, do_b, dg), (donorm,) = _rw_bwd(name + '_norm_bwd', _headnorm_fn(GLA_HEADS, True),
                                          [_full(o_f), _full(o_b), (pg, 3, w)], [onorm], [dy], tb, (0, 1, 2))
    (dq_f, dk_f, dv_f, dla_f), _ = _scan_bwd(name + '_scan_f_bwd', _gla_chunk, False, _gla_rows(pg, la_f), [], st_f,
                                             do_f, tb, GLA_HEADS, 4)
    (dq_b, dk_b, dv_b, dla_b), _ = _scan_bwd(name + '_scan_b_bwd', _gla_chunk, True, _gla_rows(pg, la_b), [], st_b,
                                             do_b, tb, GLA_HEADS, 4)
    (dr,), (dw2f, dw2b, dbgk) = _rw_bwd(name + '_la_bwd', _gla_la_fn, [(pg, 4 * w // LANES, LANES)], [w2f, w2b, bgk],
                                        [dla_f, dla_b], tb, (0,))
    dpg = jnp.concatenate([dq_f + dq_b, dk_f + dk_b, dv_f + dv_b, dg, dr], axis=1)
    return dpg, (dw2f, dw2b, dbgk, donorm)


_gla_branch.defvjp(_gla_branch_fwd, _gla_branch_bwd)


@functools.partial(jax.custom_vjp, nondiff_argnums=(0, 1))
def _ret_branch(name, tb, pr, tabs, lg):
    return _ret_branch_fwd(name, tb, pr, tabs, lg)[0]


def _ret_rows(pr, tabs):
    return [(pr, 0), (pr, RET_HEADS), (pr, 2 * RET_HEADS), (tabs[0], None), (tabs[1], None)]


def _ret_branch_fwd(name, tb, pr, tabs, lg):
    w = RET_HEADS * LANES
    o_f, st_f = _scan_fwd(name + '_scan_f', _ret_chunk, False, _ret_rows(pr, tabs), [lg[0]], tb, RET_HEADS)
    o_b, st_b = _scan_fwd(name + '_scan_b', _ret_chunk, True, _ret_rows(pr, tabs), [lg[1]], tb, RET_HEADS)
    (y,) = _rw_fwd(name + '_norm', _headnorm_fn(RET_HEADS, False), [_full(o_f), _full(o_b), (pr, 3, w)], [],
                   [(w, F32)], tb)
    return y, (pr, tabs, lg, o_f, o_b, st_f, st_b)


def _ret_branch_bwd(name, tb, res, dy):
    pr, tabs, lg, o_f, o_b, st_f, st_b = res
    w = RET_HEADS * LANES
    (do_f, do_b, dg), _ = _rw_bwd(name + '_norm_bwd', _headnorm_fn(RET_HEADS, False),
                                  [_full(o_f), _full(o_b), (pr, 3, w)], [], [dy], tb, (0, 1, 2))
    (dq_f, dk_f, dv_f), (dlg_f,) = _scan_bwd(name + '_scan_f_bwd', _ret_chunk, False, _ret_rows(pr, tabs), [lg[0]],
                                             st_f, do_f, tb, RET_HEADS, 3)
    (dq_b, dk_b, dv_b), (dlg_b,) = _scan_bwd(name + '_scan_b_bwd', _ret_chunk, True, _ret_rows(pr, tabs), [lg[1]],
                                             st_b, do_b, tb, RET_HEADS, 3)
    dpr = jnp.concatenate([dq_f + dq_b, dk_f + dk_b, dv_f + dv_b, dg], axis=1)
    return dpr, tuple(jnp.zeros_like(x) for x in tabs), jnp.stack([dlg_f, dlg_b])


_ret_branch.defvjp(_ret_branch_fwd, _ret_branch_bwd)


HALO = 8


def _halo_specs(tb, nblk, width, col_block):
    r = tb // HALO
    prev = pl.BlockSpec((HALO, width), lambda i: (jnp.maximum(i * r - 1, 0), col_block))
    nxt = pl.BlockSpec((HALO, width), lambda i: (jnp.minimum((i + 1) * r, nblk * r - 1), col_block))
    return prev, nxt


def _shifted(x, prev_blk, next_blk, i, nblk):
    tb = x.shape[0]
    row = lax.broadcasted_iota(jnp.int32, x.shape, 0)
    prev_row = jnp.where(i >= 2, prev_blk[HALO - 1:HALO], 0.0)
    next_row = jnp.where((i >= 1) & (i < nblk - 1), next_blk[0:1], 0.0)
    down = jnp.where(row == 0, prev_row, pltpu.roll(x, 1, 0))
    up = jnp.where(row == tb - 1, next_row, pltpu.roll(x, tb - 1, 0))
    return down, up


def _gelu_up(c, up):
    return jax.nn.gelu(c) * up


def _convact_fwd_call(name, gu, w_dw, b_dw, tb):
    t = gu.shape[0]
    nblk = t // tb
    prev_spec, next_spec = _halo_specs(tb, nblk, D_FF, 0)

    def body(g_ref, up_ref, prev_ref, next_ref, w_ref, b_ref, o_ref):
        i = pl.program_id(0)
        g = g_ref[...]
        down, upw = _shifted(g, prev_ref[...], next_ref[...], i, nblk)
        w = w_ref[...]
        c = w[0:1] * down + w[1:2] * g + w[2:3] * upw + b_ref[...]
        o_ref[...] = _gelu_up(c, up_ref[...])

    return pl.pallas_call(
        body, name=name, grid=(nblk,),
        in_specs=[pl.BlockSpec((tb, D_FF), lambda i: (i, 0)), pl.BlockSpec((tb, D_FF), lambda i: (i, 1)), prev_spec,
                  next_spec, _whole_spec(w_dw), _whole_spec(b_dw)],
        out_specs=pl.BlockSpec((tb, D_FF), lambda i: (i, 0)), out_shape=jax.ShapeDtypeStruct((t, D_FF), F32),
        compiler_params=_cparams(("arbitrary",)),
    )(gu, gu, gu, gu, w_dw, b_dw)


def _convact_bwd_calls(name, gu, w_dw, b_dw, dact, tb):
    t = gu.shape[0]
    nblk = t // tb
    prev_spec, next_spec = _halo_specs(tb, nblk, D_FF, 0)

    def body1(g_ref, up_ref, prev_ref, next_ref, w_ref, b_ref, da_ref, dc_ref, dup_ref, dw_ref, db_ref):
        i = pl.program_id(0)
        g = g_ref[...]
        down, upw = _shifted(g, prev_ref[...], next_ref[...], i, nblk)
        w = w_ref[...]
        c = w[0:1] * down + w[1:2] * g + w[2:3] * upw + b_ref[...]
        _, vjp = jax.vjp(_gelu_up, c, up_ref[...])
        dc, dup = vjp(da_ref[...])
        dc_ref[...] = dc
        dup_ref[...] = dup

        @pl.when(i == 0)
        def _():
            dw_ref[...] = jnp.zeros_like(dw_ref)
            db_ref[...] = jnp.zeros_like(db_ref)

        dw_ref[0:1, :] += jnp.sum(dc * down, axis=0, keepdims=True)
        dw_ref[1:2, :] += jnp.sum(dc * g, axis=0, keepdims=True)
        dw_ref[2:3, :] += jnp.sum(dc * upw, axis=0, keepdims=True)
        db_ref[...] += jnp.sum(dc, axis=0, keepdims=True)

    blk = pl.BlockSpec((tb, D_FF), lambda i: (i, 0))
    dc, dup, dw, db = pl.pallas_call(
        body1, name=name + '_a', grid=(nblk,),
        in_specs=[blk, pl.BlockSpec((tb, D_FF), lambda i: (i, 1)), prev_spec, next_spec, _whole_spec(w_dw),
                  _whole_spec(b_dw), blk],
        out_specs=[blk, blk, _whole_spec(w_dw), _whole_spec(b_dw)],
        out_shape=[jax.ShapeDtypeStruct((t, D_FF), F32)] * 2 + [jax.ShapeDtypeStruct(w_dw.shape, F32),
                                                                jax.ShapeDtypeStruct(b_dw.shape, F32)],
        compiler_params=_cparams(("arbitrary",)),
    )(gu, gu, gu, gu, w_dw, b_dw, dact)

    def body2(dc_ref, prev_ref, next_ref, dup_ref, w_ref, o_ref):
        i = pl.program_id(0)
        dc_blk = dc_ref[...]
        down, upw = _shifted(dc_blk, prev_ref[...], next_ref[...], i, nblk)
        w = w_ref[...]
        o_ref[:, 0:D_FF] = w[0:1] * upw + w[1:2] * dc_blk + w[2:3] * down
        o_ref[:, D_FF:2 * D_FF] = dup_ref[...]

    dgu = pl.pallas_call(
        body2, name=name + '_b', grid=(nblk,),
        in_specs=[blk, prev_spec, next_spec, blk, _whole_spec(w_dw)],
        out_specs=pl.BlockSpec((tb, 2 * D_FF), lambda i: (i, 0)), out_shape=jax.ShapeDtypeStruct((t, 2 * D_FF), F32),
        compiler_params=_cparams(("arbitrary",)),
    )(dc, dc, dc, dup, w_dw)
    return dgu, dw, db


@functools.partial(jax.custom_vjp, nondiff_argnums=(0, 1))
def _convact(name, tb, gu, w_dw, b_dw):
    return _convact_fwd_call(name + '_fwd', gu, w_dw, b_dw, tb)


def _convact_fwd(name, tb, gu, w_dw, b_dw):
    return _convact_fwd_call(name + '_fwd', gu, w_dw, b_dw, tb), (gu, w_dw, b_dw)


def _convact_bwd(name, tb, res, dact):
    gu, w_dw, b_dw = res
    return _convact_bwd_calls(name + '_bwd', gu, w_dw, b_dw, dact, tb)


_convact.defvjp(_convact_fwd, _convact_bwd)


def _loss_fwd_call(h, target, tb):
    nlat = target.shape[0] // tb

    def body(h_ref, t_ref, o_ref):
        @pl.when(pl.program_id(0) == 0)
        def _():
            o_ref[...] = jnp.zeros_like(o_ref)

        e = h_ref[...] - t_ref[...]
        o_ref[...] += jnp.sum(e * e, axis=0, keepdims=True)

    cols = pl.pallas_call(
        body, name='loss_fwd', grid=(nlat,),
        in_specs=[pl.BlockSpec((tb, D_MODEL), lambda i: (i + 1, 0)), pl.BlockSpec((tb, D_MODEL), lambda i: (i, 0))],
        out_specs=pl.BlockSpec((1, D_MODEL), lambda i: (0, 0)), out_shape=jax.ShapeDtypeStruct((1, D_MODEL), F32),
        compiler_params=_cparams(("arbitrary",)),
    )(h, target)
    return (0.5 / D_MODEL) * jnp.sum(cols)


def _loss_bwd_call(h, target, gbar, tb):
    def body(g_ref, h_ref, t_ref, o_ref):
        live = jnp.where(pl.program_id(0) == 0, 0.0, g_ref[...] * (1.0 / D_MODEL))
        o_ref[...] = live * (h_ref[...] - t_ref[...])

    return pl.pallas_call(
        body, name='loss_bwd', grid=(h.shape[0] // tb,),
        in_specs=[pl.BlockSpec((1, 1), lambda i: (0, 0)), pl.BlockSpec((tb, D_MODEL), lambda i: (i, 0)),
                  pl.BlockSpec((tb, D_MODEL), lambda i: (jnp.maximum(i - 1, 0), 0))],
        out_specs=pl.BlockSpec((tb, D_MODEL), lambda i: (i, 0)), out_shape=jax.ShapeDtypeStruct(h.shape, F32),
        compiler_params=_cparams(("arbitrary",)),
    )(gbar.reshape(1, 1), h, target)


@functools.partial(jax.custom_vjp, nondiff_argnums=(0,))
def _loss_op(tb, h, target):
    return _loss_fwd_call(h, target, tb)


def _loss_op_fwd(tb, h, target):
    return _loss_fwd_call(h, target, tb), (h, target)


def _loss_op_bwd(tb, res, gbar):
    h, target = res
    return _loss_bwd_call(h, target, gbar, tb), jnp.zeros_like(target)


_loss_op.defvjp(_loss_op_fwd, _loss_op_bwd)


def _rope_tables(pos, dim, theta):
    inv = theta ** (-jnp.arange(dim // 2, dtype=F32) * 2.0 / dim)
    ang = pos.astype(F32)[:, None] * inv[None, :]
    return jnp.cos(ang), jnp.sin(ang)


def _mla_tables(seq, ctx_len):
    rows = seq // GRID_W
    row_pos = jnp.repeat(jnp.arange(rows), GRID_W)
    col_pos = jnp.tile(jnp.arange(GRID_W), rows)
    cos_r, sin_r = _rope_tables(row_pos, MLA_ROPE // 2, ROPE_THETA)
    cos_c, sin_c = _rope_tables(col_pos, MLA_ROPE // 2, ROPE_THETA)
    one = jnp.ones((seq, MLA_NOPE), F32)
    z8 = jnp.zeros((seq, 8), F32)
    pad1 = jnp.ones((seq, LANES - MLA_QK), F32)
    pad0 = jnp.zeros((seq, LANES - MLA_QK), F32)
    z64 = jnp.zeros((seq, MLA_NOPE), F32)
    c = jnp.concatenate([one, cos_r, cos_r, cos_c, cos_c, pad1], axis=1)
    a = jnp.concatenate([z64, -sin_r, z8, -sin_c, z8, pad0], axis=1)
    b = jnp.concatenate([z64, z8, sin_r, z8, sin_c, pad0], axis=1)
    ctx_rows = lambda fill: jnp.full((ctx_len, LANES), fill, F32)
    return (jnp.concatenate([ctx_rows(1.0), c]), jnp.concatenate([ctx_rows(0.0), a]),
            jnp.concatenate([ctx_rows(0.0), b]))


def _ret_tables(total):
    inv = 1.0 / (RET_THETA ** jnp.linspace(0.0, 1.0, RET_DK // 2, dtype=F32))
    ang = jnp.arange(total).astype(F32)[:, None] * inv[None, :]
    cos, sin = jnp.cos(ang), jnp.sin(ang)
    return jnp.concatenate([cos, cos], axis=1), jnp.concatenate([-sin, sin], axis=1)


def _head_slots(w, heads, width):
    k = w.shape[0]
    return jnp.pad(w.reshape(k, heads, width), ((0, 0), (0, 0), (0, LANES - width))).reshape(k, heads * LANES)


def _pad_lanes(v, width):
    return jnp.pad(v, (0, LANES - width)).reshape(1, LANES)


IN_MLA_QKV = (0, 384)
IN_MLA_KR = (384, 416)
IN_GLA = (416, 2496)
IN_RET = (2496, 4544)
IN_GATES = (4544, 7616)
W_MLA_COLS = 512
W_GLA_COLS = 2176
BIG = ('w_ada', 'w_in', 'w_branch', 'w_out', 'w_ffn_in', 'w_ffn_out')
BIG_AXIS = dict(w_ada=1, w_in=1, w_branch=2, w_out=0, w_ffn_in=1, w_ffn_out=0)


def _big_layer_weights(full):
    w_in = full['w_in']
    zc = lambda n: jnp.zeros((D_MODEL, n), w_in.dtype)
    wbr = full['w_branch']
    wb_mla = jnp.pad(wbr[0].reshape(MLA_HEADS, MLA_V, D_MODEL), ((0, 0), (0, LANES - MLA_V), (0, 0)))
    return dict(
        ada=full['w_ada'],
        in_mla=jnp.concatenate([w_in[:, slice(*IN_MLA_QKV)], zc(MLA_NOPE), w_in[:, slice(*IN_MLA_KR)],
                                zc(LANES - MLA_QK)], axis=1),
        in_gla=jnp.concatenate([w_in[:, slice(*IN_GLA)], zc(W_GLA_COLS - (IN_GLA[1] - IN_GLA[0]))], axis=1),
        in_ret=w_in[:, slice(*IN_RET)], in_gate=w_in[:, slice(*IN_GATES)],
        br_mla=wb_mla.reshape(MLA_HEADS * LANES, D_MODEL), br_gla=wbr[1], br_ret=wbr[2],
        out=full['w_out'], ffn_in=full['w_ffn_in'], ffn_out=full['w_ffn_out'])


def _big_layer_grads(g):
    n_gla = IN_GLA[1] - IN_GLA[0]
    kr_at = IN_MLA_QKV[1] + MLA_NOPE
    w_in = jnp.concatenate([g['in_mla'][:, slice(*IN_MLA_QKV)], g['in_mla'][:, kr_at:kr_at + MLA_ROPE],
                            g['in_gla'][:, :n_gla], g['in_ret'], g['in_gate']], axis=1)
    br_mla = g['br_mla'].reshape(MLA_HEADS, LANES, D_MODEL)[:, :MLA_V].reshape(MLA_HEADS * MLA_V, D_MODEL)
    return dict(w_ada=g['ada'], w_in=w_in, w_branch=jnp.stack([br_mla, g['br_gla'], g['br_ret']]),
                w_out=g['out'], w_ffn_in=g['ffn_in'], w_ffn_out=g['ffn_out'])


def _layer(l, big, w, h, mod16, tabs_mla, tabs_ret, tb):
    nm = 'l%d_' % l
    row = lambda v: v.reshape(1, -1)
    b_ada = row(w['b_ada'][l])

    a = _normmod_op((nm + 'norm1', 0, 1), tb, h, row(w['norm1_w'][l]), mod16, b_ada)[0]
    pm = _mm_op(nm + 'in_mla', a, big['in_mla'])
    pg = _mm_op(nm + 'in_gla', a, big['in_gla'])
    pr = _mm_op(nm + 'in_ret', a, big['in_ret'])
    pgate = _mm_op(nm + 'in_gate', a, big['in_gate'])

    kvb = w['mla_w_kvb'][l].reshape(-1, MLA_HEADS, MLA_NOPE + MLA_V)
    kdim = kvb.shape[0]
    mla_params = (row(w['mla_q_norm_a'][l]), _head_slots(w['mla_w_qb'][l], MLA_HEADS, MLA_QK),
                  row(w['mla_kv_norm_a'][l]), _head_slots(kvb[:, :, :MLA_NOPE].reshape(kdim, -1), MLA_HEADS, MLA_NOPE),
                  _head_slots(kvb[:, :, MLA_NOPE:].reshape(kdim, -1), MLA_HEADS, MLA_V),
                  _pad_lanes(w['mla_q_norm'][l], MLA_QK), _pad_lanes(w['mla_k_norm'][l], MLA_QK))
    y_mla = _mla_branch(nm + 'mla', tb, pm, tabs_mla, mla_params)

    gk2 = w['gla_w_gk2'][l]
    rank = gk2.shape[1]
    w2f = jnp.pad(gk2[0], ((0, LANES - rank), (0, 0)))
    w2b = jnp.pad(gk2[1], ((rank, LANES - 2 * rank), (0, 0)))
    y_gla = _gla_branch(nm + 'gla', tb, pg, (w2f, w2b, w['gla_b_gk'][l], row(w['gla_o_norm'][l])))

    log_g = -jnp.exp(w['ret_decay'][l])
    lg = jnp.broadcast_to(log_g[:, :, None, None], (2, RET_HEADS, 1, LANES))
    y_ret = _ret_branch(nm + 'ret', tb, pr, tabs_ret, lg)

    z0 = _mm_op(nm + 'br_mla', y_mla, big['br_mla'])
    z1 = _mm_op(nm + 'br_gla', y_gla, big['br_gla'])
    z2 = _mm_op(nm + 'br_ret', y_ret, big['br_ret'])
    u = _merge_op((nm + 'merge',), tb, z0, z1, z2, pgate, row(w['b_gate'][l]))[0]
    y = _mm_op(nm + 'out', u, big['out'])
    h = _resid_op((nm + 'res1', 2), tb, h, y, mod16, b_ada)[0]

    a2 = _normmod_op((nm + 'norm2', 3, 4), tb, h, row(w['norm2_w'][l]), mod16, b_ada)[0]
    gu = _mm_op(nm + 'ffn_in', a2, big['ffn_in'])
    act = _convact(nm + 'convact', tb, gu, w['w_dw'][l], row(w['b_dw'][l]))
    f = _mm_op(nm + 'ffn_out', act, big['ffn_out'])
    return _resid_op((nm + 'res2', 5), tb, h, f, mod16, b_ada)[0]


def _local_loss(big, w, x, c, ctx, target):
    seq, tb = x.shape[1], ctx.shape[1]
    h = jnp.concatenate([ctx[0], x[0]], axis=0)
    cond_in = jnp.concatenate([w['c_ctx'].reshape(1, -1), c, jnp.zeros((14, D_MODEL), F32)], axis=0)
    cond16 = _silu_op(('cond_silu',), 16, cond_in)[0]
    tabs_mla = _mla_tables(seq, tb)
    tabs_ret = _ret_tables(seq + tb)
    for l in range(DEPTH):
        mod16 = _mm_op('l%d_ada' % l, cond16, big[l]['ada'])
        h = _layer(l, big[l], w, h, mod16, tabs_mla, tabs_ret, tb)
    return _loss_op(tb, h, target[0])


ANY = pl.BlockSpec(memory_space=pl.ANY)
FLAT_W = 1024


def _my_place():
    return lax.axis_index('x'), lax.axis_index('y'), lax.axis_index('c')


def _other_chips(x, y):
    return [(1 - x, y), (x, 1 - y), (1 - x, 1 - y)]


def _gather_chips(name, arrs):
    n = len(arrs)

    def body(*refs):
        ins, outs = refs[:n], refs[n:2 * n]
        send_sems, recv_sems, pass_send_sems, pass_recv_sems, own_send_sems, own_recv_sems = refs[2 * n:]
        x, y, c = _my_place()
        me = 2 * x + y
        chips = _other_chips(x, y)

        def half(a, which):
            h = arrs[a].shape[0] // 2
            return pl.ds(which * h, h)

        def ici(j, a, chip_slot, to):
            src = ins[a].at[half(a, c)] if chip_slot is None else outs[a].at[chip_slot, half(a, c)]
            return pltpu.make_async_remote_copy(
                src_ref=src, dst_ref=outs[a].at[me if chip_slot is None else chip_slot, half(a, c)],
                send_sem=send_sems.at[j, a], recv_sem=recv_sems.at[j, a], device_id=to, device_id_type=MESH)

        def passed(j, a, chip_slot, which):
            rows = outs[a].at[chip_slot, half(a, which)]
            return pltpu.make_async_remote_copy(src_ref=rows, dst_ref=rows, send_sem=pass_send_sems.at[j, a],
                                                recv_sem=pass_recv_sems.at[j, a], device_id=(x, y, 1 - c),
                                                device_id_type=MESH)

        def own(a):
            return pltpu.make_async_remote_copy(src_ref=ins[a], dst_ref=outs[a].at[me], send_sem=own_send_sems.at[a],
                                                recv_sem=own_recv_sems.at[a], device_id=(x, y, 1 - c),
                                                device_id_type=MESH)

        copies = [own(a) for a in range(n)]
        sends = [ici(j, a, None, (px, py, c)) for j, (px, py) in enumerate(chips) for a in range(n)]
        for cp in copies + sends:
            cp.start()
        passes = []
        for j, (px, py) in enumerate(chips):
            for a in range(n):
                ici(j, a, 2 * px + py, (px, py, c)).wait_recv()
                p = passed(j, a, 2 * px + py, c)
                p.start()
                passes.append(p)
        for j, (px, py) in enumerate(chips):
            for a in range(n):
                passed(j, a, 2 * px + py, 1 - c).wait_recv()
        for s in sends + passes:
            s.wait_send()
        for cp in copies:
            cp.wait()

    sems = pltpu.SemaphoreType.DMA((3, n))
    return pl.pallas_call(
        body, name=name, in_specs=[ANY] * n, out_specs=[ANY] * n,
        out_shape=[jax.ShapeDtypeStruct((N_CHIPS,) + a.shape, a.dtype) for a in arrs],
        scratch_shapes=[sems, sems, sems, sems, pltpu.SemaphoreType.DMA((n,)), pltpu.SemaphoreType.DMA((n,))],
    )(*arrs)


def _swap_halves(name, arrs):
    n = len(arrs)

    def body(*refs):
        ins, outs = refs[:n], refs[n:2 * n]
        send_sems, recv_sems = refs[2 * n:]
        x, y, c = _my_place()
        copies = []
        for a in range(n):
            half = arrs[a].shape[1] // 2
            cp = pltpu.make_async_remote_copy(src_ref=ins[a].at[:, pl.ds((1 - c) * half, half)], dst_ref=outs[a],
                                              send_sem=send_sems.at[a], recv_sem=recv_sems.at[a],
                                              device_id=(x, y, 1 - c), device_id_type=MESH)
            cp.start()
            copies.append(cp)
        for cp in copies:
            cp.wait()

    return pl.pallas_call(
        body, name=name, in_specs=[ANY] * n, out_specs=[ANY] * n,
        out_shape=[jax.ShapeDtypeStruct((a.shape[0], a.shape[1] // 2, a.shape[2]), a.dtype) for a in arrs],
        scratch_shapes=[pltpu.SemaphoreType.DMA((n,)), pltpu.SemaphoreType.DMA((n,))],
    )(*arrs)


def _sibling_exchange(name, arrs):
    n = len(arrs)

    def body(*refs):
        ins, outs = refs[:n], refs[n:2 * n]
        send_sems, recv_sems = refs[2 * n:]
        x, y, c = _my_place()
        copies = []
        for a in range(n):
            cp = pltpu.make_async_remote_copy(src_ref=ins[a], dst_ref=outs[a], send_sem=send_sems.at[a],
                                              recv_sem=recv_sems.at[a], device_id=(x, y, 1 - c), device_id_type=MESH)
            cp.start()
            copies.append(cp)
        for cp in copies:
            cp.wait()

    return pl.pallas_call(
        body, name=name, in_specs=[ANY] * n, out_specs=[ANY] * n,
        out_shape=[jax.ShapeDtypeStruct(a.shape, a.dtype) for a in arrs],
        scratch_shapes=[pltpu.SemaphoreType.DMA((n,)), pltpu.SemaphoreType.DMA((n,))],
    )(*arrs)


def _chip_all_to_all(name, arrs):
    n = len(arrs)

    def body(*refs):
        ins, outs = refs[:n], refs[n:2 * n]
        send_sems, recv_sems = refs[2 * n:]
        x, y, c = _my_place()
        sends = []
        for j, (px, py) in enumerate(_other_chips(x, y)):
            for a in range(n):
                s = pltpu.make_async_remote_copy(src_ref=ins[a].at[2 * px + py], dst_ref=outs[a].at[j],
                                                 send_sem=send_sems.at[j, a], recv_sem=recv_sems.at[j, a],
                                                 device_id=(px, py, c), device_id_type=MESH)
                s.start()
                sends.append(s)
        for s in sends:
            s.wait()

    return pl.pallas_call(
        body, name=name, in_specs=[ANY] * n, out_specs=[ANY] * n,
        out_shape=[jax.ShapeDtypeStruct((3,) + a.shape[1:], a.dtype) for a in arrs],
        scratch_shapes=[pltpu.SemaphoreType.DMA((3, n)), pltpu.SemaphoreType.DMA((3, n))],
    )(*arrs)


def _gather_all(name, arr):
    def body(in_ref, out_ref, send_sems, recv_sems, local_sem):
        x, y, c = _my_place()
        me = 4 * x + 2 * y + c
        mine = pltpu.make_async_copy(in_ref, out_ref.at[me], local_sem)
        mine.start()
        peers = []
        for k in range(1, N_DEV):
            px = (1 - x) if k & 4 else x
            py = (1 - y) if k & 2 else y
            pc = (1 - c) if k & 1 else c
            peers.append((px, py, pc))
        sends = []
        for k, peer in enumerate(peers):
            s = pltpu.make_async_remote_copy(src_ref=in_ref, dst_ref=out_ref.at[me], send_sem=send_sems.at[k],
                                             recv_sem=recv_sems.at[k], device_id=peer, device_id_type=MESH)
            s.start()
            sends.append(s)
        for k, (px, py, pc) in enumerate(peers):
            pltpu.make_async_remote_copy(src_ref=in_ref, dst_ref=out_ref.at[4 * px + 2 * py + pc],
                                         send_sem=send_sems.at[k], recv_sem=recv_sems.at[k], device_id=(px, py, pc),
                                         device_id_type=MESH).wait_recv()
        for s in sends:
            s.wait_send()
        mine.wait()

    return pl.pallas_call(
        body, name=name, in_specs=[ANY], out_specs=ANY, out_shape=jax.ShapeDtypeStruct((N_DEV,) + arr.shape, arr.dtype),
        scratch_shapes=[pltpu.SemaphoreType.DMA((N_DEV - 1,)), pltpu.SemaphoreType.DMA((N_DEV - 1,)),
                        pltpu.SemaphoreType.DMA],
    )(arr)


def _flat_rows(r):
    return _tile(r, 512, 16)


def _add_my_half(name, whole, other, my_c, out_dtype):
    n, half, wd = other.shape
    tr = _flat_rows(half)
    nb = half // tr

    def body(c_ref, a_ref, b_ref, o_ref):
        o_ref[...] = (a_ref[...] + b_ref[...]).astype(o_ref.dtype)

    spec = pl.BlockSpec((1, tr, wd), lambda s, i, c: (s, i, 0))
    grid_spec = pltpu.PrefetchScalarGridSpec(
        num_scalar_prefetch=1, grid=(n, nb),
        in_specs=[pl.BlockSpec((1, tr, wd), lambda s, i, c: (s, c[0] * nb + i, 0)), spec], out_specs=spec)
    return pl.pallas_call(body, name=name, grid_spec=grid_spec, out_shape=jax.ShapeDtypeStruct(other.shape, out_dtype),
                          compiler_params=_cparams(("arbitrary",) * 2))(my_c.astype(jnp.int32).reshape(1), whole, other)


def _sum_rows(name, a):
    n, r, wd = a.shape
    tr = _flat_rows(r)

    def body(a_ref, o_ref):
        acc = a_ref[0].astype(F32)
        for k in range(1, n):
            acc = acc + a_ref[k].astype(F32)
        o_ref[...] = acc

    return pl.pallas_call(body, name=name, grid=(r // tr,), in_specs=[pl.BlockSpec((n, tr, wd), lambda i: (0, i, 0))],
                          out_specs=pl.BlockSpec((tr, wd), lambda i: (i, 0)), out_shape=jax.ShapeDtypeStruct((r, wd), F32),
                          compiler_params=_cparams(("arbitrary",)))(a)


def _sum_own_received(name, sums, received, me):
    _, r, wd = sums.shape
    tr = _flat_rows(r)

    def body(me_ref, own_ref, rec_ref, o_ref):
        acc = own_ref[0].astype(F32)
        for k in range(3):
            acc = acc + rec_ref[k].astype(F32)
        o_ref[...] = acc

    grid_spec = pltpu.PrefetchScalarGridSpec(
        num_scalar_prefetch=1, grid=(r // tr,),
        in_specs=[pl.BlockSpec((1, tr, wd), lambda i, me: (me[0], i, 0)), pl.BlockSpec((3, tr, wd), lambda i, me: (0, i, 0))],
        out_specs=pl.BlockSpec((tr, wd), lambda i, me: (i, 0)))
    return pl.pallas_call(body, name=name, grid_spec=grid_spec, out_shape=jax.ShapeDtypeStruct((r, wd), F32),
                          compiler_params=_cparams(("arbitrary",)))(me.astype(jnp.int32).reshape(1), sums, received)


def _adamw_math(w, g, m, v):
    m = ADAM_B1 * m + (1.0 - ADAM_B1) * g
    v = ADAM_B2 * v + (1.0 - ADAM_B2) * (g * g)
    m_hat = m / (1.0 - ADAM_B1 ** ADAM_STEP)
    v_hat = v / (1.0 - ADAM_B2 ** ADAM_STEP)
    return -ADAM_LR * (m_hat / (jnp.sqrt(v_hat) + ADAM_EPS) + ADAM_WD * w), m, v


def _adamw(name, w, g, m, v):
    r, wd = w.shape
    tr = _tile(r, 256, 8)

    def body(w_ref, g_ref, m_ref, v_ref, d_ref, nm_ref, nv_ref):
        d_ref[...], nm_ref[...], nv_ref[...] = _adamw_math(w_ref[...], g_ref[...], m_ref[...], v_ref[...])

    spec = pl.BlockSpec((tr, wd), lambda i: (i, 0))
    return pl.pallas_call(body, name=name, grid=(r // tr,), in_specs=[spec] * 4, out_specs=[spec] * 3,
                          out_shape=[jax.ShapeDtypeStruct((r, wd), F32)] * 3,
                          compiler_params=_cparams(("arbitrary",)))(w, g, m, v)


def _adamw_halves(name, w, g_mine, g_other, m, v, my_c):
    r, wd = w.shape
    half = r // 2
    tr = _tile(half, 256, 8)
    nb = half // tr

    def body(c_ref, w_ref, gm_ref, go_ref, m_ref, v_ref, g_ref, d_ref, nm_ref, nv_ref):
        g = jnp.where(pl.program_id(0) // nb == c_ref[0], gm_ref[...], go_ref[...])
        g_ref[...] = g
        d_ref[...], nm_ref[...], nv_ref[...] = _adamw_math(w_ref[...], g, m_ref[...], v_ref[...])

    spec = pl.BlockSpec((tr, wd), lambda i, c: (i, 0))
    hspec = pl.BlockSpec((tr, wd), lambda i, c: (i % nb, 0))
    grid_spec = pltpu.PrefetchScalarGridSpec(num_scalar_prefetch=1, grid=(r // tr,),
                                             in_specs=[spec, hspec, hspec, spec, spec], out_specs=[spec] * 4)
    return pl.pallas_call(body, name=name, grid_spec=grid_spec, out_shape=[jax.ShapeDtypeStruct((r, wd), F32)] * 4,
                          compiler_params=_cparams(("arbitrary",)))(my_c.astype(jnp.int32).reshape(1), w, g_mine,
                                                                    g_other, m, v)


def _to_flat(parts, dtype, row_multiple):
    flat = jnp.concatenate([p.astype(dtype).reshape(-1) for p in parts])
    unit = FLAT_W * row_multiple
    total = -(-flat.shape[0] // unit) * unit
    return jnp.pad(flat, (0, total - flat.shape[0])).reshape(total // FLAT_W, FLAT_W)


def _from_flat(flat, shapes):
    flat = flat.reshape(-1)
    out, at = [], 0
    for shp in shapes:
        n = 1
        for d in shp:
            n *= d
        out.append(flat[at:at + n].reshape(shp))
        at += n
    return out


def _shard_piece(a, axis, s):
    n = a.shape[axis] // N_CHIPS
    return lax.slice_in_dim(a, s * n, (s + 1) * n, axis=axis)


def kernel(x, c, ctx, c_ctx, w_ada, b_ada, norm1_w, norm2_w, w_in, b_gate, mla_q_norm_a, mla_w_qb, mla_kv_norm_a, mla_w_kvb, mla_q_norm, mla_k_norm, gla_w_gk2, gla_b_gk, gla_o_norm, ret_decay, w_branch, w_out, w_ffn_in, w_dw, b_dw, w_ffn_out, loss_target, m_c_ctx, m_w_ada, m_b_ada, m_norm1_w, m_norm2_w, m_w_in, m_b_gate, m_mla_q_norm_a, m_mla_w_qb, m_mla_kv_norm_a, m_mla_w_kvb, m_mla_q_norm, m_mla_k_norm, m_gla_w_gk2, m_gla_b_gk, m_gla_o_norm, m_ret_decay, m_w_branch, m_w_out, m_w_ffn_in, m_w_dw, m_b_dw, m_w_ffn_out, v_c_ctx, v_w_ada, v_b_ada, v_norm1_w, v_norm2_w, v_w_in, v_b_gate, v_mla_q_norm_a, v_mla_w_qb, v_mla_kv_norm_a, v_mla_w_kvb, v_mla_q_norm, v_mla_k_norm, v_gla_w_gk2, v_gla_b_gk, v_gla_o_norm, v_ret_decay, v_w_branch, v_w_out, v_w_ffn_in, v_w_dw, v_b_dw, v_w_ffn_out):
    local = dict(c_ctx=c_ctx, w_ada=w_ada, b_ada=b_ada, norm1_w=norm1_w, norm2_w=norm2_w, w_in=w_in, b_gate=b_gate,
                 mla_q_norm_a=mla_q_norm_a, mla_w_qb=mla_w_qb, mla_kv_norm_a=mla_kv_norm_a, mla_w_kvb=mla_w_kvb,
                 mla_q_norm=mla_q_norm, mla_k_norm=mla_k_norm, gla_w_gk2=gla_w_gk2, gla_b_gk=gla_b_gk,
                 gla_o_norm=gla_o_norm, ret_decay=ret_decay, w_branch=w_branch, w_out=w_out, w_ffn_in=w_ffn_in,
                 w_dw=w_dw, b_dw=b_dw, w_ffn_out=w_ffn_out)
    mom_m = dict(c_ctx=m_c_ctx, w_ada=m_w_ada, b_ada=m_b_ada, norm1_w=m_norm1_w, norm2_w=m_norm2_w, w_in=m_w_in,
                 b_gate=m_b_gate, mla_q_norm_a=m_mla_q_norm_a, mla_w_qb=m_mla_w_qb, mla_kv_norm_a=m_mla_kv_norm_a,
                 mla_w_kvb=m_mla_w_kvb, mla_q_norm=m_mla_q_norm, mla_k_norm=m_mla_k_norm, gla_w_gk2=m_gla_w_gk2,
                 gla_b_gk=m_gla_b_gk, gla_o_norm=m_gla_o_norm, ret_decay=m_ret_decay, w_branch=m_w_branch,
                 w_out=m_w_out, w_ffn_in=m_w_ffn_in, w_dw=m_w_dw, b_dw=m_b_dw, w_ffn_out=m_w_ffn_out)
    mom_v = dict(c_ctx=v_c_ctx, w_ada=v_w_ada, b_ada=v_b_ada, norm1_w=v_norm1_w, norm2_w=v_norm2_w, w_in=v_w_in,
                 b_gate=v_b_gate, mla_q_norm_a=v_mla_q_norm_a, mla_w_qb=v_mla_w_qb, mla_kv_norm_a=v_mla_kv_norm_a,
                 mla_w_kvb=v_mla_w_kvb, mla_q_norm=v_mla_q_norm, mla_k_norm=v_mla_k_norm, gla_w_gk2=v_gla_w_gk2,
                 gla_b_gk=v_gla_b_gk, gla_o_norm=v_gla_o_norm, ret_decay=v_ret_decay, w_branch=v_w_branch,
                 w_out=v_w_out, w_ffn_in=v_w_ffn_in, w_dw=v_w_dw, b_dw=v_b_dw, w_ffn_out=v_w_ffn_out)
    axis_of = dict(SHARDED)
    small_narrow = tuple(n for n, _ in SHARDED if n not in BIG and n not in SHARDED_F32)
    small_sharded = small_narrow + SHARDED_F32
    kinds = ('grad', 'delta', 'new_m', 'new_v')
    my_x, my_y, my_c = _my_place()
    my_chip = 2 * my_x + my_y

    gathered = _gather_chips('gather_weights', [local[n].astype(MXU_DTYPE) for n in BIG] + [
        _to_flat([local[n] for n in small_narrow], MXU_DTYPE, 32), _to_flat([local[n] for n in SHARDED_F32], F32, 16)])
    big = []
    for l in range(DEPTH):
        full_l = {n: jnp.concatenate([g[s, l] for s in range(N_CHIPS)], axis=BIG_AXIS[n]) for n, g in zip(BIG, gathered)}
        big.append({k: v.astype(F32) for k, v in _big_layer_weights(full_l).items()})
    small = {n: local[n] for n in REPLICATED}
    for names, flat4 in ((small_narrow, gathered[-2]), (SHARDED_F32, gathered[-1])):
        pieces = [_from_flat(flat4[s], [local[n].shape for n in names]) for s in range(N_CHIPS)]
        for k, n in enumerate(names):
            small[n] = jnp.concatenate([pieces[s][k] for s in range(N_CHIPS)], axis=axis_of[n]).astype(F32)

    loss_local, (grad_big, grad_small, grad_x) = jax.value_and_grad(_local_loss, argnums=(0, 1, 2))(
        big, small, x, c, ctx, loss_target)
    loss = lax.psum(loss_local, ('x', 'y', 'c'))

    per_layer = [_big_layer_grads(grad_big[l]) for l in range(DEPTH)]
    stacks = []
    for n in BIG:
        st = jnp.stack([jnp.stack([_shard_piece(per_layer[l][n], BIG_AXIS[n], s) for l in range(DEPTH)])
                        for s in range(N_CHIPS)])
        stacks.append(st.reshape(N_CHIPS, -1, st.shape[-1]))
    stacks.append(jnp.stack([_to_flat([_shard_piece(grad_small[n], axis_of[n], s) for n in small_sharded], F32, 64)
                             for s in range(N_CHIPS)]))
    labels = BIG + ('small',)
    swapped = _swap_halves('grad_swap_halves', stacks)
    chip_sums = [_add_my_half('grad_add_' + n, st, sw, my_c, MXU_DTYPE) for n, st, sw in zip(labels, stacks, swapped)]
    received = _chip_all_to_all('grad_all_to_all', chip_sums)
    mine = [_sum_own_received('grad_sum_' + n, cs, rc, my_chip) for n, cs, rc in zip(labels, chip_sums, received)]
    other = _sibling_exchange('grad_share_result', mine)

    g_rep = _sum_rows('grad_sum_replicated', _gather_all('grad_gather_replicated',
                                                          _to_flat([grad_small[n] for n in REPLICATED], F32, 8)))

    results = {}
    as_rows = lambda t: t.reshape(-1, t.shape[-1])
    for k, n in enumerate(BIG):
        res = _adamw_halves('adamw_' + n, as_rows(local[n]), mine[k], other[k], as_rows(mom_m[n]), as_rows(mom_v[n]), my_c)
        for kind, val in zip(kinds, res):
            results[kind, n] = val.reshape(local[n].shape)
    flat_small = lambda d: _to_flat([d[n] for n in small_sharded], F32, 64)
    res = _adamw_halves('adamw_small', flat_small(local), mine[-1], other[-1], flat_small(mom_m), flat_small(mom_v), my_c)
    for kind, flat in zip(kinds, res):
        for n, val in zip(small_sharded, _from_flat(flat, [local[n].shape for n in small_sharded])):
            results[kind, n] = val
    flat_rep = lambda d: _to_flat([d[n] for n in REPLICATED], F32, 8)
    upd_r = _adamw('adamw_replicated', flat_rep(local), g_rep, flat_rep(mom_m), flat_rep(mom_v))
    for kind, flat in zip(kinds, (g_rep,) + tuple(upd_r)):
        for n, val in zip(REPLICATED, _from_flat(flat, [local[n].shape for n in REPLICATED])):
            results[kind, n] = val
    out = [loss, grad_x]
    for kind in ('grad', 'delta', 'new_m', 'new_v'):
        out += [results[kind, n] for n in WEIGHT_ORDER]
    return tuple(out)
```

```python
import functools

import jax
import jax.numpy as jnp
from jax import lax
from jax.experimental import pallas as pl
from jax.experimental.pallas import tpu as pltpu

F32 = jnp.float32
MXU_DTYPE = jnp.bfloat16

DEPTH = 2
D_MODEL = 1024
GRID_W = 64
CHUNK = 64
LANES = 128
MLA_HEADS = 8
MLA_NOPE = 64
MLA_ROPE = 32
MLA_QK = MLA_NOPE + MLA_ROPE
MLA_V = 64
GLA_HEADS = 4
GLA_DK = 128
GLA_GATE_NORMALIZER = 16.0
RET_HEADS = 4
RET_DK = 128
D_FF = 2816
ROPE_THETA = 10000.0
RET_THETA = 10000.0
EPS = 1e-6
ADAM_LR = 0.001
ADAM_B1 = 0.9
ADAM_B2 = 0.999
ADAM_EPS = 1e-08
ADAM_WD = 0.01
ADAM_STEP = 10
NEG_BIG = -1e30

VMEM_LIMIT_BYTES = 56 * 1024 * 1024
WEIGHT_BLOCK_BYTES = 8 * 1024 * 1024
ACC_BLOCK_BYTES = 13 * 1024 * 1024
ACC_MAX_ROWS = 2816
MM_ROWS = 512
LOG2E = 1.4426950408889634
LN2 = 0.6931471805599453
MLA_Q_SCALE = MLA_QK ** -0.5 * LOG2E

SHARDED = (('w_ada', 2), ('w_in', 2), ('b_gate', 2), ('mla_w_qb', 2), ('mla_w_kvb', 2), ('gla_w_gk2', 3),
           ('gla_b_gk', 2), ('w_branch', 3), ('w_out', 1), ('w_ffn_in', 2), ('w_dw', 2), ('w_ffn_out', 1))
SHARDED_F32 = ('b_gate', 'gla_b_gk', 'w_dw')
REPLICATED = ('c_ctx', 'b_ada', 'norm1_w', 'norm2_w', 'mla_q_norm_a', 'mla_kv_norm_a', 'mla_q_norm', 'mla_k_norm',
              'gla_o_norm', 'ret_decay', 'b_dw')
WEIGHT_ORDER = ('c_ctx', 'w_ada', 'b_ada', 'norm1_w', 'norm2_w', 'w_in', 'b_gate', 'mla_q_norm_a', 'mla_w_qb',
                'mla_kv_norm_a', 'mla_w_kvb', 'mla_q_norm', 'mla_k_norm', 'gla_w_gk2', 'gla_b_gk', 'gla_o_norm',
                'ret_decay', 'w_branch', 'w_out', 'w_ffn_in', 'w_dw', 'b_dw', 'w_ffn_out')
N_CHIPS = 4
N_DEV = 8
MESH = pl.DeviceIdType.MESH


def _cparams(sem):
    return pltpu.CompilerParams(dimension_semantics=sem, vmem_limit_bytes=VMEM_LIMIT_BYTES)


def _tile(n, target, unit):
    best = None
    for t in range(unit, min(n, target) + 1, unit):
        if n % t == 0:
            best = t
    return n if best is None else best


_DN = {'nn': (((1,), (0,)), ((), ())), 'nt': (((1,), (1,)), ((), ())), 'tn': (((0,), (0,)), ((), ()))}


def _raw_mm(x, y, form):
    return lax.dot_general(x.astype(MXU_DTYPE), y.astype(MXU_DTYPE), _DN[form], preferred_element_type=F32)


@functools.partial(jax.custom_vjp, nondiff_argnums=(2,))
def _mm(x, y, form):
    return _raw_mm(x, y, form)


def _mm_fwd(x, y, form):
    return _raw_mm(x, y, form), (x, y)


def _mm_bwd(form, res, g):
    x, y = res
    if form == 'nn':
        dx, dy = _mm(g, y, 'nt'), _mm(x, g, 'tn')
    elif form == 'nt':
        dx, dy = _mm(g, y, 'nn'), _mm(g, x, 'tn')
    else:
        dx, dy = _mm(y, g, 'nt'), _mm(x, g, 'nn')
    return dx.astype(x.dtype), dy.astype(y.dtype)


_mm.defvjp(_mm_fwd, _mm_bwd)


@functools.partial(jax.custom_vjp, nondiff_argnums=(1, 2))
def _roll(x, shift, axis):
    return pltpu.roll(x, shift, axis)


def _roll_fwd(x, shift, axis):
    return pltpu.roll(x, shift, axis), None


def _roll_bwd(shift, axis, _, g):
    return (pltpu.roll(g, (g.shape[axis] - shift) % g.shape[axis], axis),)


_roll.defvjp(_roll_fwd, _roll_bwd)


def _tri(n, upper):
    i = lax.broadcasted_iota(jnp.int32, (n, n), 0)
    j = lax.broadcasted_iota(jnp.int32, (n, n), 1)
    return jnp.where((j >= i) if upper else (j <= i), 1.0, 0.0).astype(F32)


def _tri_mm(n, upper, x):
    return jnp.dot(_tri(n, upper), x, precision=lax.Precision.HIGHEST, preferred_element_type=F32)


@functools.partial(jax.custom_vjp, nondiff_argnums=(1,))
def _cumsum_rows(x, reverse):
    return _tri_mm(x.shape[0], reverse, x)


def _cumsum_fwd(x, reverse):
    return _tri_mm(x.shape[0], reverse, x), None


def _cumsum_bwd(reverse, _, g):
    return (_tri_mm(g.shape[0], not reverse, g),)


_cumsum_rows.defvjp(_cumsum_fwd, _cumsum_bwd)


def _rms(x, n=None):
    n = x.shape[-1] if n is None else n
    return x * lax.rsqrt(jnp.sum(x * x, axis=-1, keepdims=True) / n + EPS)


def _mod_row(i, mod16, b_ada):
    m = mod16[0:8] + b_ada
    return jnp.where(i == 0, m[0:1], m[1:2])


def _row_spec(tb, spec):
    arr, cb, width = spec
    return pl.BlockSpec((tb, width), lambda i, cb=cb: (i, cb))


def _whole_spec(arr):
    nd = arr.ndim
    return pl.BlockSpec(arr.shape, lambda i, nd=nd: (0,) * nd)


def _rw_fwd(name, fn, rows, params, outs, tb):
    t = rows[0][0].shape[0]
    nr, npar = len(rows), len(params)

    def body(*refs):
        i = pl.program_id(0)
        rv = [r[...] for r in refs[:nr]]
        pv = [p[...] for p in refs[nr:nr + npar]]
        res = fn(i, rv, pv)
        for o_ref, val in zip(refs[nr + npar:], res):
            o_ref[...] = val.astype(o_ref.dtype)

    return pl.pallas_call(
        body, name=name, grid=(t // tb,),
        in_specs=[_row_spec(tb, s) for s in rows] + [_whole_spec(p) for p in params],
        out_specs=[pl.BlockSpec((tb, w), lambda i: (i, 0)) for w, _ in outs],
        out_shape=[jax.ShapeDtypeStruct((t, w), dt) for w, dt in outs],
        compiler_params=_cparams(("arbitrary",)),
    )(*[s[0] for s in rows], *params)


def _rw_bwd(name, fn, rows, params, gouts, tb, diff_rows):
    t = rows[0][0].shape[0]
    nr, npar, ng, nd = len(rows), len(params), len(gouts), len(diff_rows)

    def body(*refs):
        i = pl.program_id(0)
        rv = [r[...] for r in refs[:nr]]
        pv = [p[...] for p in refs[nr:nr + npar]]
        gv = [g[...].astype(F32) for g in refs[nr + npar:nr + npar + ng]]
        out_refs = refs[nr + npar + ng:]

        def f(dr, pvals):
            vals = list(rv)
            for k, idx in enumerate(diff_rows):
                vals[idx] = dr[k]
            return tuple(fn(i, vals, pvals))

        _, vjp = jax.vjp(f, [rv[k].astype(F32) for k in diff_rows], pv)
        drows, dpars = vjp(tuple(gv))
        for k in range(nd):
            out_refs[k][...] = drows[k]

        @pl.when(i == 0)
        def _():
            for k in range(npar):
                out_refs[nd + k][...] = jnp.zeros_like(out_refs[nd + k])

        for k in range(npar):
            out_refs[nd + k][...] += dpars[k]

    res = pl.pallas_call(
        body, name=name, grid=(t // tb,),
        in_specs=([_row_spec(tb, s) for s in rows] + [_whole_spec(p) for p in params]
                  + [pl.BlockSpec((tb, g.shape[1]), lambda i: (i, 0)) for g in gouts]),
        out_specs=([pl.BlockSpec((tb, rows[k][2]), lambda i: (i, 0)) for k in diff_rows]
                   + [_whole_spec(p) for p in params]),
        out_shape=([jax.ShapeDtypeStruct((t, rows[k][2]), F32) for k in diff_rows]
                   + [jax.ShapeDtypeStruct(p.shape, F32) for p in params]),
        compiler_params=_cparams(("arbitrary",)),
    )(*[s[0] for s in rows], *params, *gouts)
    return list(res[:nd]), list(res[nd:])


def _full(arr):
    return (arr, 0, arr.shape[1])


def _make_rw_op(fn_factory, n_rows, diff_rows, out_widths):
    @functools.partial(jax.custom_vjp, nondiff_argnums=(0, 1))
    def op(cfg, tb, *args):
        return tuple(_rw_fwd(cfg[0] + '_fwd', fn_factory(cfg), [_full(a) for a in args[:n_rows]], list(args[n_rows:]),
                             [(w, F32) for w in out_widths(cfg, args)], tb))

    def fwd(cfg, tb, *args):
        return op(cfg, tb, *args), args

    def bwd(cfg, tb, args, g):
        drows, dpars = _rw_bwd(cfg[0] + '_bwd', fn_factory(cfg), [_full(a) for a in args[:n_rows]],
                               list(args[n_rows:]), list(g), tb, diff_rows)
        full = [jnp.zeros_like(a) for a in args[:n_rows]]
        for k, idx in enumerate(diff_rows):
            full[idx] = drows[k]
        return tuple(full) + tuple(dpars)

    op.defvjp(fwd, bwd)
    return op


def _silu_fn(cfg):
    return lambda i, rows, params: (jax.nn.silu(rows[0]),)


_silu_op = _make_rw_op(_silu_fn, 1, (0,), lambda cfg, args: (args[0].shape[1],))


def _normmod_fn(cfg):
    _, shift_at, scale_at = cfg

    def fn(i, rows, params):
        (h,) = rows
        nw, mod16, b_ada = params
        mr = _mod_row(i, mod16, b_ada)
        d = h.shape[1]
        return (_rms(h) * nw * (1.0 + mr[:, scale_at * d:(scale_at + 1) * d]) + mr[:, shift_at * d:(shift_at + 1) * d],)

    return fn


_normmod_op = _make_rw_op(_normmod_fn, 1, (0,), lambda cfg, args: (args[0].shape[1],))


def _resid_fn(cfg):
    _, gate_at = cfg

    def fn(i, rows, params):
        h, y = rows
        mod16, b_ada = params
        mr = _mod_row(i, mod16, b_ada)
        d = h.shape[1]
        return (h + mr[:, gate_at * d:(gate_at + 1) * d] * y,)

    return fn


_resid_op = _make_rw_op(_resid_fn, 2, (0, 1), lambda cfg, args: (args[0].shape[1],))


def _resid_norm_fn(cfg):
    _, gate_at, shift_at, scale_at = cfg

    def fn(i, rows, params):
        h, y = rows
        mod_r, b_r, nw, mod_n, b_n = params
        d = h.shape[1]
        h = h + _mod_row(i, mod_r, b_r)[:, gate_at * d:(gate_at + 1) * d] * y
        mn = _mod_row(i, mod_n, b_n)
        return h, _rms(h) * nw * (1.0 + mn[:, scale_at * d:(scale_at + 1) * d]) + mn[:, shift_at * d:(shift_at + 1) * d]

    return fn


_resid_norm_op = _make_rw_op(_resid_norm_fn, 2, (0, 1), lambda cfg, args: (args[0].shape[1],) * 2)


def _merge_fn(cfg):
    def fn(i, rows, params):
        z0, z1, z2, pg = rows
        (bg,) = params
        d = z0.shape[1]
        out = None
        for n, z in enumerate((z0, z1, z2)):
            term = jax.nn.sigmoid(pg[:, n * d:(n + 1) * d] + bg[:, n * d:(n + 1) * d]) * z
            out = term if out is None else out + term
        return (out,)

    return fn


_merge_op = _make_rw_op(_merge_fn, 4, (0, 1, 2, 3), lambda cfg, args: (args[0].shape[1],))


def _matmul(name, a, b, form):
    if form == 'nn':
        (m, k), (_, n) = a.shape, b.shape
        tm = _tile(m, MM_ROWS, 8)
        tn = _tile(n, max(LANES, WEIGHT_BLOCK_BYTES // (k * b.dtype.itemsize)), LANES)

        def body(a_ref, b_ref, o_ref):
            o_ref[...] = _raw_mm(a_ref[...], b_ref[...], 'nn')

        return pl.pallas_call(
            body, name=name, grid=(n // tn, m // tm),
            in_specs=[pl.BlockSpec((tm, k), lambda j, i: (i, 0)), pl.BlockSpec((k, tn), lambda j, i: (0, j))],
            out_specs=pl.BlockSpec((tm, tn), lambda j, i: (i, j)),
            out_shape=jax.ShapeDtypeStruct((m, n), F32), compiler_params=_cparams(("arbitrary", "arbitrary")),
        )(a, b)
    if form == 'nt':
        (m, n), (k, _) = a.shape, b.shape
        tm = _tile(m, MM_ROWS, 8)
        tk = _tile(k, max(LANES, WEIGHT_BLOCK_BYTES // (n * b.dtype.itemsize)), LANES)

        def body(a_ref, b_ref, o_ref):
            o_ref[...] = _raw_mm(a_ref[...], b_ref[...], 'nt')

        return pl.pallas_call(
            body, name=name, grid=(k // tk, m // tm),
            in_specs=[pl.BlockSpec((tm, n), lambda j, i: (i, 0)), pl.BlockSpec((tk, n), lambda j, i: (j, 0))],
            out_specs=pl.BlockSpec((tm, tk), lambda j, i: (i, j)),
            out_shape=jax.ShapeDtypeStruct((m, k), F32), compiler_params=_cparams(("arbitrary", "arbitrary")),
        )(a, b)
    (m, ka), (_, n) = a.shape, b.shape
    tka = _tile(ka, ACC_MAX_ROWS, LANES)
    tn, tmc = _tile(n, max(LANES, ACC_BLOCK_BYTES // (4 * tka)), LANES), _tile(m, MM_ROWS, 8)

    def body(a_ref, b_ref, o_ref):
        @pl.when(pl.program_id(2) == 0)
        def _():
            o_ref[...] = jnp.zeros_like(o_ref)

        o_ref[...] += _raw_mm(a_ref[...], b_ref[...], 'tn')

    return pl.pallas_call(
        body, name=name, grid=(ka // tka, n // tn, m // tmc),
        in_specs=[pl.BlockSpec((tmc, tka), lambda i, j, s: (s, i)), pl.BlockSpec((tmc, tn), lambda i, j, s: (s, j))],
        out_specs=pl.BlockSpec((tka, tn), lambda i, j, s: (i, j)),
        out_shape=jax.ShapeDtypeStruct((ka, n), F32), compiler_params=_cparams(("arbitrary", "arbitrary", "arbitrary")),
    )(a, b)


@functools.partial(jax.custom_vjp, nondiff_argnums=(0,))
def _mm_op(name, a, w):
    return _matmul(name + '_fwd', a, w.astype(MXU_DTYPE), 'nn')


def _mm_op_fwd(name, a, w):
    wb = w.astype(MXU_DTYPE)
    return _matmul(name + '_fwd', a, wb, 'nn'), (a, wb)


def _mm_op_bwd(name, res, g):
    a, wb = res
    return _matmul(name + '_da', g, wb, 'nt'), _matmul(name + '_dw', a, g, 'tn')


_mm_op.defvjp(_mm_op_fwd, _mm_op_bwd)


def _matmul_nt_sum(name, gs, ws):
    m, k, n = gs[0].shape[0], ws[0].shape[0], len(gs)
    tm = _tile(m, 256, 8)
    row_bytes = sum(w.shape[1] * w.dtype.itemsize for w in ws)
    tk = _tile(k, max(LANES, WEIGHT_BLOCK_BYTES // row_bytes), LANES)

    def body(*refs):
        acc = _raw_mm(refs[0][...], refs[n][...], 'nt')
        for p in range(1, n):
            acc = acc + _raw_mm(refs[p][...], refs[n + p][...], 'nt')
        refs[2 * n][...] = acc

    return pl.pallas_call(
        body, name=name, grid=(k // tk, m // tm),
        in_specs=([pl.BlockSpec((tm, g.shape[1]), lambda j, i: (i, 0)) for g in gs]
                  + [pl.BlockSpec((tk, w.shape[1]), lambda j, i: (j, 0)) for w in ws]),
        out_specs=pl.BlockSpec((tm, tk), lambda j, i: (i, j)),
        out_shape=jax.ShapeDtypeStruct((m, k), F32), compiler_params=_cparams(("arbitrary", "arbitrary")),
    )(*gs, *ws)


@functools.partial(jax.custom_vjp, nondiff_argnums=(0,))
def _proj_op(names, a, ws):
    return tuple(_matmul(nm + '_fwd', a, w.astype(MXU_DTYPE), 'nn') for nm, w in zip(names, ws))


def _proj_op_fwd(names, a, ws):
    wbs = tuple(w.astype(MXU_DTYPE) for w in ws)
    return tuple(_matmul(nm + '_fwd', a, wb, 'nn') for nm, wb in zip(names, wbs)), (a, wbs)


def _proj_op_bwd(names, res, gs):
    a, wbs = res
    da = _matmul_nt_sum(names[0] + '_da_all', list(gs), list(wbs))
    return da, tuple(_matmul(nm + '_dw', a, g, 'tn') for nm, g in zip(names, gs))


_proj_op.defvjp(_proj_op_fwd, _proj_op_bwd)


def _rope128(x, c, a, b):
    return x * c + _roll(x, LANES - 8, 1) * a + _roll(x, 8, 1) * b


def _mla_prep_fn(i, rows, params):
    pm, c, a, b = rows
    qna, wqb, kvna, wkn, wv, qn, kn = params
    cq, ckv, kr_slot = pm[:, 0:256], pm[:, 256:384], pm[:, 384:512]
    q_all = _mm(_rms(cq) * qna, wqb, 'nn')
    ckvn = _rms(ckv) * kvna
    k_all = _mm(ckvn, wkn, 'nn')
    v_all = _mm(ckvn, wv, 'nn')
    qs, ks = [], []
    for h in range(MLA_HEADS):
        sl = slice(LANES * h, LANES * (h + 1))
        qs.append(_rope128(_rms(q_all[:, sl], MLA_QK) * qn, c, a, b) * MLA_Q_SCALE)
        ks.append(_rope128(_rms(k_all[:, sl] + kr_slot, MLA_QK) * kn, c, a, b))
    return jnp.concatenate(qs, axis=1), jnp.concatenate(ks, axis=1), v_all


def _attn_fwd(name, q, k, v, tb, ctx_len):
    t = q.shape[0]

    def body(q_ref, k_ref, v_ref, o_ref, lse_ref):
        qi = pl.program_id(1)

        def attend(k, v):
            s2 = _raw_mm(q_ref[...], k, 'nt')
            m2 = jnp.max(s2, axis=-1, keepdims=True)
            p = jnp.exp2(s2 - m2)
            l = jnp.sum(p, axis=-1, keepdims=True)
            o_ref[...] = _raw_mm(p, v, 'nn') / l
            lse_ref[...] = jnp.broadcast_to((m2 + jnp.log2(l)) * LN2, lse_ref.shape)

        @pl.when(qi == 0)
        def _():
            attend(k_ref[0:ctx_len, :], v_ref[0:ctx_len, :])

        @pl.when(qi != 0)
        def _():
            attend(k_ref[...], v_ref[...])

    blk = pl.BlockSpec((tb, LANES), lambda h, i: (i, h))
    whole = pl.BlockSpec((t, LANES), lambda h, i: (0, h))
    return pl.pallas_call(
        body, name=name, grid=(MLA_HEADS, t // tb), in_specs=[blk, whole, whole], out_specs=[blk, blk],
        out_shape=[jax.ShapeDtypeStruct(q.shape, F32)] * 2, compiler_params=_cparams(("arbitrary", "arbitrary")),
    )(q, k, v)


def _attn_bwd(name, q, k, v, o, lse, do, tb, ctx_len):
    t = q.shape[0]
    ck = _tile(t, 2816, 256)

    def body(q_ref, k_ref, v_ref, o_ref, lse_ref, do_ref, dq_ref, dk_ref, dv_ref):
        qi = pl.program_id(1)

        @pl.when(qi == 0)
        def _():
            dk_ref[...] = jnp.zeros_like(dk_ref)
            dv_ref[...] = jnp.zeros_like(dv_ref)

        q = q_ref[...]
        do = do_ref[...].astype(MXU_DTYPE)
        lse2 = lse_ref[...][:, 0:1] * LOG2E
        delta = jnp.sum(do_ref[...] * o_ref[...], axis=-1, keepdims=True)

        def part(rows):
            ks, vs = k_ref[rows, :], v_ref[rows, :]
            p = jnp.exp2(_raw_mm(q, ks, 'nt') - lse2)
            ds = p * ((_raw_mm(do, vs, 'nt') - delta) * LN2)
            dk_ref[rows, :] += _raw_mm(ds, q, 'tn')
            dv_ref[rows, :] += _raw_mm(p, do, 'tn')
            return _raw_mm(ds, ks, 'nn')

        @pl.when(qi == 0)
        def _():
            dq_ref[...] = part(pl.ds(0, ctx_len))

        @pl.when(qi != 0)
        def _():
            dq = part(pl.ds(0, ck))
            for c in range(1, t // ck):
                dq = dq + part(pl.ds(c * ck, ck))
            dq_ref[...] = dq

    blk = pl.BlockSpec((tb, LANES), lambda h, i: (i, h))
    whole = pl.BlockSpec((t, LANES), lambda h, i: (0, h))
    return pl.pallas_call(
        body, name=name, grid=(MLA_HEADS, t // tb), in_specs=[blk, whole, whole, blk, blk, blk],
        out_specs=[blk, whole, whole], out_shape=[jax.ShapeDtypeStruct(q.shape, F32)] * 3,
        compiler_params=_cparams(("arbitrary", "arbitrary")),
    )(q, k, v, o, lse, do)


def _mla_rows(pm, tabs):
    return [(pm, 0, pm.shape[1])] + [_full(x) for x in tabs]


@functools.partial(jax.custom_vjp, nondiff_argnums=(0, 1))
def _mla_branch(name, tb, pm, tabs, params):
    return _mla_branch_fwd(name, tb, pm, tabs, params)[0]


def _mla_branch_fwd(name, tb, pm, tabs, params):
    w = LANES * MLA_HEADS
    q, k, v = _rw_fwd(name + '_prep', _mla_prep_fn, _mla_rows(pm, tabs), list(params), [(w, MXU_DTYPE)] * 3, tb)
    o, lse = _attn_fwd(name + '_attn', q, k, v, tb, tb)
    return o, (pm, tabs, params, q, k, v, o, lse)


def _mla_branch_bwd(name, tb, res, do):
    pm, tabs, params, q, k, v, o, lse = res
    dq, dk, dv = _attn_bwd(name + '_attn_bwd', q, k, v, o, lse, do, tb, tb)
    (dpm,), dpars = _rw_bwd(name + '_prep_bwd', _mla_prep_fn, _mla_rows(pm, tabs), list(params), [dq, dk, dv], tb, (0,))
    return dpm, tuple(jnp.zeros_like(x) for x in tabs), tuple(dpars)


_mla_branch.defvjp(_mla_branch_fwd, _mla_branch_bwd)


def _chunk_masks(reverse):
    i = lax.broadcasted_iota(jnp.int32, (CHUNK, CHUNK), 0)
    j = lax.broadcasted_iota(jnp.int32, (CHUNK, CHUNK), 1)
    return i, j, ((j > i) if reverse else (j <= i))


def _gla_chunk(reverse, rows, params, st0):
    q, k, v, la = rows
    q = q * (GLA_DK ** -0.5)
    cum = _cumsum_rows(la, reverse)
    tot = cum[0:1] if reverse else cum[CHUNK - 1:CHUNK]
    st1 = st0 * jnp.exp(tot) + _mm(v, k * jnp.exp(tot - cum), 'tn')
    qd = q * jnp.exp(cum)
    _, _, mask = _chunk_masks(reverse)
    att = jnp.where(mask, _mm(qd, k * jnp.exp(-cum), 'nt'), 0.0)
    return _mm(att, v, 'nn') + _mm(qd, st0, 'nt'), st1


def _ret_chunk(reverse, rows, params, st0):
    q, k, v, cc, ss = rows
    (lg,) = params
    q = q * cc + _roll(q, RET_DK // 2, 1) * ss
    k = (k * cc + _roll(k, RET_DK // 2, 1) * ss) * (RET_DK ** -0.5)
    r = lax.broadcasted_iota(jnp.int32, (CHUNK, LANES), 0).astype(F32)
    zeta = jnp.exp((r if reverse else (CHUNK - 1.0 - r)) * lg)
    xi = jnp.exp(((CHUNK - r) if reverse else (r + 1.0)) * lg)
    st1 = st0 * jnp.exp(CHUNK * lg) + _mm(v, k * zeta, 'tn')
    i, j, mask = _chunk_masks(reverse)
    rel = jnp.where(mask, (j - i) if reverse else (i - j), 0).astype(F32)
    dmat = jnp.where(mask, jnp.exp(rel * lg[:, 0:CHUNK]), 0.0)
    att = _mm(q, k, 'nt') * dmat
    return _mm(att, v, 'nn') + _mm(q, st0, 'nt') * xi, st1


def _scan_order(reverse, nblk):
    if reverse:
        return lambda t: jnp.where(t == 0, 0, nblk - t)
    return lambda t: t


def _scan_specs(rows, params, tb, heads, blk_of):
    def rspec(spec):
        _, cb = spec
        if cb is None:
            return pl.BlockSpec((tb, LANES), lambda s: (blk_of(s), 0))
        return pl.BlockSpec((tb, heads * LANES), lambda s, cb=cb: (blk_of(s), cb // heads))

    return [rspec(s) for s in rows] + [pl.BlockSpec((heads, 1, LANES), lambda s: (0, 0, 0)) for _ in params]


def _head_rows(row_refs, rows, sl, h):
    lanes = pl.ds(h * LANES, LANES)
    return [r[sl, :] if spec[1] is None else r[sl, lanes] for r, spec in zip(row_refs, rows)]


def _scan_fwd(name, chunk_fn, reverse, rows, params, tb, heads):
    t = rows[0][0].shape[0]
    nblk, cpb = t // tb, tb // CHUNK
    blk_of = _scan_order(reverse, nblk)
    nr, npar = len(rows), len(params)
    order = list(range(cpb))[::-1] if reverse else list(range(cpb))

    def body(*refs):
        row_refs, par_refs = refs[:nr], refs[nr:nr + npar]
        o_ref, st_out_ref, st_ref = refs[nr + npar:]

        @pl.when(pl.program_id(0) == 0)
        def _():
            st_ref[...] = jnp.zeros_like(st_ref)

        for c in order:
            sl = pl.ds(c * CHUNK, CHUNK)
            for h in range(heads):
                st0 = st_ref[h]
                st_out_ref[h, c] = st0
                o, st1 = chunk_fn(reverse, _head_rows(row_refs, rows, sl, h), [p[h] for p in par_refs], st0)
                o_ref[sl, pl.ds(h * LANES, LANES)] = o
                st_ref[h] = st1

    return pl.pallas_call(
        body, name=name, grid=(nblk,), in_specs=_scan_specs(rows, params, tb, heads, blk_of),
        out_specs=[pl.BlockSpec((tb, heads * LANES), lambda s: (blk_of(s), 0)),
                   pl.BlockSpec((heads, cpb, LANES, LANES), lambda s: (0, blk_of(s), 0, 0))],
        out_shape=[jax.ShapeDtypeStruct((t, heads * LANES), F32),
                   jax.ShapeDtypeStruct((heads, t // CHUNK, LANES, LANES), F32)],
        scratch_shapes=[pltpu.VMEM((heads, LANES, LANES), F32)],
        compiler_params=_cparams(("arbitrary",)),
    )(*[s[0] for s in rows], *params)


def _scan_bwd(name, chunk_fn, reverse, rows, params, states, do, tb, heads, n_diff):
    t = rows[0][0].shape[0]
    nblk, cpb = t // tb, tb // CHUNK
    fwd_blk = _scan_order(reverse, nblk)
    blk_of = lambda s: fwd_blk(nblk - 1 - s)
    nr, npar = len(rows), len(params)
    order = list(range(cpb)) if reverse else list(range(cpb))[::-1]

    def body(*refs):
        row_refs, par_refs = refs[:nr], refs[nr:nr + npar]
        st_in_ref, do_ref = refs[nr + npar:nr + npar + 2]
        out_refs = refs[nr + npar + 2:-1]
        dst_ref = refs[-1]

        @pl.when(pl.program_id(0) == 0)
        def _():
            dst_ref[...] = jnp.zeros_like(dst_ref)
            for k in range(npar):
                out_refs[n_diff + k][...] = jnp.zeros_like(out_refs[n_diff + k])

        for c in order:
            sl = pl.ds(c * CHUNK, CHUNK)
            for h in range(heads):
                lanes = pl.ds(h * LANES, LANES)
                rv = _head_rows(row_refs, rows, sl, h)

                def f(dr, pvals, st0, rv=rv):
                    return chunk_fn(reverse, list(dr) + rv[n_diff:], pvals, st0)

                _, vjp = jax.vjp(f, rv[:n_diff], [p[h] for p in par_refs], st_in_ref[h, c])
                drows, dpars, dst0 = vjp((do_ref[sl, lanes], dst_ref[h]))
                for k in range(n_diff):
                    out_refs[k][sl, lanes] = drows[k]
                for k in range(npar):
                    out_refs[n_diff + k][h] += dpars[k]
                dst_ref[h] = dst0

    wide = pl.BlockSpec((tb, heads * LANES), lambda s: (blk_of(s), 0))
    pblk = pl.BlockSpec((heads, 1, LANES), lambda s: (0, 0, 0))
    res = pl.pallas_call(
        body, name=name, grid=(nblk,),
        in_specs=(_scan_specs(rows, params, tb, heads, blk_of)
                  + [pl.BlockSpec((heads, cpb, LANES, LANES), lambda s: (0, blk_of(s), 0, 0)), wide]),
        out_specs=[wide] * n_diff + [pblk for _ in params],
        out_shape=([jax.ShapeDtypeStruct((t, heads * LANES), F32)] * n_diff
                   + [jax.ShapeDtypeStruct(p.shape, F32) for p in params]),
        scratch_shapes=[pltpu.VMEM((heads, LANES, LANES), F32)],
        compiler_params=_cparams(("arbitrary",)),
    )(*[s[0] for s in rows], *params, states, do)
    return list(res[:n_diff]), list(res[n_diff:])


def _gla_la_fn(i, rows, params):
    (r,) = rows
    w2f, w2b, bgk = params
    la_f = jax.nn.log_sigmoid(_mm(r, w2f, 'nn') + bgk[0:1]) / GLA_GATE_NORMALIZER
    la_b = jax.nn.log_sigmoid(_mm(r, w2b, 'nn') + bgk[1:2]) / GLA_GATE_NORMALIZER
    return la_f, la_b


def _headnorm_fn(heads, with_weight):
    def fn(i, rows, params):
        o_f, o_b, g = rows
        outs = []
        for h in range(heads):
            sl = slice(LANES * h, LANES * (h + 1))
            y = _rms(o_f[:, sl] + o_b[:, sl])
            outs.append(y * params[0] if with_weight else y)
        return (jnp.concatenate(outs, axis=1) * jax.nn.silu(g),)

    return fn


@functools.partial(jax.custom_vjp, nondiff_argnums=(0, 1))
def _gla_branch(name, tb, pg, params):
    return _gla_branch_fwd(name, tb, pg, params)[0]


def _gla_rows(pg, la):
    return [(pg, 0), (pg, GLA_HEADS), (pg, 2 * GLA_HEADS), (la, 0)]


def _gla_branch_fwd(name, tb, pg, params):
    w2f, w2b, bgk, onorm = params
    w = GLA_HEADS * LANES
    la_f, la_b = _rw_fwd(name + '_la', _gla_la_fn, [(pg, 4 * w // LANES, LANES)], [w2f, w2b, bgk], [(w, F32)] * 2, tb)
    o_f, st_f = _scan_fwd(name + '_scan_f', _gla_chunk, False, _gla_rows(pg, la_f), [], tb, GLA_HEADS)
    o_b, st_b = _scan_fwd(name + '_scan_b', _gla_chunk, True, _gla_rows(pg, la_b), [], tb, GLA_HEADS)
    (y,) = _rw_fwd(name + '_norm', _headnorm_fn(GLA_HEADS, True), [_full(o_f), _full(o_b), (pg, 3, w)], [onorm],
                   [(w, F32)], tb)
    return y, (pg, params, la_f, la_b, o_f, o_b, st_f, st_b)


def _gla_branch_bwd(name, tb, res, dy):
    pg, params, la_f, la_b, o_f, o_b, st_f, st_b = res
    w2f, w2b, bgk, onorm = params
    w = GLA_HEADS * LANES
    (do_f, do_b, dg), (donorm,) = _rw_bwd(name + '_norm_bwd', _headnorm_fn(GLA_HEADS, True),
                                          [_full(o_f), _full(o_b), (pg, 3, w)], [onorm], [dy], tb, (0, 1, 2))
    (dq_f, dk_f, dv_f, dla_f), _ = _scan_bwd(name + '_scan_f_bwd', _gla_chunk, False, _gla_rows(pg, la_f), [], st_f,
                                             do_f, tb, GLA_HEADS, 4)
    (dq_b, dk_b, dv_b, dla_b), _ = _scan_bwd(name + '_scan_b_bwd', _gla_chunk, True, _gla_rows(pg, la_b), [], st_b,
                                             do_b, tb, GLA_HEADS, 4)
    (dr,), (dw2f, dw2b, dbgk) = _rw_bwd(name + '_la_bwd', _gla_la_fn, [(pg, 4 * w // LANES, LANES)], [w2f, w2b, bgk],
                                        [dla_f, dla_b], tb, (0,))
    dpg = jnp.concatenate([dq_f + dq_b, dk_f + dk_b, dv_f + dv_b, dg, dr], axis=1)
    return dpg, (dw2f, dw2b, dbgk, donorm)


_gla_branch.defvjp(_gla_branch_fwd, _gla_branch_bwd)


@functools.partial(jax.custom_vjp, nondiff_argnums=(0, 1))
def _ret_branch(name, tb, pr, tabs, lg):
    return _ret_branch_fwd(name, tb, pr, tabs, lg)[0]


def _ret_rows(pr, tabs):
    return [(pr, 0), (pr, RET_HEADS), (pr, 2 * RET_HEADS), (tabs[0], None), (tabs[1], None)]


def _ret_branch_fwd(name, tb, pr, tabs, lg):
    w = RET_HEADS * LANES
    o_f, st_f = _scan_fwd(name + '_scan_f', _ret_chunk, False, _ret_rows(pr, tabs), [lg[0]], tb, RET_HEADS)
    o_b, st_b = _scan_fwd(name + '_scan_b', _ret_chunk, True, _ret_rows(pr, tabs), [lg[1]], tb, RET_HEADS)
    (y,) = _rw_fwd(name + '_norm', _headnorm_fn(RET_HEADS, False), [_full(o_f), _full(o_b), (pr, 3, w)], [],
                   [(w, F32)], tb)
    return y, (pr, tabs, lg, o_f, o_b, st_f, st_b)


def _ret_branch_bwd(name, tb, res, dy):
    pr, tabs, lg, o_f, o_b, st_f, st_b = res
    w = RET_HEADS * LANES
    (do_f, do_b, dg), _ = _rw_bwd(name + '_norm_bwd', _headnorm_fn(RET_HEADS, False),
                                  [_full(o_f), _full(o_b), (pr, 3, w)], [], [dy], tb, (0, 1, 2))
    (dq_f, dk_f, dv_f), (dlg_f,) = _scan_bwd(name + '_scan_f_bwd', _ret_chunk, False, _ret_rows(pr, tabs), [lg[0]],
                                             st_f, do_f, tb, RET_HEADS, 3)
    (dq_b, dk_b, dv_b), (dlg_b,) = _scan_bwd(name + '_scan_b_bwd', _ret_chunk, True, _ret_rows(pr, tabs), [lg[1]],
                                             st_b, do_b, tb, RET_HEADS, 3)
    dpr = jnp.concatenate([dq_f + dq_b, dk_f + dk_b, dv_f + dv_b, dg], axis=1)
    return dpr, tuple(jnp.zeros_like(x) for x in tabs), jnp.stack([dlg_f, dlg_b])


_ret_branch.defvjp(_ret_branch_fwd, _ret_branch_bwd)


HALO = 8


def _halo_specs(tb, nblk, width, col_block):
    r = tb // HALO
    prev = pl.BlockSpec((HALO, width), lambda i: (jnp.maximum(i * r - 1, 0), col_block))
    nxt = pl.BlockSpec((HALO, width), lambda i: (jnp.minimum((i + 1) * r, nblk * r - 1), col_block))
    return prev, nxt


def _shifted(x, prev_blk, next_blk, i, nblk):
    tb = x.shape[0]
    row = lax.broadcasted_iota(jnp.int32, x.shape, 0)
    prev_row = jnp.where(i >= 2, prev_blk[HALO - 1:HALO], 0.0)
    next_row = jnp.where((i >= 1) & (i < nblk - 1), next_blk[0:1], 0.0)
    down = jnp.where(row == 0, prev_row, pltpu.roll(x, 1, 0))
    up = jnp.where(row == tb - 1, next_row, pltpu.roll(x, tb - 1, 0))
    return down, up


def _gelu_up(c, up):
    return jax.nn.gelu(c) * up


def _convact_fwd_call(name, gu, w_dw, b_dw, tb):
    t = gu.shape[0]
    nblk = t // tb
    prev_spec, next_spec = _halo_specs(tb, nblk, D_FF, 0)

    def body(g_ref, up_ref, prev_ref, next_ref, w_ref, b_ref, o_ref):
        i = pl.program_id(0)
        g = g_ref[...]
        down, upw = _shifted(g, prev_ref[...], next_ref[...], i, nblk)
        w = w_ref[...]
        c = w[0:1] * down + w[1:2] * g + w[2:3] * upw + b_ref[...]
        o_ref[...] = _gelu_up(c, up_ref[...])

    return pl.pallas_call(
        body, name=name, grid=(nblk,),
        in_specs=[pl.BlockSpec((tb, D_FF), lambda i: (i, 0)), pl.BlockSpec((tb, D_FF), lambda i: (i, 1)), prev_spec,
                  next_spec, _whole_spec(w_dw), _whole_spec(b_dw)],
        out_specs=pl.BlockSpec((tb, D_FF), lambda i: (i, 0)), out_shape=jax.ShapeDtypeStruct((t, D_FF), F32),
        compiler_params=_cparams(("arbitrary",)),
    )(gu, gu, gu, gu, w_dw, b_dw)


def _convact_bwd_calls(name, gu, w_dw, b_dw, dact, tb):
    t = gu.shape[0]
    nblk = t // tb
    prev_spec, next_spec = _halo_specs(tb, nblk, D_FF, 0)

    def body1(g_ref, up_ref, prev_ref, next_ref, w_ref, b_ref, da_ref, dc_ref, dup_ref, dw_ref, db_ref):
        i = pl.program_id(0)
        g = g_ref[...]
        down, upw = _shifted(g, prev_ref[...], next_ref[...], i, nblk)
        w = w_ref[...]
        c = w[0:1] * down + w[1:2] * g + w[2:3] * upw + b_ref[...]
        _, vjp = jax.vjp(_gelu_up, c, up_ref[...])
        dc, dup = vjp(da_ref[...])
        dc_ref[...] = dc
        dup_ref[...] = dup

        @pl.when(i == 0)
        def _():
            dw_ref[...] = jnp.zeros_like(dw_ref)
            db_ref[...] = jnp.zeros_like(db_ref)

        dw_ref[0:1, :] += jnp.sum(dc * down, axis=0, keepdims=True)
        dw_ref[1:2, :] += jnp.sum(dc * g, axis=0, keepdims=True)
        dw_ref[2:3, :] += jnp.sum(dc * upw, axis=0, keepdims=True)
        db_ref[...] += jnp.sum(dc, axis=0, keepdims=True)

    blk = pl.BlockSpec((tb, D_FF), lambda i: (i, 0))
    dc, dup, dw, db = pl.pallas_call(
        body1, name=name + '_a', grid=(nblk,),
        in_specs=[blk, pl.BlockSpec((tb, D_FF), lambda i: (i, 1)), prev_spec, next_spec, _whole_spec(w_dw),
                  _whole_spec(b_dw), blk],
        out_specs=[blk, blk, _whole_spec(w_dw), _whole_spec(b_dw)],
        out_shape=[jax.ShapeDtypeStruct((t, D_FF), F32)] * 2 + [jax.ShapeDtypeStruct(w_dw.shape, F32),
                                                                jax.ShapeDtypeStruct(b_dw.shape, F32)],
        compiler_params=_cparams(("arbitrary",)),
    )(gu, gu, gu, gu, w_dw, b_dw, dact)

    def body2(dc_ref, prev_ref, next_ref, dup_ref, w_ref, o_ref):
        i = pl.program_id(0)
        dc_blk = dc_ref[...]
        down, upw = _shifted(dc_blk, prev_ref[...], next_ref[...], i, nblk)
        w = w_ref[...]
        o_ref[:, 0:D_FF] = w[0:1] * upw + w[1:2] * dc_blk + w[2:3] * down
        o_ref[:, D_FF:2 * D_FF] = dup_ref[...]

    dgu = pl.pallas_call(
        body2, name=name + '_b', grid=(nblk,),
        in_specs=[blk, prev_spec, next_spec, blk, _whole_spec(w_dw)],
        out_specs=pl.BlockSpec((tb, 2 * D_FF), lambda i: (i, 0)), out_shape=jax.ShapeDtypeStruct((t, 2 * D_FF), F32),
        compiler_params=_cparams(("arbitrary",)),
    )(dc, dc, dc, dup, w_dw)
    return dgu, dw, db


@functools.partial(jax.custom_vjp, nondiff_argnums=(0, 1))
def _convact(name, tb, gu, w_dw, b_dw):
    return _convact_fwd_call(name + '_fwd', gu, w_dw, b_dw, tb)


def _convact_fwd(name, tb, gu, w_dw, b_dw):
    return _convact_fwd_call(name + '_fwd', gu, w_dw, b_dw, tb), (gu, w_dw, b_dw)


def _convact_bwd(name, tb, res, dact):
    gu, w_dw, b_dw = res
    return _convact_bwd_calls(name + '_bwd', gu, w_dw, b_dw, dact, tb)


_convact.defvjp(_convact_fwd, _convact_bwd)


def _loss_fwd_call(h, target, tb):
    nlat = target.shape[0] // tb

    def body(h_ref, t_ref, o_ref):
        @pl.when(pl.program_id(0) == 0)
        def _():
            o_ref[...] = jnp.zeros_like(o_ref)

        e = h_ref[...] - t_ref[...]
        o_ref[...] += jnp.sum(e * e, axis=0, keepdims=True)

    cols = pl.pallas_call(
        body, name='loss_fwd', grid=(nlat,),
        in_specs=[pl.BlockSpec((tb, D_MODEL), lambda i: (i + 1, 0)), pl.BlockSpec((tb, D_MODEL), lambda i: (i, 0))],
        out_specs=pl.BlockSpec((1, D_MODEL), lambda i: (0, 0)), out_shape=jax.ShapeDtypeStruct((1, D_MODEL), F32),
        compiler_params=_cparams(("arbitrary",)),
    )(h, target)
    return (0.5 / D_MODEL) * jnp.sum(cols)


def _loss_bwd_call(h, target, gbar, tb):
    def body(g_ref, h_ref, t_ref, o_ref):
        live = jnp.where(pl.program_id(0) == 0, 0.0, g_ref[...] * (1.0 / D_MODEL))
        o_ref[...] = live * (h_ref[...] - t_ref[...])

    return pl.pallas_call(
        body, name='loss_bwd', grid=(h.shape[0] // tb,),
        in_specs=[pl.BlockSpec((1, 1), lambda i: (0, 0)), pl.BlockSpec((tb, D_MODEL), lambda i: (i, 0)),
                  pl.BlockSpec((tb, D_MODEL), lambda i: (jnp.maximum(i - 1, 0), 0))],
        out_specs=pl.BlockSpec((tb, D_MODEL), lambda i: (i, 0)), out_shape=jax.ShapeDtypeStruct(h.shape, F32),
        compiler_params=_cparams(("arbitrary",)),
    )(gbar.reshape(1, 1), h, target)


@functools.partial(jax.custom_vjp, nondiff_argnums=(0,))
def _loss_op(tb, h, target):
    return _loss_fwd_call(h, target, tb)


def _loss_op_fwd(tb, h, target):
    return _loss_fwd_call(h, target, tb), (h, target)


def _loss_op_bwd(tb, res, gbar):
    h, target = res
    return _loss_bwd_call(h, target, gbar, tb), jnp.zeros_like(target)


_loss_op.defvjp(_loss_op_fwd, _loss_op_bwd)


def _rope_tables(pos, dim, theta):
    inv = theta ** (-jnp.arange(dim // 2, dtype=F32) * 2.0 / dim)
    ang = pos.astype(F32)[:, None] * inv[None, :]
    return jnp.cos(ang), jnp.sin(ang)


def _mla_tables(seq, ctx_len):
    rows = seq // GRID_W
    row_pos = jnp.repeat(jnp.arange(rows), GRID_W)
    col_pos = jnp.tile(jnp.arange(GRID_W), rows)
    cos_r, sin_r = _rope_tables(row_pos, MLA_ROPE // 2, ROPE_THETA)
    cos_c, sin_c = _rope_tables(col_pos, MLA_ROPE // 2, ROPE_THETA)
    one = jnp.ones((seq, MLA_NOPE), F32)
    z8 = jnp.zeros((seq, 8), F32)
    pad1 = jnp.ones((seq, LANES - MLA_QK), F32)
    pad0 = jnp.zeros((seq, LANES - MLA_QK), F32)
    z64 = jnp.zeros((seq, MLA_NOPE), F32)
    c = jnp.concatenate([one, cos_r, cos_r, cos_c, cos_c, pad1], axis=1)
    a = jnp.concatenate([z64, -sin_r, z8, -sin_c, z8, pad0], axis=1)
    b = jnp.concatenate([z64, z8, sin_r, z8, sin_c, pad0], axis=1)
    ctx_rows = lambda fill: jnp.full((ctx_len, LANES), fill, F32)
    return (jnp.concatenate([ctx_rows(1.0), c]), jnp.concatenate([ctx_rows(0.0), a]),
            jnp.concatenate([ctx_rows(0.0), b]))


def _ret_tables(total):
    inv = 1.0 / (RET_THETA ** jnp.linspace(0.0, 1.0, RET_DK // 2, dtype=F32))
    ang = jnp.arange(total).astype(F32)[:, None] * inv[None, :]
    cos, sin = jnp.cos(ang), jnp.sin(ang)
    return jnp.concatenate([cos, cos], axis=1), jnp.concatenate([-sin, sin], axis=1)


def _head_slots(w, heads, width):
    k = w.shape[0]
    return jnp.pad(w.reshape(k, heads, width), ((0, 0), (0, 0), (0, LANES - width))).reshape(k, heads * LANES)


def _pad_lanes(v, width):
    return jnp.pad(v, (0, LANES - width)).reshape(1, LANES)


IN_MLA_QKV = (0, 384)
IN_MLA_KR = (384, 416)
IN_GLA = (416, 2496)
IN_RET = (2496, 4544)
IN_GATES = (4544, 7616)
W_MLA_COLS = 512
W_GLA_COLS = 2176
BIG = ('w_ada', 'w_in', 'w_branch', 'w_out', 'w_ffn_in', 'w_ffn_out')
BIG_AXIS = dict(w_ada=1, w_in=1, w_branch=2, w_out=0, w_ffn_in=1, w_ffn_out=0)


def _big_layer_weights(full):
    w_in = full['w_in']
    zc = lambda n: jnp.zeros((D_MODEL, n), w_in.dtype)
    wbr = full['w_branch']
    wb_mla = jnp.pad(wbr[0].reshape(MLA_HEADS, MLA_V, D_MODEL), ((0, 0), (0, LANES - MLA_V), (0, 0)))
    return dict(
        ada=full['w_ada'],
        in_mla=jnp.concatenate([w_in[:, slice(*IN_MLA_QKV)], zc(MLA_NOPE), w_in[:, slice(*IN_MLA_KR)],
                                zc(LANES - MLA_QK)], axis=1),
        in_gla=jnp.concatenate([w_in[:, slice(*IN_GLA)], zc(W_GLA_COLS - (IN_GLA[1] - IN_GLA[0]))], axis=1),
        in_ret=w_in[:, slice(*IN_RET)], in_gate=w_in[:, slice(*IN_GATES)],
        br_mla=wb_mla.reshape(MLA_HEADS * LANES, D_MODEL), br_gla=wbr[1], br_ret=wbr[2],
        out=full['w_out'], ffn_in=full['w_ffn_in'], ffn_out=full['w_ffn_out'])


def _big_layer_grads(g):
    n_gla = IN_GLA[1] - IN_GLA[0]
    kr_at = IN_MLA_QKV[1] + MLA_NOPE
    w_in = jnp.concatenate([g['in_mla'][:, slice(*IN_MLA_QKV)], g['in_mla'][:, kr_at:kr_at + MLA_ROPE],
                            g['in_gla'][:, :n_gla], g['in_ret'], g['in_gate']], axis=1)
    br_mla = g['br_mla'].reshape(MLA_HEADS, LANES, D_MODEL)[:, :MLA_V].reshape(MLA_HEADS * MLA_V, D_MODEL)
    return dict(w_ada=g['ada'], w_in=w_in, w_branch=jnp.stack([br_mla, g['br_gla'], g['br_ret']]),
                w_out=g['out'], w_ffn_in=g['ffn_in'], w_ffn_out=g['ffn_out'])


def _row(v):
    return v.reshape(1, -1)


def _mixers(l, big, w, a, tabs_mla, tabs_ret, tb):
    nm = 'l%d_' % l
    row = _row
    pm, pg, pr, pgate = _proj_op(tuple(nm + s for s in ('in_mla', 'in_gla', 'in_ret', 'in_gate')), a,
                                 (big['in_mla'], big['in_gla'], big['in_ret'], big['in_gate']))

    kvb = w['mla_w_kvb'][l].reshape(-1, MLA_HEADS, MLA_NOPE + MLA_V)
    kdim = kvb.shape[0]
    mla_params = (row(w['mla_q_norm_a'][l]), _head_slots(w['mla_w_qb'][l], MLA_HEADS, MLA_QK),
                  row(w['mla_kv_norm_a'][l]), _head_slots(kvb[:, :, :MLA_NOPE].reshape(kdim, -1), MLA_HEADS, MLA_NOPE),
                  _head_slots(kvb[:, :, MLA_NOPE:].reshape(kdim, -1), MLA_HEADS, MLA_V),
                  _pad_lanes(w['mla_q_norm'][l], MLA_QK), _pad_lanes(w['mla_k_norm'][l], MLA_QK))
    y_mla = _mla_branch(nm + 'mla', tb, pm, tabs_mla, mla_params)

    gk2 = w['gla_w_gk2'][l]
    rank = gk2.shape[1]
    w2f = jnp.pad(gk2[0], ((0, LANES - rank), (0, 0)))
    w2b = jnp.pad(gk2[1], ((rank, LANES - 2 * rank), (0, 0)))
    y_gla = _gla_branch(nm + 'gla', tb, pg, (w2f, w2b, w['gla_b_gk'][l], row(w['gla_o_norm'][l])))

    log_g = -jnp.exp(w['ret_decay'][l])
    lg = jnp.broadcast_to(log_g[:, :, None, None], (2, RET_HEADS, 1, LANES))
    y_ret = _ret_branch(nm + 'ret', tb, pr, tabs_ret, lg)

    z0 = _mm_op(nm + 'br_mla', y_mla, big['br_mla'])
    z1 = _mm_op(nm + 'br_gla', y_gla, big['br_gla'])
    z2 = _mm_op(nm + 'br_ret', y_ret, big['br_ret'])
    u = _merge_op((nm + 'merge',), tb, z0, z1, z2, pgate, row(w['b_gate'][l]))[0]
    return _mm_op(nm + 'out', u, big['out'])


def _ffn(l, big, w, a2, tb):
    nm = 'l%d_' % l
    gu = _mm_op(nm + 'ffn_in', a2, big['ffn_in'])
    act = _convact(nm + 'convact', tb, gu, w['w_dw'][l], _row(w['b_dw'][l]))
    return _mm_op(nm + 'ffn_out', act, big['ffn_out'])


def _local_loss(big, w, x, c, ctx, target):
    seq, tb = x.shape[1], ctx.shape[1]
    h = jnp.concatenate([ctx[0], x[0]], axis=0)
    cond_in = jnp.concatenate([w['c_ctx'].reshape(1, -1), c, jnp.zeros((14, D_MODEL), F32)], axis=0)
    cond16 = _silu_op(('cond_silu',), 16, cond_in)[0]
    tabs_mla = _mla_tables(seq, tb)
    tabs_ret = _ret_tables(seq + tb)
    mods = [_mm_op('l%d_ada' % l, cond16, big[l]['ada']) for l in range(DEPTH)]
    b_ada = [_row(w['b_ada'][l]) for l in range(DEPTH)]
    a = _normmod_op(('l0_norm1', 0, 1), tb, h, _row(w['norm1_w'][0]), mods[0], b_ada[0])[0]
    for l in range(DEPTH):
        nm = 'l%d_' % l
        y = _mixers(l, big[l], w, a, tabs_mla, tabs_ret, tb)
        h, a2 = _resid_norm_op((nm + 'res1_norm2', 2, 3, 4), tb, h, y, mods[l], b_ada[l], _row(w['norm2_w'][l]),
                               mods[l], b_ada[l])
        f = _ffn(l, big[l], w, a2, tb)
        if l + 1 < DEPTH:
            h, a = _resid_norm_op((nm + 'res2_norm1', 5, 0, 1), tb, h, f, mods[l], b_ada[l],
                                  _row(w['norm1_w'][l + 1]), mods[l + 1], b_ada[l + 1])
        else:
            h = _resid_op((nm + 'res2', 5), tb, h, f, mods[l], b_ada[l])[0]
    return _loss_op(tb, h, target[0])


ANY = pl.BlockSpec(memory_space=pl.ANY)
FLAT_W = 1024


def _my_place():
    return lax.axis_index('x'), lax.axis_index('y'), lax.axis_index('c')


def _other_chips(x, y):
    return [(1 - x, y), (x, 1 - y), (1 - x, 1 - y)]


def _gather_chips(name, arrs):
    n = len(arrs)

    def body(*refs):
        ins, outs = refs[:n], refs[n:2 * n]
        send_sems, recv_sems, pass_send_sems, pass_recv_sems, own_send_sems, own_recv_sems = refs[2 * n:]
        x, y, c = _my_place()
        me = 2 * x + y
        chips = _other_chips(x, y)

        def half(a, which):
            h = arrs[a].shape[0] // 2
            return pl.ds(which * h, h)

        def ici(j, a, chip_slot, to):
            src = ins[a].at[half(a, c)] if chip_slot is None else outs[a].at[chip_slot, half(a, c)]
            return pltpu.make_async_remote_copy(
                src_ref=src, dst_ref=outs[a].at[me if chip_slot is None else chip_slot, half(a, c)],
                send_sem=send_sems.at[j, a], recv_sem=recv_sems.at[j, a], device_id=to, device_id_type=MESH)

        def passed(j, a, chip_slot, which):
            rows = outs[a].at[chip_slot, half(a, which)]
            return pltpu.make_async_remote_copy(src_ref=rows, dst_ref=rows, send_sem=pass_send_sems.at[j, a],
                                                recv_sem=pass_recv_sems.at[j, a], device_id=(x, y, 1 - c),
                                                device_id_type=MESH)

        def own(a):
            return pltpu.make_async_remote_copy(src_ref=ins[a], dst_ref=outs[a].at[me], send_sem=own_send_sems.at[a],
                                                recv_sem=own_recv_sems.at[a], device_id=(x, y, 1 - c),
                                                device_id_type=MESH)

        copies = [own(a) for a in range(n)]
        sends = [ici(j, a, None, (px, py, c)) for j, (px, py) in enumerate(chips) for a in range(n)]
        for cp in copies + sends:
            cp.start()
        passes = []
        for j, (px, py) in enumerate(chips):
            for a in range(n):
                ici(j, a, 2 * px + py, (px, py, c)).wait_recv()
                p = passed(j, a, 2 * px + py, c)
                p.start()
                passes.append(p)
        for j, (px, py) in enumerate(chips):
            for a in range(n):
                passed(j, a, 2 * px + py, 1 - c).wait_recv()
        for s in sends + passes:
            s.wait_send()
        for cp in copies:
            cp.wait()

    sems = pltpu.SemaphoreType.DMA((3, n))
    return pl.pallas_call(
        body, name=name, in_specs=[ANY] * n, out_specs=[ANY] * n,
        out_shape=[jax.ShapeDtypeStruct((N_CHIPS,) + a.shape, a.dtype) for a in arrs],
        scratch_shapes=[sems, sems, sems, sems, pltpu.SemaphoreType.DMA((n,)), pltpu.SemaphoreType.DMA((n,))],
    )(*arrs)


def _swap_halves(name, arrs):
    n = len(arrs)

    def body(*refs):
        ins, outs = refs[:n], refs[n:2 * n]
        send_sems, recv_sems = refs[2 * n:]
        x, y, c = _my_place()
        copies = []
        for a in range(n):
            half = arrs[a].shape[1] // 2
            cp = pltpu.make_async_remote_copy(src_ref=ins[a].at[:, pl.ds((1 - c) * half, half)], dst_ref=outs[a],
                                              send_sem=send_sems.at[a], recv_sem=recv_sems.at[a],
                                              device_id=(x, y, 1 - c), device_id_type=MESH)
            cp.start()
            copies.append(cp)
        for cp in copies:
            cp.wait()

    return pl.pallas_call(
        body, name=name, in_specs=[ANY] * n, out_specs=[ANY] * n,
        out_shape=[jax.ShapeDtypeStruct((a.shape[0], a.shape[1] // 2, a.shape[2]), a.dtype) for a in arrs],
        scratch_shapes=[pltpu.SemaphoreType.DMA((n,)), pltpu.SemaphoreType.DMA((n,))],
    )(*arrs)


def _sibling_exchange(name, arrs):
    n = len(arrs)

    def body(*refs):
        ins, outs = refs[:n], refs[n:2 * n]
        send_sems, recv_sems = refs[2 * n:]
        x, y, c = _my_place()
        copies = []
        for a in range(n):
            cp = pltpu.make_async_remote_copy(src_ref=ins[a], dst_ref=outs[a], send_sem=send_sems.at[a],
                                              recv_sem=recv_sems.at[a], device_id=(x, y, 1 - c), device_id_type=MESH)
            cp.start()
            copies.append(cp)
        for cp in copies:
            cp.wait()

    return pl.pallas_call(
        body, name=name, in_specs=[ANY] * n, out_specs=[ANY] * n,
        out_shape=[jax.ShapeDtypeStruct(a.shape, a.dtype) for a in arrs],
        scratch_shapes=[pltpu.SemaphoreType.DMA((n,)), pltpu.SemaphoreType.DMA((n,))],
    )(*arrs)


def _chip_all_to_all(name, arrs):
    n = len(arrs)

    def body(*refs):
        ins, outs = refs[:n], refs[n:2 * n]
        send_sems, recv_sems = refs[2 * n:]
        x, y, c = _my_place()
        sends = []
        for j, (px, py) in enumerate(_other_chips(x, y)):
            for a in range(n):
                s = pltpu.make_async_remote_copy(src_ref=ins[a].at[2 * px + py], dst_ref=outs[a].at[j],
                                                 send_sem=send_sems.at[j, a], recv_sem=recv_sems.at[j, a],
                                                 device_id=(px, py, c), device_id_type=MESH)
                s.start()
                sends.append(s)
        for s in sends:
            s.wait()

    return pl.pallas_call(
        body, name=name, in_specs=[ANY] * n, out_specs=[ANY] * n,
        out_shape=[jax.ShapeDtypeStruct((3,) + a.shape[1:], a.dtype) for a in arrs],
        scratch_shapes=[pltpu.SemaphoreType.DMA((3, n)), pltpu.SemaphoreType.DMA((3, n))],
    )(*arrs)


def _gather_all(name, arr):
    def body(in_ref, out_ref, send_sems, recv_sems, local_sem):
        x, y, c = _my_place()
        me = 4 * x + 2 * y + c
        mine = pltpu.make_async_copy(in_ref, out_ref.at[me], local_sem)
        mine.start()
        peers = []
        for k in range(1, N_DEV):
            px = (1 - x) if k & 4 else x
            py = (1 - y) if k & 2 else y
            pc = (1 - c) if k & 1 else c
            peers.append((px, py, pc))
        sends = []
        for k, peer in enumerate(peers):
            s = pltpu.make_async_remote_copy(src_ref=in_ref, dst_ref=out_ref.at[me], send_sem=send_sems.at[k],
                                             recv_sem=recv_sems.at[k], device_id=peer, device_id_type=MESH)
            s.start()
            sends.append(s)
        for k, (px, py, pc) in enumerate(peers):
            pltpu.make_async_remote_copy(src_ref=in_ref, dst_ref=out_ref.at[4 * px + 2 * py + pc],
                                         send_sem=send_sems.at[k], recv_sem=recv_sems.at[k], device_id=(px, py, pc),
                                         device_id_type=MESH).wait_recv()
        for s in sends:
            s.wait_send()
        mine.wait()

    return pl.pallas_call(
        body, name=name, in_specs=[ANY], out_specs=ANY, out_shape=jax.ShapeDtypeStruct((N_DEV,) + arr.shape, arr.dtype),
        scratch_shapes=[pltpu.SemaphoreType.DMA((N_DEV - 1,)), pltpu.SemaphoreType.DMA((N_DEV - 1,)),
                        pltpu.SemaphoreType.DMA],
    )(arr)


def _flat_rows(r):
    return _tile(r, 512, 16)


def _add_my_half(name, whole, other, my_c, out_dtype):
    n, half, wd = other.shape
    tr = _flat_rows(half)
    nb = half // tr

    def body(c_ref, a_ref, b_ref, o_ref):
        o_ref[...] = (a_ref[...] + b_ref[...]).astype(o_ref.dtype)

    spec = pl.BlockSpec((1, tr, wd), lambda s, i, c: (s, i, 0))
    grid_spec = pltpu.PrefetchScalarGridSpec(
        num_scalar_prefetch=1, grid=(n, nb),
        in_specs=[pl.BlockSpec((1, tr, wd), lambda s, i, c: (s, c[0] * nb + i, 0)), spec], out_specs=spec)
    return pl.pallas_call(body, name=name, grid_spec=grid_spec, out_shape=jax.ShapeDtypeStruct(other.shape, out_dtype),
                          compiler_params=_cparams(("arbitrary",) * 2))(my_c.astype(jnp.int32).reshape(1), whole, other)


def _sum_rows(name, a):
    n, r, wd = a.shape
    tr = _flat_rows(r)

    def body(a_ref, o_ref):
        acc = a_ref[0].astype(F32)
        for k in range(1, n):
            acc = acc + a_ref[k].astype(F32)
        o_ref[...] = acc

    return pl.pallas_call(body, name=name, grid=(r // tr,), in_specs=[pl.BlockSpec((n, tr, wd), lambda i: (0, i, 0))],
                          out_specs=pl.BlockSpec((tr, wd), lambda i: (i, 0)), out_shape=jax.ShapeDtypeStruct((r, wd), F32),
                          compiler_params=_cparams(("arbitrary",)))(a)


def _sum_own_received(name, sums, received, me):
    _, r, wd = sums.shape
    tr = _flat_rows(r)

    def body(me_ref, own_ref, rec_ref, o_ref):
        acc = own_ref[0].astype(F32)
        for k in range(3):
            acc = acc + rec_ref[k].astype(F32)
        o_ref[...] = acc

    grid_spec = pltpu.PrefetchScalarGridSpec(
        num_scalar_prefetch=1, grid=(r // tr,),
        in_specs=[pl.BlockSpec((1, tr, wd), lambda i, me: (me[0], i, 0)), pl.BlockSpec((3, tr, wd), lambda i, me: (0, i, 0))],
        out_specs=pl.BlockSpec((tr, wd), lambda i, me: (i, 0)))
    return pl.pallas_call(body, name=name, grid_spec=grid_spec, out_shape=jax.ShapeDtypeStruct((r, wd), F32),
                          compiler_params=_cparams(("arbitrary",)))(me.astype(jnp.int32).reshape(1), sums, received)


def _adamw_math(w, g, m, v):
    m = ADAM_B1 * m + (1.0 - ADAM_B1) * g
    v = ADAM_B2 * v + (1.0 - ADAM_B2) * (g * g)
    m_hat = m / (1.0 - ADAM_B1 ** ADAM_STEP)
    v_hat = v / (1.0 - ADAM_B2 ** ADAM_STEP)
    return -ADAM_LR * (m_hat / (jnp.sqrt(v_hat) + ADAM_EPS) + ADAM_WD * w), m, v


def _adamw(name, w, g, m, v):
    r, wd = w.shape
    tr = _tile(r, 256, 8)

    def body(w_ref, g_ref, m_ref, v_ref, d_ref, nm_ref, nv_ref):
        d_ref[...], nm_ref[...], nv_ref[...] = _adamw_math(w_ref[...], g_ref[...], m_ref[...], v_ref[...])

    spec = pl.BlockSpec((tr, wd), lambda i: (i, 0))
    return pl.pallas_call(body, name=name, grid=(r // tr,), in_specs=[spec] * 4, out_specs=[spec] * 3,
                          out_shape=[jax.ShapeDtypeStruct((r, wd), F32)] * 3,
                          compiler_params=_cparams(("arbitrary",)))(w, g, m, v)


def _adamw_halves(name, w, g_mine, g_other, m, v, my_c):
    r, wd = w.shape
    half = r // 2
    tr = _tile(half, 256, 8)
    nb = half // tr

    def body(c_ref, w_ref, gm_ref, go_ref, m_ref, v_ref, g_ref, d_ref, nm_ref, nv_ref):
        g = jnp.where(pl.program_id(0) // nb == c_ref[0], gm_ref[...], go_ref[...])
        g_ref[...] = g
        d_ref[...], nm_ref[...], nv_ref[...] = _adamw_math(w_ref[...], g, m_ref[...], v_ref[...])

    spec = pl.BlockSpec((tr, wd), lambda i, c: (i, 0))
    hspec = pl.BlockSpec((tr, wd), lambda i, c: (i % nb, 0))
    grid_spec = pltpu.PrefetchScalarGridSpec(num_scalar_prefetch=1, grid=(r // tr,),
                                             in_specs=[spec, hspec, hspec, spec, spec], out_specs=[spec] * 4)
    return pl.pallas_call(body, name=name, grid_spec=grid_spec, out_shape=[jax.ShapeDtypeStruct((r, wd), F32)] * 4,
                          compiler_params=_cparams(("arbitrary",)))(my_c.astype(jnp.int32).reshape(1), w, g_mine,
                                                                    g_other, m, v)


def _to_flat(parts, dtype, row_multiple):
    flat = jnp.concatenate([p.astype(dtype).reshape(-1) for p in parts])
    unit = FLAT_W * row_multiple
    total = -(-flat.shape[0] // unit) * unit
    return jnp.pad(flat, (0, total - flat.shape[0])).reshape(total // FLAT_W, FLAT_W)


def _from_flat(flat, shapes):
    flat = flat.reshape(-1)
    out, at = [], 0
    for shp in shapes:
        n = 1
        for d in shp:
            n *= d
        out.append(flat[at:at + n].reshape(shp))
        at += n
    return out


def _shard_piece(a, axis, s):
    n = a.shape[axis] // N_CHIPS
    return lax.slice_in_dim(a, s * n, (s + 1) * n, axis=axis)


def kernel(x, c, ctx, c_ctx, w_ada, b_ada, norm1_w, norm2_w, w_in, b_gate, mla_q_norm_a, mla_w_qb, mla_kv_norm_a, mla_w_kvb, mla_q_norm, mla_k_norm, gla_w_gk2, gla_b_gk, gla_o_norm, ret_decay, w_branch, w_out, w_ffn_in, w_dw, b_dw, w_ffn_out, loss_target, m_c_ctx, m_w_ada, m_b_ada, m_norm1_w, m_norm2_w, m_w_in, m_b_gate, m_mla_q_norm_a, m_mla_w_qb, m_mla_kv_norm_a, m_mla_w_kvb, m_mla_q_norm, m_mla_k_norm, m_gla_w_gk2, m_gla_b_gk, m_gla_o_norm, m_ret_decay, m_w_branch, m_w_out, m_w_ffn_in, m_w_dw, m_b_dw, m_w_ffn_out, v_c_ctx, v_w_ada, v_b_ada, v_norm1_w, v_norm2_w, v_w_in, v_b_gate, v_mla_q_norm_a, v_mla_w_qb, v_mla_kv_norm_a, v_mla_w_kvb, v_mla_q_norm, v_mla_k_norm, v_gla_w_gk2, v_gla_b_gk, v_gla_o_norm, v_ret_decay, v_w_branch, v_w_out, v_w_ffn_in, v_w_dw, v_b_dw, v_w_ffn_out):
    local = dict(c_ctx=c_ctx, w_ada=w_ada, b_ada=b_ada, norm1_w=norm1_w, norm2_w=norm2_w, w_in=w_in, b_gate=b_gate,
                 mla_q_norm_a=mla_q_norm_a, mla_w_qb=mla_w_qb, mla_kv_norm_a=mla_kv_norm_a, mla_w_kvb=mla_w_kvb,
                 mla_q_norm=mla_q_norm, mla_k_norm=mla_k_norm, gla_w_gk2=gla_w_gk2, gla_b_gk=gla_b_gk,
                 gla_o_norm=gla_o_norm, ret_decay=ret_decay, w_branch=w_branch, w_out=w_out, w_ffn_in=w_ffn_in,
                 w_dw=w_dw, b_dw=b_dw, w_ffn_out=w_ffn_out)
    mom_m = dict(c_ctx=m_c_ctx, w_ada=m_w_ada, b_ada=m_b_ada, norm1_w=m_norm1_w, norm2_w=m_norm2_w, w_in=m_w_in,
                 b_gate=m_b_gate, mla_q_norm_a=m_mla_q_norm_a, mla_w_qb=m_mla_w_qb, mla_kv_norm_a=m_mla_kv_norm_a,
                 mla_w_kvb=m_mla_w_kvb, mla_q_norm=m_mla_q_norm, mla_k_norm=m_mla_k_norm, gla_w_gk2=m_gla_w_gk2,
                 gla_b_gk=m_gla_b_gk, gla_o_norm=m_gla_o_norm, ret_decay=m_ret_decay, w_branch=m_w_branch,
                 w_out=m_w_out, w_ffn_in=m_w_ffn_in, w_dw=m_w_dw, b_dw=m_b_dw, w_ffn_out=m_w_ffn_out)
    mom_v = dict(c_ctx=v_c_ctx, w_ada=v_w_ada, b_ada=v_b_ada, norm1_w=v_norm1_w, norm2_w=v_norm2_w, w_in=v_w_in,
                 b_gate=v_b_gate, mla_q_norm_a=v_mla_q_norm_a, mla_w_qb=v_mla_w_qb, mla_kv_norm_a=v_mla_kv_norm_a,
                 mla_w_kvb=v_mla_w_kvb, mla_q_norm=v_mla_q_norm, mla_k_norm=v_mla_k_norm, gla_w_gk2=v_gla_w_gk2,
                 gla_b_gk=v_gla_b_gk, gla_o_norm=v_gla_o_norm, ret_decay=v_ret_decay, w_branch=v_w_branch,
                 w_out=v_w_out, w_ffn_in=v_w_ffn_in, w_dw=v_w_dw, b_dw=v_b_dw, w_ffn_out=v_w_ffn_out)
    axis_of = dict(SHARDED)
    small_narrow = tuple(n for n, _ in SHARDED if n not in BIG and n not in SHARDED_F32)
    small_sharded = small_narrow + SHARDED_F32
    kinds = ('grad', 'delta', 'new_m', 'new_v')
    my_x, my_y, my_c = _my_place()
    my_chip = 2 * my_x + my_y

    gathered = _gather_chips('gather_weights', [local[n].astype(MXU_DTYPE) for n in BIG] + [
        _to_flat([local[n] for n in small_narrow], MXU_DTYPE, 32), _to_flat([local[n] for n in SHARDED_F32], F32, 16)])
    big = []
    for l in range(DEPTH):
        full_l = {n: jnp.concatenate([g[s, l] for s in range(N_CHIPS)], axis=BIG_AXIS[n]) for n, g in zip(BIG, gathered)}
        big.append({k: v.astype(F32) for k, v in _big_layer_weights(full_l).items()})
    small = {n: local[n] for n in REPLICATED}
    for names, flat4 in ((small_narrow, gathered[-2]), (SHARDED_F32, gathered[-1])):
        pieces = [_from_flat(flat4[s], [local[n].shape for n in names]) for s in range(N_CHIPS)]
        for k, n in enumerate(names):
            small[n] = jnp.concatenate([pieces[s][k] for s in range(N_CHIPS)], axis=axis_of[n]).astype(F32)

    loss_local, (grad_big, grad_small, grad_x) = jax.value_and_grad(_local_loss, argnums=(0, 1, 2))(
        big, small, x, c, ctx, loss_target)
    loss = lax.psum(loss_local, ('x', 'y', 'c'))

    per_layer = [_big_layer_grads(grad_big[l]) for l in range(DEPTH)]
    stacks = []
    for n in BIG:
        st = jnp.stack([jnp.stack([_shard_piece(per_layer[l][n], BIG_AXIS[n], s) for l in range(DEPTH)])
                        for s in range(N_CHIPS)])
        stacks.append(st.reshape(N_CHIPS, -1, st.shape[-1]))
    stacks.append(jnp.stack([_to_flat([_shard_piece(grad_small[n], axis_of[n], s) for n in small_sharded], F32, 64)
                             for s in range(N_CHIPS)]))
    labels = BIG + ('small',)
    swapped = _swap_halves('grad_swap_halves', stacks)
    chip_sums = [_add_my_half('grad_add_' + n, st, sw, my_c, MXU_DTYPE) for n, st, sw in zip(labels, stacks, swapped)]
    received = _chip_all_to_all('grad_all_to_all', chip_sums)
    mine = [_sum_own_received('grad_sum_' + n, cs, rc, my_chip) for n, cs, rc in zip(labels, chip_sums, received)]
    other = _sibling_exchange('grad_share_result', mine)

    g_rep = _sum_rows('grad_sum_replicated', _gather_all('grad_gather_replicated',
                                                          _to_flat([grad_small[n] for n in REPLICATED], F32, 8)))

    results = {}
    as_rows = lambda t: t.reshape(-1, t.shape[-1])
    for k, n in enumerate(BIG):
        res = _adamw_halves('adamw_' + n, as_rows(local[n]), mine[k], other[k], as_rows(mom_m[n]), as_rows(mom_v[n]), my_c)
        for kind, val in zip(kinds, res):
            results[kind, n] = val.reshape(local[n].shape)
    flat_small = lambda d: _to_flat([d[n] for n in small_sharded], F32, 64)
    res = _adamw_halves('adamw_small', flat_small(local), mine[-1], other[-1], flat_small(mom_m), flat_small(mom_v), my_c)
    for kind, flat in zip(kinds, res):
        for n, val in zip(small_sharded, _from_flat(flat, [local[n].shape for n in small_sharded])):
            results[kind, n] = val
    flat_rep = lambda d: _to_flat([d[n] for n in REPLICATED], F32, 8)
    upd_r = _adamw('adamw_replicated', flat_rep(local), g_rep, flat_rep(mom_m), flat_rep(mom_v))
    for kind, flat in zip(kinds, (g_rep,) + tuple(upd_r)):
        for n, val in zip(REPLICATED, _from_flat(flat, [local[n].shape for n in REPLICATED])):
            results[kind, n] = val
    out = [loss, grad_x]
    for kind in ('grad', 'delta', 'new_m', 'new_v'):
        out += [results[kind, n] for n in WEIGHT_ORDER]
    return tuple(out)
```

```python
import functools

import jax
import jax.numpy as jnp
from jax import lax
from jax.experimental import pallas as pl
from jax.experimental.pallas import tpu as pltpu

F32 = jnp.float32
MXU_DTYPE = jnp.bfloat16

DEPTH = 2
D_MODEL = 1024
GRID_W = 64
CHUNK = 64
LANES = 128
MLA_HEADS = 8
MLA_NOPE = 64
MLA_ROPE = 32
MLA_QK = MLA_NOPE + MLA_ROPE
MLA_V = 64
GLA_HEADS = 4
GLA_DK = 128
GLA_GATE_NORMALIZER = 16.0
RET_HEADS = 4
RET_DK = 128
D_FF = 2816
ROPE_THETA = 10000.0
RET_THETA = 10000.0
EPS = 1e-6
ADAM_LR = 0.001
ADAM_B1 = 0.9
ADAM_B2 = 0.999
ADAM_EPS = 1e-08
ADAM_WD = 0.01
ADAM_STEP = 10
NEG_BIG = -1e30

VMEM_LIMIT_BYTES = 56 * 1024 * 1024
WEIGHT_BLOCK_BYTES = 8 * 1024 * 1024
ACC_BLOCK_BYTES = 13 * 1024 * 1024
ACC_MAX_ROWS = 2816
MM_ROWS = 512
LOG2E = 1.4426950408889634
LN2 = 0.6931471805599453
MLA_Q_SCALE = MLA_QK ** -0.5 * LOG2E

SHARDED = (('w_ada', 2), ('w_in', 2), ('b_gate', 2), ('mla_w_qb', 2), ('mla_w_kvb', 2), ('gla_w_gk2', 3),
           ('gla_b_gk', 2), ('w_branch', 3), ('w_out', 1), ('w_ffn_in', 2), ('w_dw', 2), ('w_ffn_out', 1))
SHARDED_F32 = ('b_gate', 'gla_b_gk', 'w_dw')
REPLICATED = ('c_ctx', 'b_ada', 'norm1_w', 'norm2_w', 'mla_q_norm_a', 'mla_kv_norm_a', 'mla_q_norm', 'mla_k_norm',
              'gla_o_norm', 'ret_decay', 'b_dw')
WEIGHT_ORDER = ('c_ctx', 'w_ada', 'b_ada', 'norm1_w', 'norm2_w', 'w_in', 'b_gate', 'mla_q_norm_a', 'mla_w_qb',
                'mla_kv_norm_a', 'mla_w_kvb', 'mla_q_norm', 'mla_k_norm', 'gla_w_gk2', 'gla_b_gk', 'gla_o_norm',
                'ret_decay', 'w_branch', 'w_out', 'w_ffn_in', 'w_dw', 'b_dw', 'w_ffn_out')
N_CHIPS = 4
N_DEV = 8
MESH = pl.DeviceIdType.MESH


def _cparams(sem):
    return pltpu.CompilerParams(dimension_semantics=sem, vmem_limit_bytes=VMEM_LIMIT_BYTES)


def _tile(n, target, unit):
    best = None
    for t in range(unit, min(n, target) + 1, unit):
        if n % t == 0:
            best = t
    return n if best is None else best


_DN = {'nn': (((1,), (0,)), ((), ())), 'nt': (((1,), (1,)), ((), ())), 'tn': (((0,), (0,)), ((), ()))}


def _raw_mm(x, y, form):
    return lax.dot_general(x.astype(MXU_DTYPE), y.astype(MXU_DTYPE), _DN[form], preferred_element_type=F32)


@functools.partial(jax.custom_vjp, nondiff_argnums=(2,))
def _mm(x, y, form):
    return _raw_mm(x, y, form)


def _mm_fwd(x, y, form):
    return _raw_mm(x, y, form), (x, y)


def _mm_bwd(form, res, g):
    x, y = res
    if form == 'nn':
        dx, dy = _mm(g, y, 'nt'), _mm(x, g, 'tn')
    elif form == 'nt':
        dx, dy = _mm(g, y, 'nn'), _mm(g, x, 'tn')
    else:
        dx, dy = _mm(y, g, 'nt'), _mm(x, g, 'nn')
    return dx.astype(x.dtype), dy.astype(y.dtype)


_mm.defvjp(_mm_fwd, _mm_bwd)


@functools.partial(jax.custom_vjp, nondiff_argnums=(1, 2))
def _roll(x, shift, axis):
    return pltpu.roll(x, shift, axis)


def _roll_fwd(x, shift, axis):
    return pltpu.roll(x, shift, axis), None


def _roll_bwd(shift, axis, _, g):
    return (pltpu.roll(g, (g.shape[axis] - shift) % g.shape[axis], axis),)


_roll.defvjp(_roll_fwd, _roll_bwd)


def _running_sum(x, reverse):
    n = x.shape[0]
    row = lax.broadcasted_iota(jnp.int32, x.shape, 0)
    d = 1
    while d < n:
        if reverse:
            x = x + jnp.where(row < n - d, pltpu.roll(x, n - d, 0), 0.0)
        else:
            x = x + jnp.where(row >= d, pltpu.roll(x, d, 0), 0.0)
        d *= 2
    return x


@functools.partial(jax.custom_vjp, nondiff_argnums=(1,))
def _cumsum_rows(x, reverse):
    return _running_sum(x, reverse)


def _cumsum_fwd(x, reverse):
    return _running_sum(x, reverse), None


def _cumsum_bwd(reverse, _, g):
    return (_running_sum(g, not reverse),)


_cumsum_rows.defvjp(_cumsum_fwd, _cumsum_bwd)


def _rms(x, n=None):
    n = x.shape[-1] if n is None else n
    return x * lax.rsqrt(jnp.sum(x * x, axis=-1, keepdims=True) / n + EPS)


def _mod_row(i, mod16, b_ada):
    m = mod16[0:8] + b_ada
    return jnp.where(i == 0, m[0:1], m[1:2])


def _row_spec(tb, spec):
    arr, cb, width = spec
    return pl.BlockSpec((tb, width), lambda i, cb=cb: (i, cb))


def _whole_spec(arr):
    nd = arr.ndim
    return pl.BlockSpec(arr.shape, lambda i, nd=nd: (0,) * nd)


def _rw_fwd(name, fn, rows, params, outs, tb):
    t = rows[0][0].shape[0]
    nr, npar = len(rows), len(params)

    def body(*refs):
        i = pl.program_id(0)
        rv = [r[...] for r in refs[:nr]]
        pv = [p[...] for p in refs[nr:nr + npar]]
        res = fn(i, rv, pv)
        for o_ref, val in zip(refs[nr + npar:], res):
            o_ref[...] = val.astype(o_ref.dtype)

    return pl.pallas_call(
        body, name=name, grid=(t // tb,),
        in_specs=[_row_spec(tb, s) for s in rows] + [_whole_spec(p) for p in params],
        out_specs=[pl.BlockSpec((tb, w), lambda i: (i, 0)) for w, _ in outs],
        out_shape=[jax.ShapeDtypeStruct((t, w), dt) for w, dt in outs],
        compiler_params=_cparams(("arbitrary",)),
    )(*[s[0] for s in rows], *params)


def _rw_bwd(name, fn, rows, params, gouts, tb, diff_rows):
    t = rows[0][0].shape[0]
    nr, npar, ng, nd = len(rows), len(params), len(gouts), len(diff_rows)

    def body(*refs):
        i = pl.program_id(0)
        rv = [r[...] for r in refs[:nr]]
        pv = [p[...] for p in refs[nr:nr + npar]]
        gv = [g[...].astype(F32) for g in refs[nr + npar:nr + npar + ng]]
        out_refs = refs[nr + npar + ng:]

        def f(dr, pvals):
            vals = list(rv)
            for k, idx in enumerate(diff_rows):
                vals[idx] = dr[k]
            return tuple(fn(i, vals, pvals))

        _, vjp = jax.vjp(f, [rv[k].astype(F32) for k in diff_rows], pv)
        drows, dpars = vjp(tuple(gv))
        for k in range(nd):
            out_refs[k][...] = drows[k]

        @pl.when(i == 0)
        def _():
            for k in range(npar):
                out_refs[nd + k][...] = jnp.zeros_like(out_refs[nd + k])

        for k in range(npar):
            out_refs[nd + k][...] += dpars[k]

    res = pl.pallas_call(
        body, name=name, grid=(t // tb,),
        in_specs=([_row_spec(tb, s) for s in rows] + [_whole_spec(p) for p in params]
                  + [pl.BlockSpec((tb, g.shape[1]), lambda i: (i, 0)) for g in gouts]),
        out_specs=([pl.BlockSpec((tb, rows[k][2]), lambda i: (i, 0)) for k in diff_rows]
                   + [_whole_spec(p) for p in params]),
        out_shape=([jax.ShapeDtypeStruct((t, rows[k][2]), F32) for k in diff_rows]
                   + [jax.ShapeDtypeStruct(p.shape, F32) for p in params]),
        compiler_params=_cparams(("arbitrary",)),
    )(*[s[0] for s in rows], *params, *gouts)
    return list(res[:nd]), list(res[nd:])


def _full(arr):
    return (arr, 0, arr.shape[1])


def _make_rw_op(fn_factory, n_rows, diff_rows, out_widths):
    @functools.partial(jax.custom_vjp, nondiff_argnums=(0, 1))
    def op(cfg, tb, *args):
        return tuple(_rw_fwd(cfg[0] + '_fwd', fn_factory(cfg), [_full(a) for a in args[:n_rows]], list(args[n_rows:]),
                             [(w, F32) for w in out_widths(cfg, args)], tb))

    def fwd(cfg, tb, *args):
        return op(cfg, tb, *args), args

    def bwd(cfg, tb, args, g):
        drows, dpars = _rw_bwd(cfg[0] + '_bwd', fn_factory(cfg), [_full(a) for a in args[:n_rows]],
                               list(args[n_rows:]), list(g), tb, diff_rows)
        full = [jnp.zeros_like(a) for a in args[:n_rows]]
        for k, idx in enumerate(diff_rows):
            full[idx] = drows[k]
        return tuple(full) + tuple(dpars)

    op.defvjp(fwd, bwd)
    return op


def _silu_fn(cfg):
    return lambda i, rows, params: (jax.nn.silu(rows[0]),)


_silu_op = _make_rw_op(_silu_fn, 1, (0,), lambda cfg, args: (args[0].shape[1],))


def _normmod_fn(cfg):
    _, shift_at, scale_at = cfg

    def fn(i, rows, params):
        (h,) = rows
        nw, mod16, b_ada = params
        mr = _mod_row(i, mod16, b_ada)
        d = h.shape[1]
        return (_rms(h) * nw * (1.0 + mr[:, scale_at * d:(scale_at + 1) * d]) + mr[:, shift_at * d:(shift_at + 1) * d],)

    return fn


_normmod_op = _make_rw_op(_normmod_fn, 1, (0,), lambda cfg, args: (args[0].shape[1],))


def _resid_fn(cfg):
    _, gate_at = cfg

    def fn(i, rows, params):
        h, y = rows
        mod16, b_ada = params
        mr = _mod_row(i, mod16, b_ada)
        d = h.shape[1]
        return (h + mr[:, gate_at * d:(gate_at + 1) * d] * y,)

    return fn


_resid_op = _make_rw_op(_resid_fn, 2, (0, 1), lambda cfg, args: (args[0].shape[1],))


def _resid_norm_fn(cfg):
    _, gate_at, shift_at, scale_at = cfg

    def fn(i, rows, params):
        h, y = rows
        mod_r, b_r, nw, mod_n, b_n = params
        d = h.shape[1]
        h = h + _mod_row(i, mod_r, b_r)[:, gate_at * d:(gate_at + 1) * d] * y
        mn = _mod_row(i, mod_n, b_n)
        return h, _rms(h) * nw * (1.0 + mn[:, scale_at * d:(scale_at + 1) * d]) + mn[:, shift_at * d:(shift_at + 1) * d]

    return fn


_resid_norm_op = _make_rw_op(_resid_norm_fn, 2, (0, 1), lambda cfg, args: (args[0].shape[1],) * 2)


def _merge_fn(cfg):
    def fn(i, rows, params):
        z0, z1, z2, pg = rows
        (bg,) = params
        d = z0.shape[1]
        out = None
        for n, z in enumerate((z0, z1, z2)):
            term = jax.nn.sigmoid(pg[:, n * d:(n + 1) * d] + bg[:, n * d:(n + 1) * d]) * z
            out = term if out is None else out + term
        return (out,)

    return fn


_merge_op = _make_rw_op(_merge_fn, 4, (0, 1, 2, 3), lambda cfg, args: (args[0].shape[1],))


def _matmul(name, a, b, form):
    if form == 'nn':
        (m, k), (_, n) = a.shape, b.shape
        tm = _tile(m, MM_ROWS, 8)
        tn = _tile(n, max(LANES, WEIGHT_BLOCK_BYTES // (k * b.dtype.itemsize)), LANES)

        def body(a_ref, b_ref, o_ref):
            o_ref[...] = _raw_mm(a_ref[...], b_ref[...], 'nn')

        return pl.pallas_call(
            body, name=name, grid=(n // tn, m // tm),
            in_specs=[pl.BlockSpec((tm, k), lambda j, i: (i, 0)), pl.BlockSpec((k, tn), lambda j, i: (0, j))],
            out_specs=pl.BlockSpec((tm, tn), lambda j, i: (i, j)),
            out_shape=jax.ShapeDtypeStruct((m, n), F32), compiler_params=_cparams(("arbitrary", "arbitrary")),
        )(a, b)
    if form == 'nt':
        (m, n), (k, _) = a.shape, b.shape
        tm = _tile(m, MM_ROWS, 8)
        tk = _tile(k, max(LANES, WEIGHT_BLOCK_BYTES // (n * b.dtype.itemsize)), LANES)

        def body(a_ref, b_ref, o_ref):
            o_ref[...] = _raw_mm(a_ref[...], b_ref[...], 'nt')

        return pl.pallas_call(
            body, name=name, grid=(k // tk, m // tm),
            in_specs=[pl.BlockSpec((tm, n), lambda j, i: (i, 0)), pl.BlockSpec((tk, n), lambda j, i: (j, 0))],
            out_specs=pl.BlockSpec((tm, tk), lambda j, i: (i, j)),
            out_shape=jax.ShapeDtypeStruct((m, k), F32), compiler_params=_cparams(("arbitrary", "arbitrary")),
        )(a, b)
    (m, ka), (_, n) = a.shape, b.shape
    tka = _tile(ka, ACC_MAX_ROWS, LANES)
    tn, tmc = _tile(n, max(LANES, ACC_BLOCK_BYTES // (4 * tka)), LANES), _tile(m, MM_ROWS, 8)

    def body(a_ref, b_ref, o_ref):
        @pl.when(pl.program_id(2) == 0)
        def _():
            o_ref[...] = jnp.zeros_like(o_ref)

        o_ref[...] += _raw_mm(a_ref[...], b_ref[...], 'tn')

    return pl.pallas_call(
        body, name=name, grid=(ka // tka, n // tn, m // tmc),
        in_specs=[pl.BlockSpec((tmc, tka), lambda i, j, s: (s, i)), pl.BlockSpec((tmc, tn), lambda i, j, s: (s, j))],
        out_specs=pl.BlockSpec((tka, tn), lambda i, j, s: (i, j)),
        out_shape=jax.ShapeDtypeStruct((ka, n), F32), compiler_params=_cparams(("arbitrary", "arbitrary", "arbitrary")),
    )(a, b)


@functools.partial(jax.custom_vjp, nondiff_argnums=(0,))
def _mm_op(name, a, w):
    return _matmul(name + '_fwd', a, w.astype(MXU_DTYPE), 'nn')


def _mm_op_fwd(name, a, w):
    wb = w.astype(MXU_DTYPE)
    return _matmul(name + '_fwd', a, wb, 'nn'), (a, wb)


def _mm_op_bwd(name, res, g):
    a, wb = res
    return _matmul(name + '_da', g, wb, 'nt'), _matmul(name + '_dw', a, g, 'tn')


_mm_op.defvjp(_mm_op_fwd, _mm_op_bwd)


def _matmul_nt_sum(name, gs, ws):
    m, k, n = gs[0].shape[0], ws[0].shape[0], len(gs)
    tm = _tile(m, 256, 8)
    row_bytes = sum(w.shape[1] * w.dtype.itemsize for w in ws)
    tk = _tile(k, max(LANES, WEIGHT_BLOCK_BYTES // row_bytes), LANES)

    def body(*refs):
        acc = _raw_mm(refs[0][...], refs[n][...], 'nt')
        for p in range(1, n):
            acc = acc + _raw_mm(refs[p][...], refs[n + p][...], 'nt')
        refs[2 * n][...] = acc

    return pl.pallas_call(
        body, name=name, grid=(k // tk, m // tm),
        in_specs=([pl.BlockSpec((tm, g.shape[1]), lambda j, i: (i, 0)) for g in gs]
                  + [pl.BlockSpec((tk, w.shape[1]), lambda j, i: (j, 0)) for w in ws]),
        out_specs=pl.BlockSpec((tm, tk), lambda j, i: (i, j)),
        out_shape=jax.ShapeDtypeStruct((m, k), F32), compiler_params=_cparams(("arbitrary", "arbitrary")),
    )(*gs, *ws)


@functools.partial(jax.custom_vjp, nondiff_argnums=(0,))
def _proj_op(names, a, ws):
    return tuple(_matmul(nm + '_fwd', a, w.astype(MXU_DTYPE), 'nn') for nm, w in zip(names, ws))


def _proj_op_fwd(names, a, ws):
    wbs = tuple(w.astype(MXU_DTYPE) for w in ws)
    return tuple(_matmul(nm + '_fwd', a, wb, 'nn') for nm, wb in zip(names, wbs)), (a, wbs)


def _proj_op_bwd(names, res, gs):
    a, wbs = res
    da = _matmul_nt_sum(names[0] + '_da_all', list(gs), list(wbs))
    return da, tuple(_matmul(nm + '_dw', a, g, 'tn') for nm, g in zip(names, gs))


_proj_op.defvjp(_proj_op_fwd, _proj_op_bwd)


def _rope128(x, c, a, b):
    return x * c + _roll(x, LANES - 8, 1) * a + _roll(x, 8, 1) * b


def _mla_prep_fn(i, rows, params):
    pm, c, a, b = rows
    qna, wqb, kvna, wkn, wv, qn, kn = params
    cq, ckv, kr_slot = pm[:, 0:256], pm[:, 256:384], pm[:, 384:512]
    q_all = _mm(_rms(cq) * qna, wqb, 'nn')
    ckvn = _rms(ckv) * kvna
    k_all = _mm(ckvn, wkn, 'nn')
    v_all = _mm(ckvn, wv, 'nn')
    qs, ks = [], []
    for h in range(MLA_HEADS):
        sl = slice(LANES * h, LANES * (h + 1))
        qs.append(_rope128(_rms(q_all[:, sl], MLA_QK) * qn, c, a, b) * MLA_Q_SCALE)
        ks.append(_rope128(_rms(k_all[:, sl] + kr_slot, MLA_QK) * kn, c, a, b))
    return jnp.concatenate(qs, axis=1), jnp.concatenate(ks, axis=1), v_all


def _attn_fwd(name, q, k, v, tb, ctx_len):
    t = q.shape[0]

    def body(q_ref, k_ref, v_ref, o_ref, lse_ref):
        qi = pl.program_id(1)

        def attend(k, v):
            s2 = _raw_mm(q_ref[...], k, 'nt')
            m2 = jnp.max(s2, axis=-1, keepdims=True)
            p = jnp.exp2(s2 - m2)
            l = jnp.sum(p, axis=-1, keepdims=True)
            o_ref[...] = _raw_mm(p, v, 'nn') / l
            lse_ref[...] = jnp.broadcast_to((m2 + jnp.log2(l)) * LN2, lse_ref.shape)

        @pl.when(qi == 0)
        def _():
            attend(k_ref[0:ctx_len, :], v_ref[0:ctx_len, :])

        @pl.when(qi != 0)
        def _():
            attend(k_ref[...], v_ref[...])

    blk = pl.BlockSpec((tb, LANES), lambda h, i: (i, h))
    whole = pl.BlockSpec((t, LANES), lambda h, i: (0, h))
    return pl.pallas_call(
        body, name=name, grid=(MLA_HEADS, t // tb), in_specs=[blk, whole, whole], out_specs=[blk, blk],
        out_shape=[jax.ShapeDtypeStruct(q.shape, F32)] * 2, compiler_params=_cparams(("arbitrary", "arbitrary")),
    )(q, k, v)


def _attn_bwd(name, q, k, v, o, lse, do, tb, ctx_len):
    t = q.shape[0]
    ck = _tile(t, 2816, 256)

    def body(q_ref, k_ref, v_ref, o_ref, lse_ref, do_ref, dq_ref, dk_ref, dv_ref):
        qi = pl.program_id(1)

        @pl.when(qi == 0)
        def _():
            dk_ref[...] = jnp.zeros_like(dk_ref)
            dv_ref[...] = jnp.zeros_like(dv_ref)

        q = q_ref[...]
        do = do_ref[...].astype(MXU_DTYPE)
        lse2 = lse_ref[...][:, 0:1] * LOG2E
        delta = jnp.sum(do_ref[...] * o_ref[...], axis=-1, keepdims=True)

        def part(rows):
            ks, vs = k_ref[rows, :], v_ref[rows, :]
            p = jnp.exp2(_raw_mm(q, ks, 'nt') - lse2)
            ds = p * ((_raw_mm(do, vs, 'nt') - delta) * LN2)
            dk_ref[rows, :] += _raw_mm(ds, q, 'tn')
            dv_ref[rows, :] += _raw_mm(p, do, 'tn')
            return _raw_mm(ds, ks, 'nn')

        @pl.when(qi == 0)
        def _():
            dq_ref[...] = part(pl.ds(0, ctx_len))

        @pl.when(qi != 0)
        def _():
            dq = part(pl.ds(0, ck))
            for c in range(1, t // ck):
                dq = dq + part(pl.ds(c * ck, ck))
            dq_ref[...] = dq

    blk = pl.BlockSpec((tb, LANES), lambda h, i: (i, h))
    whole = pl.BlockSpec((t, LANES), lambda h, i: (0, h))
    return pl.pallas_call(
        body, name=name, grid=(MLA_HEADS, t // tb), in_specs=[blk, whole, whole, blk, blk, blk],
        out_specs=[blk, whole, whole], out_shape=[jax.ShapeDtypeStruct(q.shape, F32)] * 3,
        compiler_params=_cparams(("arbitrary", "arbitrary")),
    )(q, k, v, o, lse, do)


def _mla_rows(pm, tabs):
    return [(pm, 0, pm.shape[1])] + [_full(x) for x in tabs]


@functools.partial(jax.custom_vjp, nondiff_argnums=(0, 1))
def _mla_branch(name, tb, pm, tabs, params):
    return _mla_branch_fwd(name, tb, pm, tabs, params)[0]


def _mla_branch_fwd(name, tb, pm, tabs, params):
    w = LANES * MLA_HEADS
    q, k, v = _rw_fwd(name + '_prep', _mla_prep_fn, _mla_rows(pm, tabs), list(params), [(w, MXU_DTYPE)] * 3, tb)
    o, lse = _attn_fwd(name + '_attn', q, k, v, tb, tb)
    return o, (pm, tabs, params, q, k, v, o, lse)


def _mla_branch_bwd(name, tb, res, do):
    pm, tabs, params, q, k, v, o, lse = res
    dq, dk, dv = _attn_bwd(name + '_attn_bwd', q, k, v, o, lse, do, tb, tb)
    (dpm,), dpars = _rw_bwd(name + '_prep_bwd', _mla_prep_fn, _mla_rows(pm, tabs), list(params), [dq, dk, dv], tb, (0,))
    return dpm, tuple(jnp.zeros_like(x) for x in tabs), tuple(dpars)


_mla_branch.defvjp(_mla_branch_fwd, _mla_branch_bwd)


def _chunk_masks(reverse):
    i = lax.broadcasted_iota(jnp.int32, (CHUNK, CHUNK), 0)
    j = lax.broadcasted_iota(jnp.int32, (CHUNK, CHUNK), 1)
    return i, j, ((j > i) if reverse else (j <= i))


def _gla_chunk(reverse, rows, params, st0):
    q, k, v, la = rows
    q = q * (GLA_DK ** -0.5)
    cum = _cumsum_rows(la, reverse)
    tot = cum[0:1] if reverse else cum[CHUNK - 1:CHUNK]
    st1 = st0 * jnp.exp(tot) + _mm(v, k * jnp.exp(tot - cum), 'tn')
    qd = q * jnp.exp(cum)
    _, _, mask = _chunk_masks(reverse)
    att = jnp.where(mask, _mm(qd, k * jnp.exp(-cum), 'nt'), 0.0)
    return _mm(att, v, 'nn') + _mm(qd, st0, 'nt'), st1


def _ret_chunk(reverse, rows, params, st0):
    q, k, v, cc, ss = rows
    (lg,) = params
    q = q * cc + _roll(q, RET_DK // 2, 1) * ss
    k = (k * cc + _roll(k, RET_DK // 2, 1) * ss) * (RET_DK ** -0.5)
    r = lax.broadcasted_iota(jnp.int32, (CHUNK, LANES), 0).astype(F32)
    zeta = jnp.exp((r if reverse else (CHUNK - 1.0 - r)) * lg)
    xi = jnp.exp(((CHUNK - r) if reverse else (r + 1.0)) * lg)
    st1 = st0 * jnp.exp(CHUNK * lg) + _mm(v, k * zeta, 'tn')
    i, j, mask = _chunk_masks(reverse)
    rel = jnp.where(mask, (j - i) if reverse else (i - j), 0).astype(F32)
    dmat = jnp.where(mask, jnp.exp(rel * lg[:, 0:CHUNK]), 0.0)
    att = _mm(q, k, 'nt') * dmat
    return _mm(att, v, 'nn') + _mm(q, st0, 'nt') * xi, st1


def _scan_order(reverse, nblk):
    if reverse:
        return lambda t: jnp.where(t == 0, 0, nblk - t)
    return lambda t: t


def _scan_specs(rows, params, tb, heads, blk_of):
    def rspec(spec):
        _, cb = spec
        if cb is None:
            return pl.BlockSpec((tb, LANES), lambda s: (blk_of(s), 0))
        return pl.BlockSpec((tb, heads * LANES), lambda s, cb=cb: (blk_of(s), cb // heads))

    return [rspec(s) for s in rows] + [pl.BlockSpec((heads, 1, LANES), lambda s: (0, 0, 0)) for _ in params]


def _head_rows(row_refs, rows, sl, h):
    lanes = pl.ds(h * LANES, LANES)
    return [r[sl, :] if spec[1] is None else r[sl, lanes] for r, spec in zip(row_refs, rows)]


def _scan_fwd(name, chunk_fn, reverse, rows, params, tb, heads):
    t = rows[0][0].shape[0]
    nblk, cpb = t // tb, tb // CHUNK
    blk_of = _scan_order(reverse, nblk)
    nr, npar = len(rows), len(params)
    order = list(range(cpb))[::-1] if reverse else list(range(cpb))

    def body(*refs):
        row_refs, par_refs = refs[:nr], refs[nr:nr + npar]
        o_ref, st_out_ref, st_ref = refs[nr + npar:]

        @pl.when(pl.program_id(0) == 0)
        def _():
            st_ref[...] = jnp.zeros_like(st_ref)

        for c in order:
            sl = pl.ds(c * CHUNK, CHUNK)
            for h in range(heads):
                st0 = st_ref[h]
                st_out_ref[h, c] = st0
                o, st1 = chunk_fn(reverse, _head_rows(row_refs, rows, sl, h), [p[h] for p in par_refs], st0)
                o_ref[sl, pl.ds(h * LANES, LANES)] = o
                st_ref[h] = st1

    return pl.pallas_call(
        body, name=name, grid=(nblk,), in_specs=_scan_specs(rows, params, tb, heads, blk_of),
        out_specs=[pl.BlockSpec((tb, heads * LANES), lambda s: (blk_of(s), 0)),
                   pl.BlockSpec((heads, cpb, LANES, LANES), lambda s: (0, blk_of(s), 0, 0))],
        out_shape=[jax.ShapeDtypeStruct((t, heads * LANES), F32),
                   jax.ShapeDtypeStruct((heads, t // CHUNK, LANES, LANES), F32)],
        scratch_shapes=[pltpu.VMEM((heads, LANES, LANES), F32)],
        compiler_params=_cparams(("arbitrary",)),
    )(*[s[0] for s in rows], *params)


def _scan_bwd(name, chunk_fn, reverse, rows, params, states, do, tb, heads, n_diff):
    t = rows[0][0].shape[0]
    nblk, cpb = t // tb, tb // CHUNK
    fwd_blk = _scan_order(reverse, nblk)
    blk_of = lambda s: fwd_blk(nblk - 1 - s)
    nr, npar = len(rows), len(params)
    order = list(range(cpb)) if reverse else list(range(cpb))[::-1]

    def body(*refs):
        row_refs, par_refs = refs[:nr], refs[nr:nr + npar]
        st_in_ref, do_ref = refs[nr + npar:nr + npar + 2]
        out_refs = refs[nr + npar + 2:-1]
        dst_ref = refs[-1]

        @pl.when(pl.program_id(0) == 0)
        def _():
            dst_ref[...] = jnp.zeros_like(dst_ref)
            for k in range(npar):
                out_refs[n_diff + k][...] = jnp.zeros_like(out_refs[n_diff + k])

        for c in order:
            sl = pl.ds(c * CHUNK, CHUNK)
            for h in range(heads):
                lanes = pl.ds(h * LANES, LANES)
                rv = _head_rows(row_refs, rows, sl, h)

                def f(dr, pvals, st0, rv=rv):
                    return chunk_fn(reverse, list(dr) + rv[n_diff:], pvals, st0)

                _, vjp = jax.vjp(f, rv[:n_diff], [p[h] for p in par_refs], st_in_ref[h, c])
                drows, dpars, dst0 = vjp((do_ref[sl, lanes], dst_ref[h]))
                for k in range(n_diff):
                    out_refs[k][sl, lanes] = drows[k]
                for k in range(npar):
                    out_refs[n_diff + k][h] += dpars[k]
                dst_ref[h] = dst0

    wide = pl.BlockSpec((tb, heads * LANES), lambda s: (blk_of(s), 0))
    pblk = pl.BlockSpec((heads, 1, LANES), lambda s: (0, 0, 0))
    res = pl.pallas_call(
        body, name=name, grid=(nblk,),
        in_specs=(_scan_specs(rows, params, tb, heads, blk_of)
                  + [pl.BlockSpec((heads, cpb, LANES, LANES), lambda s: (0, blk_of(s), 0, 0)), wide]),
        out_specs=[wide] * n_diff + [pblk for _ in params],
        out_shape=([jax.ShapeDtypeStruct((t, heads * LANES), F32)] * n_diff
                   + [jax.ShapeDtypeStruct(p.shape, F32) for p in params]),
        scratch_shapes=[pltpu.VMEM((heads, LANES, LANES), F32)],
        compiler_params=_cparams(("arbitrary",)),
    )(*[s[0] for s in rows], *params, states, do)
    return list(res[:n_diff]), list(res[n_diff:])


def _gla_la_fn(i, rows, params):
    (r,) = rows
    w2f, w2b, bgk = params
    la_f = jax.nn.log_sigmoid(_mm(r, w2f, 'nn') + bgk[0:1]) / GLA_GATE_NORMALIZER
    la_b = jax.nn.log_sigmoid(_mm(r, w2b, 'nn') + bgk[1:2]) / GLA_GATE_NORMALIZER
    return la_f, la_b


def _headnorm_fn(heads, with_weight):
    def fn(i, rows, params):
        o_f, o_b, g = rows
        outs = []
        for h in range(heads):
            sl = slice(LANES * h, LANES * (h + 1))
            y = _rms(o_f[:, sl] + o_b[:, sl])
            outs.append(y * params[0] if with_weight else y)
        return (jnp.concatenate(outs, axis=1) * jax.nn.silu(g),)

    return fn


@functools.partial(jax.custom_vjp, nondiff_argnums=(0, 1))
def _gla_branch(name, tb, pg, params):
    return _gla_branch_fwd(name, tb, pg, params)[0]


def _gla_rows(pg, la):
    return [(pg, 0), (pg, GLA_HEADS), (pg, 2 * GLA_HEADS), (la, 0)]


def _gla_branch_fwd(name, tb, pg, params):
    w2f, w2b, bgk, onorm = params
    w = GLA_HEADS * LANES
    la_f, la_b = _rw_fwd(name + '_la', _gla_la_fn, [(pg, 4 * w // LANES, LANES)], [w2f, w2b, bgk], [(w, F32)] * 2, tb)
    o_f, st_f = _scan_fwd(name + '_scan_f', _gla_chunk, False, _gla_rows(pg, la_f), [], tb, GLA_HEADS)
    o_b, st_b = _scan_fwd(name + '_scan_b', _gla_chunk, True, _gla_rows(pg, la_b), [], tb, GLA_HEADS)
    (y,) = _rw_fwd(name + '_norm', _headnorm_fn(GLA_HEADS, True), [_full(o_f), _full(o_b), (pg, 3, w)], [onorm],
                   [(w, F32)], tb)
    return y, (pg, params, la_f, la_b, o_f, o_b, st_f, st_b)


def _gla_branch_bwd(name, tb, res, dy):
    pg, params, la_f, la_b, o_f, o_b, st_f, st_b = res
    w2f, w2b, bgk, onorm = params
    w = GLA_HEADS * LANES
    (do_f, do_b, dg), (donorm,) = _rw_bwd(name + '_norm_bwd', _headnorm_fn(GLA_HEADS, True),
                                          [_full(o_f), _full(o_b), (pg, 3, w)], [onorm], [dy], tb, (0, 1, 2))
    (dq_f, dk_f, dv_f, dla_f), _ = _scan_bwd(name + '_scan_f_bwd', _gla_chunk, False, _gla_rows(pg, la_f), [], st_f,
                                             do_f, tb, GLA_HEADS, 4)
    (dq_b, dk_b, dv_b, dla_b), _ = _scan_bwd(name + '_scan_b_bwd', _gla_chunk, True, _gla_rows(pg, la_b), [], st_b,
                                             do_b, tb, GLA_HEADS, 4)
    (dr,), (dw2f, dw2b, dbgk) = _rw_bwd(name + '_la_bwd', _gla_la_fn, [(pg, 4 * w // LANES, LANES)], [w2f, w2b, bgk],
                                        [dla_f, dla_b], tb, (0,))
    dpg = jnp.concatenate([dq_f + dq_b, dk_f + dk_b, dv_f + dv_b, dg, dr], axis=1)
    return dpg, (dw2f, dw2b, dbgk, donorm)


_gla_branch.defvjp(_gla_branch_fwd, _gla_branch_bwd)


@functools.partial(jax.custom_vjp, nondiff_argnums=(0, 1))
def _ret_branch(name, tb, pr, tabs, lg):
    return _ret_branch_fwd(name, tb, pr, tabs, lg)[0]


def _ret_rows(pr, tabs):
    return [(pr, 0), (pr, RET_HEADS), (pr, 2 * RET_HEADS), (tabs[0], None), (tabs[1], None)]


def _ret_branch_fwd(name, tb, pr, tabs, lg):
    w = RET_HEADS * LANES
    o_f, st_f = _scan_fwd(name + '_scan_f', _ret_chunk, False, _ret_rows(pr, tabs), [lg[0]], tb, RET_HEADS)
    o_b, st_b = _scan_fwd(name + '_scan_b', _ret_chunk, True, _ret_rows(pr, tabs), [lg[1]], tb, RET_HEADS)
    (y,) = _rw_fwd(name + '_norm', _headnorm_fn(RET_HEADS, False), [_full(o_f), _full(o_b), (pr, 3, w)], [],
                   [(w, F32)], tb)
    return y, (pr, tabs, lg, o_f, o_b, st_f, st_b)


def _ret_branch_bwd(name, tb, res, dy):
    pr, tabs, lg, o_f, o_b, st_f, st_b = res
    w = RET_HEADS * LANES
    (do_f, do_b, dg), _ = _rw_bwd(name + '_norm_bwd', _headnorm_fn(RET_HEADS, False),
                                  [_full(o_f), _full(o_b), (pr, 3, w)], [], [dy], tb, (0, 1, 2))
    (dq_f, dk_f, dv_f), (dlg_f,) = _scan_bwd(name + '_scan_f_bwd', _ret_chunk, False, _ret_rows(pr, tabs), [lg[0]],
                                             st_f, do_f, tb, RET_HEADS, 3)
    (dq_b, dk_b, dv_b), (dlg_b,) = _scan_bwd(name + '_scan_b_bwd', _ret_chunk, True, _ret_rows(pr, tabs), [lg[1]],
                                             st_b, do_b, tb, RET_HEADS, 3)
    dpr = jnp.concatenate([dq_f + dq_b, dk_f + dk_b, dv_f + dv_b, dg], axis=1)
    return dpr, tuple(jnp.zeros_like(x) for x in tabs), jnp.stack([dlg_f, dlg_b])


_ret_branch.defvjp(_ret_branch_fwd, _ret_branch_bwd)


HALO = 8


def _halo_specs(tb, nblk, width, col_block):
    r = tb // HALO
    prev = pl.BlockSpec((HALO, width), lambda i: (jnp.maximum(i * r - 1, 0), col_block))
    nxt = pl.BlockSpec((HALO, width), lambda i: (jnp.minimum((i + 1) * r, nblk * r - 1), col_block))
    return prev, nxt


def _shifted(x, prev_blk, next_blk, i, nblk):
    tb = x.shape[0]
    row = lax.broadcasted_iota(jnp.int32, x.shape, 0)
    prev_row = jnp.where(i >= 2, prev_blk[HALO - 1:HALO], 0.0)
    next_row = jnp.where((i >= 1) & (i < nblk - 1), next_blk[0:1], 0.0)
    down = jnp.where(row == 0, prev_row, pltpu.roll(x, 1, 0))
    up = jnp.where(row == tb - 1, next_row, pltpu.roll(x, tb - 1, 0))
    return down, up


def _gelu_up(c, up):
    return jax.nn.gelu(c) * up


def _convact_fwd_call(name, gu, w_dw, b_dw, tb):
    t = gu.shape[0]
    nblk = t // tb
    prev_spec, next_spec = _halo_specs(tb, nblk, D_FF, 0)

    def body(g_ref, up_ref, prev_ref, next_ref, w_ref, b_ref, o_ref):
        i = pl.program_id(0)
        g = g_ref[...]
        down, upw = _shifted(g, prev_ref[...], next_ref[...], i, nblk)
        w = w_ref[...]
        c = w[0:1] * down + w[1:2] * g + w[2:3] * upw + b_ref[...]
        o_ref[...] = _gelu_up(c, up_ref[...])

    return pl.pallas_call(
        body, name=name, grid=(nblk,),
        in_specs=[pl.BlockSpec((tb, D_FF), lambda i: (i, 0)), pl.BlockSpec((tb, D_FF), lambda i: (i, 1)), prev_spec,
                  next_spec, _whole_spec(w_dw), _whole_spec(b_dw)],
        out_specs=pl.BlockSpec((tb, D_FF), lambda i: (i, 0)), out_shape=jax.ShapeDtypeStruct((t, D_FF), F32),
        compiler_params=_cparams(("arbitrary",)),
    )(gu, gu, gu, gu, w_dw, b_dw)


def _convact_bwd_calls(name, gu, w_dw, b_dw, dact, tb):
    t = gu.shape[0]
    nblk = t // tb
    prev_spec, next_spec = _halo_specs(tb, nblk, D_FF, 0)

    def body1(g_ref, up_ref, prev_ref, next_ref, w_ref, b_ref, da_ref, dc_ref, dup_ref, dw_ref, db_ref):
        i = pl.program_id(0)
        g = g_ref[...]
        down, upw = _shifted(g, prev_ref[...], next_ref[...], i, nblk)
        w = w_ref[...]
        c = w[0:1] * down + w[1:2] * g + w[2:3] * upw + b_ref[...]
        _, vjp = jax.vjp(_gelu_up, c, up_ref[...])
        dc, dup = vjp(da_ref[...])
        dc_ref[...] = dc
        dup_ref[...] = dup

        @pl.when(i == 0)
        def _():
            dw_ref[...] = jnp.zeros_like(dw_ref)
            db_ref[...] = jnp.zeros_like(db_ref)

        dw_ref[0:1, :] += jnp.sum(dc * down, axis=0, keepdims=True)
        dw_ref[1:2, :] += jnp.sum(dc * g, axis=0, keepdims=True)
        dw_ref[2:3, :] += jnp.sum(dc * upw, axis=0, keepdims=True)
        db_ref[...] += jnp.sum(dc, axis=0, keepdims=True)

    blk = pl.BlockSpec((tb, D_FF), lambda i: (i, 0))
    dc, dup, dw, db = pl.pallas_call(
        body1, name=name + '_a', grid=(nblk,),
        in_specs=[blk, pl.BlockSpec((tb, D_FF), lambda i: (i, 1)), prev_spec, next_spec, _whole_spec(w_dw),
                  _whole_spec(b_dw), blk],
        out_specs=[blk, blk, _whole_spec(w_dw), _whole_spec(b_dw)],
        out_shape=[jax.ShapeDtypeStruct((t, D_FF), F32)] * 2 + [jax.ShapeDtypeStruct(w_dw.shape, F32),
                                                                jax.ShapeDtypeStruct(b_dw.shape, F32)],
        compiler_params=_cparams(("arbitrary",)),
    )(gu, gu, gu, gu, w_dw, b_dw, dact)

    def body2(dc_ref, prev_ref, next_ref, dup_ref, w_ref, o_ref):
        i = pl.program_id(0)
        dc_blk = dc_ref[...]
        down, upw = _shifted(dc_blk, prev_ref[...], next_ref[...], i, nblk)
        w = w_ref[...]
        o_ref[:, 0:D_FF] = w[0:1] * upw + w[1:2] * dc_blk + w[2:3] * down
        o_ref[:, D_FF:2 * D_FF] = dup_ref[...]

    dgu = pl.pallas_call(
        body2, name=name + '_b', grid=(nblk,),
        in_specs=[blk, prev_spec, next_spec, blk, _whole_spec(w_dw)],
        out_specs=pl.BlockSpec((tb, 2 * D_FF), lambda i: (i, 0)), out_shape=jax.ShapeDtypeStruct((t, 2 * D_FF), F32),
        compiler_params=_cparams(("arbitrary",)),
    )(dc, dc, dc, dup, w_dw)
    return dgu, dw, db


@functools.partial(jax.custom_vjp, nondiff_argnums=(0, 1))
def _convact(name, tb, gu, w_dw, b_dw):
    return _convact_fwd_call(name + '_fwd', gu, w_dw, b_dw, tb)


def _convact_fwd(name, tb, gu, w_dw, b_dw):
    return _convact_fwd_call(name + '_fwd', gu, w_dw, b_dw, tb), (gu, w_dw, b_dw)


def _convact_bwd(name, tb, res, dact):
    gu, w_dw, b_dw = res
    return _convact_bwd_calls(name + '_bwd', gu, w_dw, b_dw, dact, tb)


_convact.defvjp(_convact_fwd, _convact_bwd)


def _loss_fwd_call(h, target, tb):
    nlat = target.shape[0] // tb

    def body(h_ref, t_ref, o_ref):
        @pl.when(pl.program_id(0) == 0)
        def _():
            o_ref[...] = jnp.zeros_like(o_ref)

        e = h_ref[...] - t_ref[...]
        o_ref[...] += jnp.sum(e * e, axis=0, keepdims=True)

    cols = pl.pallas_call(
        body, name='loss_fwd', grid=(nlat,),
        in_specs=[pl.BlockSpec((tb, D_MODEL), lambda i: (i + 1, 0)), pl.BlockSpec((tb, D_MODEL), lambda i: (i, 0))],
        out_specs=pl.BlockSpec((1, D_MODEL), lambda i: (0, 0)), out_shape=jax.ShapeDtypeStruct((1, D_MODEL), F32),
        compiler_params=_cparams(("arbitrary",)),
    )(h, target)
    return (0.5 / D_MODEL) * jnp.sum(cols)


def _loss_bwd_call(h, target, gbar, tb):
    def body(g_ref, h_ref, t_ref, o_ref):
        live = jnp.where(pl.program_id(0) == 0, 0.0, g_ref[...] * (1.0 / D_MODEL))
        o_ref[...] = live * (h_ref[...] - t_ref[...])

    return pl.pallas_call(
        body, name='loss_bwd', grid=(h.shape[0] // tb,),
        in_specs=[pl.BlockSpec((1, 1), lambda i: (0, 0)), pl.BlockSpec((tb, D_MODEL), lambda i: (i, 0)),
                  pl.BlockSpec((tb, D_MODEL), lambda i: (jnp.maximum(i - 1, 0), 0))],
        out_specs=pl.BlockSpec((tb, D_MODEL), lambda i: (i, 0)), out_shape=jax.ShapeDtypeStruct(h.shape, F32),
        compiler_params=_cparams(("arbitrary",)),
    )(gbar.reshape(1, 1), h, target)


@functools.partial(jax.custom_vjp, nondiff_argnums=(0,))
def _loss_op(tb, h, target):
    return _loss_fwd_call(h, target, tb)


def _loss_op_fwd(tb, h, target):
    return _loss_fwd_call(h, target, tb), (h, target)


def _loss_op_bwd(tb, res, gbar):
    h, target = res
    return _loss_bwd_call(h, target, gbar, tb), jnp.zeros_like(target)


_loss_op.defvjp(_loss_op_fwd, _loss_op_bwd)


def _rope_tables(pos, dim, theta):
    inv = theta ** (-jnp.arange(dim // 2, dtype=F32) * 2.0 / dim)
    ang = pos.astype(F32)[:, None] * inv[None, :]
    return jnp.cos(ang), jnp.sin(ang)


def _mla_tables(seq, ctx_len):
    rows = seq // GRID_W
    row_pos = jnp.repeat(jnp.arange(rows), GRID_W)
    col_pos = jnp.tile(jnp.arange(GRID_W), rows)
    cos_r, sin_r = _rope_tables(row_pos, MLA_ROPE // 2, ROPE_THETA)
    cos_c, sin_c = _rope_tables(col_pos, MLA_ROPE // 2, ROPE_THETA)
    one = jnp.ones((seq, MLA_NOPE), F32)
    z8 = jnp.zeros((seq, 8), F32)
    pad1 = jnp.ones((seq, LANES - MLA_QK), F32)
    pad0 = jnp.zeros((seq, LANES - MLA_QK), F32)
    z64 = jnp.zeros((seq, MLA_NOPE), F32)
    c = jnp.concatenate([one, cos_r, cos_r, cos_c, cos_c, pad1], axis=1)
    a = jnp.concatenate([z64, -sin_r, z8, -sin_c, z8, pad0], axis=1)
    b = jnp.concatenate([z64, z8, sin_r, z8, sin_c, pad0], axis=1)
    ctx_rows = lambda fill: jnp.full((ctx_len, LANES), fill, F32)
    return (jnp.concatenate([ctx_rows(1.0), c]), jnp.concatenate([ctx_rows(0.0), a]),
            jnp.concatenate([ctx_rows(0.0), b]))


def _ret_tables(total):
    inv = 1.0 / (RET_THETA ** jnp.linspace(0.0, 1.0, RET_DK // 2, dtype=F32))
    ang = jnp.arange(total).astype(F32)[:, None] * inv[None, :]
    cos, sin = jnp.cos(ang), jnp.sin(ang)
    return jnp.concatenate([cos, cos], axis=1), jnp.concatenate([-sin, sin], axis=1)


def _head_slots(w, heads, width):
    k = w.shape[0]
    return jnp.pad(w.reshape(k, heads, width), ((0, 0), (0, 0), (0, LANES - width))).reshape(k, heads * LANES)


def _pad_lanes(v, width):
    return jnp.pad(v, (0, LANES - width)).reshape(1, LANES)


IN_MLA_QKV = (0, 384)
IN_MLA_KR = (384, 416)
IN_GLA = (416, 2496)
IN_RET = (2496, 4544)
IN_GATES = (4544, 7616)
W_MLA_COLS = 512
W_GLA_COLS = 2176
BIG = ('w_ada', 'w_in', 'w_branch', 'w_out', 'w_ffn_in', 'w_ffn_out')
BIG_AXIS = dict(w_ada=1, w_in=1, w_branch=2, w_out=0, w_ffn_in=1, w_ffn_out=0)


def _big_layer_weights(full):
    w_in = full['w_in']
    zc = lambda n: jnp.zeros((D_MODEL, n), w_in.dtype)
    wbr = full['w_branch']
    wb_mla = jnp.pad(wbr[0].reshape(MLA_HEADS, MLA_V, D_MODEL), ((0, 0), (0, LANES - MLA_V), (0, 0)))
    return dict(
        ada=full['w_ada'],
        in_mla=jnp.concatenate([w_in[:, slice(*IN_MLA_QKV)], zc(MLA_NOPE), w_in[:, slice(*IN_MLA_KR)],
                                zc(LANES - MLA_QK)], axis=1),
        in_gla=jnp.concatenate([w_in[:, slice(*IN_GLA)], zc(W_GLA_COLS - (IN_GLA[1] - IN_GLA[0]))], axis=1),
        in_ret=w_in[:, slice(*IN_RET)], in_gate=w_in[:, slice(*IN_GATES)],
        br_mla=wb_mla.reshape(MLA_HEADS * LANES, D_MODEL), br_gla=wbr[1], br_ret=wbr[2],
        out=full['w_out'], ffn_in=full['w_ffn_in'], ffn_out=full['w_ffn_out'])


def _big_layer_grads(g):
    n_gla = IN_GLA[1] - IN_GLA[0]
    kr_at = IN_MLA_QKV[1] + MLA_NOPE
    w_in = jnp.concatenate([g['in_mla'][:, slice(*IN_MLA_QKV)], g['in_mla'][:, kr_at:kr_at + MLA_ROPE],
                            g['in_gla'][:, :n_gla], g['in_ret'], g['in_gate']], axis=1)
    br_mla = g['br_mla'].reshape(MLA_HEADS, LANES, D_MODEL)[:, :MLA_V].reshape(MLA_HEADS * MLA_V, D_MODEL)
    return dict(w_ada=g['ada'], w_in=w_in, w_branch=jnp.stack([br_mla, g['br_gla'], g['br_ret']]),
                w_out=g['out'], w_ffn_in=g['ffn_in'], w_ffn_out=g['ffn_out'])


def _row(v):
    return v.reshape(1, -1)


def _mixers(l, big, w, a, tabs_mla, tabs_ret, tb):
    nm = 'l%d_' % l
    row = _row
    pm, pg, pr, pgate = _proj_op(tuple(nm + s for s in ('in_mla', 'in_gla', 'in_ret', 'in_gate')), a,
                                 (big['in_mla'], big['in_gla'], big['in_ret'], big['in_gate']))

    kvb = w['mla_w_kvb'][l].reshape(-1, MLA_HEADS, MLA_NOPE + MLA_V)
    kdim = kvb.shape[0]
    mla_params = (row(w['mla_q_norm_a'][l]), _head_slots(w['mla_w_qb'][l], MLA_HEADS, MLA_QK),
                  row(w['mla_kv_norm_a'][l]), _head_slots(kvb[:, :, :MLA_NOPE].reshape(kdim, -1), MLA_HEADS, MLA_NOPE),
                  _head_slots(kvb[:, :, MLA_NOPE:].reshape(kdim, -1), MLA_HEADS, MLA_V),
                  _pad_lanes(w['mla_q_norm'][l], MLA_QK), _pad_lanes(w['mla_k_norm'][l], MLA_QK))
    y_mla = _mla_branch(nm + 'mla', tb, pm, tabs_mla, mla_params)

    gk2 = w['gla_w_gk2'][l]
    rank = gk2.shape[1]
    w2f = jnp.pad(gk2[0], ((0, LANES - rank), (0, 0)))
    w2b = jnp.pad(gk2[1], ((rank, LANES - 2 * rank), (0, 0)))
    y_gla = _gla_branch(nm + 'gla', tb, pg, (w2f, w2b, w['gla_b_gk'][l], row(w['gla_o_norm'][l])))

    log_g = -jnp.exp(w['ret_decay'][l])
    lg = jnp.broadcast_to(log_g[:, :, None, None], (2, RET_HEADS, 1, LANES))
    y_ret = _ret_branch(nm + 'ret', tb, pr, tabs_ret, lg)

    z0 = _mm_op(nm + 'br_mla', y_mla, big['br_mla'])
    z1 = _mm_op(nm + 'br_gla', y_gla, big['br_gla'])
    z2 = _mm_op(nm + 'br_ret', y_ret, big['br_ret'])
    u = _merge_op((nm + 'merge',), tb, z0, z1, z2, pgate, row(w['b_gate'][l]))[0]
    return _mm_op(nm + 'out', u, big['out'])


def _ffn(l, big, w, a2, tb):
    nm = 'l%d_' % l
    gu = _mm_op(nm + 'ffn_in', a2, big['ffn_in'])
    act = _convact(nm + 'convact', tb, gu, w['w_dw'][l], _row(w['b_dw'][l]))
    return _mm_op(nm + 'ffn_out', act, big['ffn_out'])


def _local_loss(big, w, x, c, ctx, target):
    seq, tb = x.shape[1], ctx.shape[1]
    h = jnp.concatenate([ctx[0], x[0]], axis=0)
    cond_in = jnp.concatenate([w['c_ctx'].reshape(1, -1), c, jnp.zeros((14, D_MODEL), F32)], axis=0)
    cond16 = _silu_op(('cond_silu',), 16, cond_in)[0]
    tabs_mla = _mla_tables(seq, tb)
    tabs_ret = _ret_tables(seq + tb)
    mods = [_mm_op('l%d_ada' % l, cond16, big[l]['ada']) for l in range(DEPTH)]
    b_ada = [_row(w['b_ada'][l]) for l in range(DEPTH)]
    a = _normmod_op(('l0_norm1', 0, 1), tb, h, _row(w['norm1_w'][0]), mods[0], b_ada[0])[0]
    for l in range(DEPTH):
        nm = 'l%d_' % l
        y = _mixers(l, big[l], w, a, tabs_mla, tabs_ret, tb)
        h, a2 = _resid_norm_op((nm + 'res1_norm2', 2, 3, 4), tb, h, y, mods[l], b_ada[l], _row(w['norm2_w'][l]),
                               mods[l], b_ada[l])
        f = _ffn(l, big[l], w, a2, tb)
        if l + 1 < DEPTH:
            h, a = _resid_norm_op((nm + 'res2_norm1', 5, 0, 1), tb, h, f, mods[l], b_ada[l],
                                  _row(w['norm1_w'][l + 1]), mods[l + 1], b_ada[l + 1])
        else:
            h = _resid_op((nm + 'res2', 5), tb, h, f, mods[l], b_ada[l])[0]
    return _loss_op(tb, h, target[0])


ANY = pl.BlockSpec(memory_space=pl.ANY)
FLAT_W = 1024


def _my_place():
    return lax.axis_index('x'), lax.axis_index('y'), lax.axis_index('c')


def _other_chips(x, y):
    return [(1 - x, y), (x, 1 - y), (1 - x, 1 - y)]


def _gather_chips(name, arrs):
    n = len(arrs)

    def body(*refs):
        ins, outs = refs[:n], refs[n:2 * n]
        send_sems, recv_sems, pass_send_sems, pass_recv_sems, own_send_sems, own_recv_sems = refs[2 * n:]
        x, y, c = _my_place()
        me = 2 * x + y
        chips = _other_chips(x, y)

        def half(a, which):
            h = arrs[a].shape[0] // 2
            return pl.ds(which * h, h)

        def ici(j, a, chip_slot, to):
            src = ins[a].at[half(a, c)] if chip_slot is None else outs[a].at[chip_slot, half(a, c)]
            return pltpu.make_async_remote_copy(
                src_ref=src, dst_ref=outs[a].at[me if chip_slot is None else chip_slot, half(a, c)],
                send_sem=send_sems.at[j, a], recv_sem=recv_sems.at[j, a], device_id=to, device_id_type=MESH)

        def passed(j, a, chip_slot, which):
            rows = outs[a].at[chip_slot, half(a, which)]
            return pltpu.make_async_remote_copy(src_ref=rows, dst_ref=rows, send_sem=pass_send_sems.at[j, a],
                                                recv_sem=pass_recv_sems.at[j, a], device_id=(x, y, 1 - c),
                                                device_id_type=MESH)

        def own(a):
            return pltpu.make_async_remote_copy(src_ref=ins[a], dst_ref=outs[a].at[me], send_sem=own_send_sems.at[a],
                                                recv_sem=own_recv_sems.at[a], device_id=(x, y, 1 - c),
                                                device_id_type=MESH)

        copies = [own(a) for a in range(n)]
        sends = [ici(j, a, None, (px, py, c)) for j, (px, py) in enumerate(chips) for a in range(n)]
        for cp in copies + sends:
            cp.start()
        passes = []
        for j, (px, py) in enumerate(chips):
            for a in range(n):
                ici(j, a, 2 * px + py, (px, py, c)).wait_recv()
                p = passed(j, a, 2 * px + py, c)
                p.start()
                passes.append(p)
        for j, (px, py) in enumerate(chips):
            for a in range(n):
                passed(j, a, 2 * px + py, 1 - c).wait_recv()
        for s in sends + passes:
            s.wait_send()
        for cp in copies:
            cp.wait()

    sems = pltpu.SemaphoreType.DMA((3, n))
    return pl.pallas_call(
        body, name=name, in_specs=[ANY] * n, out_specs=[ANY] * n,
        out_shape=[jax.ShapeDtypeStruct((N_CHIPS,) + a.shape, a.dtype) for a in arrs],
        scratch_shapes=[sems, sems, sems, sems, pltpu.SemaphoreType.DMA((n,)), pltpu.SemaphoreType.DMA((n,))],
    )(*arrs)


def _swap_halves(name, arrs):
    n = len(arrs)

    def body(*refs):
        ins, outs = refs[:n], refs[n:2 * n]
        send_sems, recv_sems = refs[2 * n:]
        x, y, c = _my_place()
        copies = []
        for a in range(n):
            half = arrs[a].shape[1] // 2
            cp = pltpu.make_async_remote_copy(src_ref=ins[a].at[:, pl.ds((1 - c) * half, half)], dst_ref=outs[a],
                                              send_sem=send_sems.at[a], recv_sem=recv_sems.at[a],
                                              device_id=(x, y, 1 - c), device_id_type=MESH)
            cp.start()
            copies.append(cp)
        for cp in copies:
            cp.wait()

    return pl.pallas_call(
        body, name=name, in_specs=[ANY] * n, out_specs=[ANY] * n,
        out_shape=[jax.ShapeDtypeStruct((a.shape[0], a.shape[1] // 2, a.shape[2]), a.dtype) for a in arrs],
        scratch_shapes=[pltpu.SemaphoreType.DMA((n,)), pltpu.SemaphoreType.DMA((n,))],
    )(*arrs)


def _sibling_exchange(name, arrs):
    n = len(arrs)

    def body(*refs):
        ins, outs = refs[:n], refs[n:2 * n]
        send_sems, recv_sems = refs[2 * n:]
        x, y, c = _my_place()
        copies = []
        for a in range(n):
            cp = pltpu.make_async_remote_copy(src_ref=ins[a], dst_ref=outs[a], send_sem=send_sems.at[a],
                                              recv_sem=recv_sems.at[a], device_id=(x, y, 1 - c), device_id_type=MESH)
            cp.start()
            copies.append(cp)
        for cp in copies:
            cp.wait()

    return pl.pallas_call(
        body, name=name, in_specs=[ANY] * n, out_specs=[ANY] * n,
        out_shape=[jax.ShapeDtypeStruct(a.shape, a.dtype) for a in arrs],
        scratch_shapes=[pltpu.SemaphoreType.DMA((n,)), pltpu.SemaphoreType.DMA((n,))],
    )(*arrs)


def _chip_all_to_all(name, arrs):
    n = len(arrs)

    def body(*refs):
        ins, outs = refs[:n], refs[n:2 * n]
        send_sems, recv_sems = refs[2 * n:]
        x, y, c = _my_place()
        sends = []
        for j, (px, py) in enumerate(_other_chips(x, y)):
            for a in range(n):
                s = pltpu.make_async_remote_copy(src_ref=ins[a].at[2 * px + py], dst_ref=outs[a].at[j],
                                                 send_sem=send_sems.at[j, a], recv_sem=recv_sems.at[j, a],
                                                 device_id=(px, py, c), device_id_type=MESH)
                s.start()
                sends.append(s)
        for s in sends:
            s.wait()

    return pl.pallas_call(
        body, name=name, in_specs=[ANY] * n, out_specs=[ANY] * n,
        out_shape=[jax.ShapeDtypeStruct((3,) + a.shape[1:], a.dtype) for a in arrs],
        scratch_shapes=[pltpu.SemaphoreType.DMA((3, n)), pltpu.SemaphoreType.DMA((3, n))],
    )(*arrs)


def _gather_all(name, arr):
    def body(in_ref, out_ref, send_sems, recv_sems, local_sem):
        x, y, c = _my_place()
        me = 4 * x + 2 * y + c
        mine = pltpu.make_async_copy(in_ref, out_ref.at[me], local_sem)
        mine.start()
        peers = []
        for k in range(1, N_DEV):
            px = (1 - x) if k & 4 else x
            py = (1 - y) if k & 2 else y
            pc = (1 - c) if k & 1 else c
            peers.append((px, py, pc))
        sends = []
        for k, peer in enumerate(peers):
            s = pltpu.make_async_remote_copy(src_ref=in_ref, dst_ref=out_ref.at[me], send_sem=send_sems.at[k],
                                             recv_sem=recv_sems.at[k], device_id=peer, device_id_type=MESH)
            s.start()
            sends.append(s)
        for k, (px, py, pc) in enumerate(peers):
            pltpu.make_async_remote_copy(src_ref=in_ref, dst_ref=out_ref.at[4 * px + 2 * py + pc],
                                         send_sem=send_sems.at[k], recv_sem=recv_sems.at[k], device_id=(px, py, pc),
                                         device_id_type=MESH).wait_recv()
        for s in sends:
            s.wait_send()
        mine.wait()

    return pl.pallas_call(
        body, name=name, in_specs=[ANY], out_specs=ANY, out_shape=jax.ShapeDtypeStruct((N_DEV,) + arr.shape, arr.dtype),
        scratch_shapes=[pltpu.SemaphoreType.DMA((N_DEV - 1,)), pltpu.SemaphoreType.DMA((N_DEV - 1,)),
                        pltpu.SemaphoreType.DMA],
    )(arr)


def _flat_rows(r):
    return _tile(r, 512, 16)


def _add_my_half(name, whole, other, my_c, out_dtype):
    n, half, wd = other.shape
    tr = _flat_rows(half)
    nb = half // tr

    def body(c_ref, a_ref, b_ref, o_ref):
        o_ref[...] = (a_ref[...] + b_ref[...]).astype(o_ref.dtype)

    spec = pl.BlockSpec((1, tr, wd), lambda s, i, c: (s, i, 0))
    grid_spec = pltpu.PrefetchScalarGridSpec(
        num_scalar_prefetch=1, grid=(n, nb),
        in_specs=[pl.BlockSpec((1, tr, wd), lambda s, i, c: (s, c[0] * nb + i, 0)), spec], out_specs=spec)
    return pl.pallas_call(body, name=name, grid_spec=grid_spec, out_shape=jax.ShapeDtypeStruct(other.shape, out_dtype),
                          compiler_params=_cparams(("arbitrary",) * 2))(my_c.astype(jnp.int32).reshape(1), whole, other)


def _sum_rows(name, a):
    n, r, wd = a.shape
    tr = _flat_rows(r)

    def body(a_ref, o_ref):
        acc = a_ref[0].astype(F32)
        for k in range(1, n):
            acc = acc + a_ref[k].astype(F32)
        o_ref[...] = acc

    return pl.pallas_call(body, name=name, grid=(r // tr,), in_specs=[pl.BlockSpec((n, tr, wd), lambda i: (0, i, 0))],
                          out_specs=pl.BlockSpec((tr, wd), lambda i: (i, 0)), out_shape=jax.ShapeDtypeStruct((r, wd), F32),
                          compiler_params=_cparams(("arbitrary",)))(a)


def _sum_own_received(name, sums, received, me):
    _, r, wd = sums.shape
    tr = _flat_rows(r)

    def body(me_ref, own_ref, rec_ref, o_ref):
        acc = own_ref[0].astype(F32)
        for k in range(3):
            acc = acc + rec_ref[k].astype(F32)
        o_ref[...] = acc

    grid_spec = pltpu.PrefetchScalarGridSpec(
        num_scalar_prefetch=1, grid=(r // tr,),
        in_specs=[pl.BlockSpec((1, tr, wd), lambda i, me: (me[0], i, 0)), pl.BlockSpec((3, tr, wd), lambda i, me: (0, i, 0))],
        out_specs=pl.BlockSpec((tr, wd), lambda i, me: (i, 0)))
    return pl.pallas_call(body, name=name, grid_spec=grid_spec, out_shape=jax.ShapeDtypeStruct((r, wd), F32),
                          compiler_params=_cparams(("arbitrary",)))(me.astype(jnp.int32).reshape(1), sums, received)


def _adamw_math(w, g, m, v):
    m = ADAM_B1 * m + (1.0 - ADAM_B1) * g
    v = ADAM_B2 * v + (1.0 - ADAM_B2) * (g * g)
    m_hat = m / (1.0 - ADAM_B1 ** ADAM_STEP)
    v_hat = v / (1.0 - ADAM_B2 ** ADAM_STEP)
    return -ADAM_LR * (m_hat / (jnp.sqrt(v_hat) + ADAM_EPS) + ADAM_WD * w), m, v


def _adamw(name, w, g, m, v):
    r, wd = w.shape
    tr = _tile(r, 256, 8)

    def body(w_ref, g_ref, m_ref, v_ref, d_ref, nm_ref, nv_ref):
        d_ref[...], nm_ref[...], nv_ref[...] = _adamw_math(w_ref[...], g_ref[...], m_ref[...], v_ref[...])

    spec = pl.BlockSpec((tr, wd), lambda i: (i, 0))
    return pl.pallas_call(body, name=name, grid=(r // tr,), in_specs=[spec] * 4, out_specs=[spec] * 3,
                          out_shape=[jax.ShapeDtypeStruct((r, wd), F32)] * 3,
                          compiler_params=_cparams(("arbitrary",)))(w, g, m, v)


def _adamw_halves(name, w, g_mine, g_other, m, v, my_c):
    r, wd = w.shape
    half = r // 2
    tr = _tile(half, 256, 8)
    nb = half // tr

    def body(c_ref, w_ref, gm_ref, go_ref, m_ref, v_ref, g_ref, d_ref, nm_ref, nv_ref):
        g = jnp.where(pl.program_id(0) // nb == c_ref[0], gm_ref[...], go_ref[...])
        g_ref[...] = g
        d_ref[...], nm_ref[...], nv_ref[...] = _adamw_math(w_ref[...], g, m_ref[...], v_ref[...])

    spec = pl.BlockSpec((tr, wd), lambda i, c: (i, 0))
    hspec = pl.BlockSpec((tr, wd), lambda i, c: (i % nb, 0))
    grid_spec = pltpu.PrefetchScalarGridSpec(num_scalar_prefetch=1, grid=(r // tr,),
                                             in_specs=[spec, hspec, hspec, spec, spec], out_specs=[spec] * 4)
    return pl.pallas_call(body, name=name, grid_spec=grid_spec, out_shape=[jax.ShapeDtypeStruct((r, wd), F32)] * 4,
                          compiler_params=_cparams(("arbitrary",)))(my_c.astype(jnp.int32).reshape(1), w, g_mine,
                                                                    g_other, m, v)


def _to_flat(parts, dtype, row_multiple):
    flat = jnp.concatenate([p.astype(dtype).reshape(-1) for p in parts])
    unit = FLAT_W * row_multiple
    total = -(-flat.shape[0] // unit) * unit
    return jnp.pad(flat, (0, total - flat.shape[0])).reshape(total // FLAT_W, FLAT_W)


def _from_flat(flat, shapes):
    flat = flat.reshape(-1)
    out, at = [], 0
    for shp in shapes:
        n = 1
        for d in shp:
            n *= d
        out.append(flat[at:at + n].reshape(shp))
        at += n
    return out


def _shard_piece(a, axis, s):
    n = a.shape[axis] // N_CHIPS
    return lax.slice_in_dim(a, s * n, (s + 1) * n, axis=axis)


def kernel(x, c, ctx, c_ctx, w_ada, b_ada, norm1_w, norm2_w, w_in, b_gate, mla_q_norm_a, mla_w_qb, mla_kv_norm_a, mla_w_kvb, mla_q_norm, mla_k_norm, gla_w_gk2, gla_b_gk, gla_o_norm, ret_decay, w_branch, w_out, w_ffn_in, w_dw, b_dw, w_ffn_out, loss_target, m_c_ctx, m_w_ada, m_b_ada, m_norm1_w, m_norm2_w, m_w_in, m_b_gate, m_mla_q_norm_a, m_mla_w_qb, m_mla_kv_norm_a, m_mla_w_kvb, m_mla_q_norm, m_mla_k_norm, m_gla_w_gk2, m_gla_b_gk, m_gla_o_norm, m_ret_decay, m_w_branch, m_w_out, m_w_ffn_in, m_w_dw, m_b_dw, m_w_ffn_out, v_c_ctx, v_w_ada, v_b_ada, v_norm1_w, v_norm2_w, v_w_in, v_b_gate, v_mla_q_norm_a, v_mla_w_qb, v_mla_kv_norm_a, v_mla_w_kvb, v_mla_q_norm, v_mla_k_norm, v_gla_w_gk2, v_gla_b_gk, v_gla_o_norm, v_ret_decay, v_w_branch, v_w_out, v_w_ffn_in, v_w_dw, v_b_dw, v_w_ffn_out):
    local = dict(c_ctx=c_ctx, w_ada=w_ada, b_ada=b_ada, norm1_w=norm1_w, norm2_w=norm2_w, w_in=w_in, b_gate=b_gate,
                 mla_q_norm_a=mla_q_norm_a, mla_w_qb=mla_w_qb, mla_kv_norm_a=mla_kv_norm_a, mla_w_kvb=mla_w_kvb,
                 mla_q_norm=mla_q_norm, mla_k_norm=mla_k_norm, gla_w_gk2=gla_w_gk2, gla_b_gk=gla_b_gk,
                 gla_o_norm=gla_o_norm, ret_decay=ret_decay, w_branch=w_branch, w_out=w_out, w_ffn_in=w_ffn_in,
                 w_dw=w_dw, b_dw=b_dw, w_ffn_out=w_ffn_out)
    mom_m = dict(c_ctx=m_c_ctx, w_ada=m_w_ada, b_ada=m_b_ada, norm1_w=m_norm1_w, norm2_w=m_norm2_w, w_in=m_w_in,
                 b_gate=m_b_gate, mla_q_norm_a=m_mla_q_norm_a, mla_w_qb=m_mla_w_qb, mla_kv_norm_a=m_mla_kv_norm_a,
                 mla_w_kvb=m_mla_w_kvb, mla_q_norm=m_mla_q_norm, mla_k_norm=m_mla_k_norm, gla_w_gk2=m_gla_w_gk2,
                 gla_b_gk=m_gla_b_gk, gla_o_norm=m_gla_o_norm, ret_decay=m_ret_decay, w_branch=m_w_branch,
                 w_out=m_w_out, w_ffn_in=m_w_ffn_in, w_dw=m_w_dw, b_dw=m_b_dw, w_ffn_out=m_w_ffn_out)
    mom_v = dict(c_ctx=v_c_ctx, w_ada=v_w_ada, b_ada=v_b_ada, norm1_w=v_norm1_w, norm2_w=v_norm2_w, w_in=v_w_in,
                 b_gate=v_b_gate, mla_q_norm_a=v_mla_q_norm_a, mla_w_qb=v_mla_w_qb, mla_kv_norm_a=v_mla_kv_norm_a,
                 mla_w_kvb=v_mla_w_kvb, mla_q_norm=v_mla_q_norm, mla_k_norm=v_mla_k_norm, gla_w_gk2=v_gla_w_gk2,
                 gla_b_gk=v_gla_b_gk, gla_o_norm=v_gla_o_norm, ret_decay=v_ret_decay, w_branch=v_w_branch,
                 w_out=v_w_out, w_ffn_in=v_w_ffn_in, w_dw=v_w_dw, b_dw=v_b_dw, w_ffn_out=v_w_ffn_out)
    axis_of = dict(SHARDED)
    small_narrow = tuple(n for n, _ in SHARDED if n not in BIG and n not in SHARDED_F32)
    small_sharded = small_narrow + SHARDED_F32
    kinds = ('grad', 'delta', 'new_m', 'new_v')
    my_x, my_y, my_c = _my_place()
    my_chip = 2 * my_x + my_y

    gathered = _gather_chips('gather_weights', [local[n].astype(MXU_DTYPE) for n in BIG] + [
        _to_flat([local[n] for n in small_narrow], MXU_DTYPE, 32), _to_flat([local[n] for n in SHARDED_F32], F32, 16)])
    big = []
    for l in range(DEPTH):
        full_l = {n: jnp.concatenate([g[s, l] for s in range(N_CHIPS)], axis=BIG_AXIS[n]) for n, g in zip(BIG, gathered)}
        big.append({k: v.astype(F32) for k, v in _big_layer_weights(full_l).items()})
    small = {n: local[n] for n in REPLICATED}
    for names, flat4 in ((small_narrow, gathered[-2]), (SHARDED_F32, gathered[-1])):
        pieces = [_from_flat(flat4[s], [local[n].shape for n in names]) for s in range(N_CHIPS)]
        for k, n in enumerate(names):
            small[n] = jnp.concatenate([pieces[s][k] for s in range(N_CHIPS)], axis=axis_of[n]).astype(F32)

    loss_local, (grad_big, grad_small, grad_x) = jax.value_and_grad(_local_loss, argnums=(0, 1, 2))(
        big, small, x, c, ctx, loss_target)
    loss = lax.psum(loss_local, ('x', 'y', 'c'))

    per_layer = [_big_layer_grads(grad_big[l]) for l in range(DEPTH)]
    stacks = []
    for n in BIG:
        st = jnp.stack([jnp.stack([_shard_piece(per_layer[l][n], BIG_AXIS[n], s) for l in range(DEPTH)])
                        for s in range(N_CHIPS)])
        stacks.append(st.reshape(N_CHIPS, -1, st.shape[-1]))
    stacks.append(jnp.stack([_to_flat([_shard_piece(grad_small[n], axis_of[n], s) for n in small_sharded], F32, 64)
                             for s in range(N_CHIPS)]))
    labels = BIG + ('small',)
    swapped = _swap_halves('grad_swap_halves', stacks)
    chip_sums = [_add_my_half('grad_add_' + n, st, sw, my_c, MXU_DTYPE) for n, st, sw in zip(labels, stacks, swapped)]
    received = _chip_all_to_all('grad_all_to_all', chip_sums)
    mine = [_sum_own_received('grad_sum_' + n, cs, rc, my_chip) for n, cs, rc in zip(labels, chip_sums, received)]
    other = _sibling_exchange('grad_share_result', mine)

    g_rep = _sum_rows('grad_sum_replicated', _gather_all('grad_gather_replicated',
                                                          _to_flat([grad_small[n] for n in REPLICATED], F32, 8)))

    results = {}
    as_rows = lambda t: t.reshape(-1, t.shape[-1])
    for k, n in enumerate(BIG):
        res = _adamw_halves('adamw_' + n, as_rows(local[n]), mine[k], other[k], as_rows(mom_m[n]), as_rows(mom_v[n]), my_c)
        for kind, val in zip(kinds, res):
            results[kind, n] = val.reshape(local[n].shape)
    flat_small = lambda d: _to_flat([d[n] for n in small_sharded], F32, 64)
    res = _adamw_halves('adamw_small', flat_small(local), mine[-1], other[-1], flat_small(mom_m), flat_small(mom_v), my_c)
    for kind, flat in zip(kinds, res):
        for n, val in zip(small_sharded, _from_flat(flat, [local[n].shape for n in small_sharded])):
            results[kind, n] = val
    flat_rep = lambda d: _to_flat([d[n] for n in REPLICATED], F32, 8)
    upd_r = _adamw('adamw_replicated', flat_rep(local), g_rep, flat_rep(mom_m), flat_rep(mom_v))
    for kind, flat in zip(kinds, (g_rep,) + tuple(upd_r)):
        for n, val in zip(REPLICATED, _from_flat(flat, [local[n].shape for n in REPLICATED])):
            results[kind, n] = val
    out = [loss, grad_x]
    for kind in ('grad', 'delta', 'new_m', 'new_v'):
        out += [results[kind, n] for n in WEIGHT_ORDER]
    return tuple(out)
```

```python
import functools

import jax
import jax.numpy as jnp
from jax import lax
from jax.experimental import pallas as pl
from jax.experimental.pallas import tpu as pltpu

F32 = jnp.float32
MXU_DTYPE = jnp.bfloat16

DEPTH = 2
D_MODEL = 1024
GRID_W = 64
CHUNK = 64
LANES = 128
MLA_HEADS = 8
MLA_NOPE = 64
MLA_ROPE = 32
MLA_QK = MLA_NOPE + MLA_ROPE
MLA_V = 64
GLA_HEADS = 4
GLA_DK = 128
GLA_GATE_NORMALIZER = 16.0
RET_HEADS = 4
RET_DK = 128
D_FF = 2816
ROPE_THETA = 10000.0
RET_THETA = 10000.0
EPS = 1e-6
ADAM_LR = 0.001
ADAM_B1 = 0.9
ADAM_B2 = 0.999
ADAM_EPS = 1e-08
ADAM_WD = 0.01
ADAM_STEP = 10
NEG_BIG = -1e30

VMEM_LIMIT_BYTES = 56 * 1024 * 1024
WEIGHT_BLOCK_BYTES = 8 * 1024 * 1024
ACC_BLOCK_BYTES = 13 * 1024 * 1024
ACC_MAX_ROWS = 2816
MM_ROWS = 512
LOG2E = 1.4426950408889634
LN2 = 0.6931471805599453
MLA_Q_SCALE = MLA_QK ** -0.5 * LOG2E

SHARDED = (('w_ada', 2), ('w_in', 2), ('b_gate', 2), ('mla_w_qb', 2), ('mla_w_kvb', 2), ('gla_w_gk2', 3),
           ('gla_b_gk', 2), ('w_branch', 3), ('w_out', 1), ('w_ffn_in', 2), ('w_dw', 2), ('w_ffn_out', 1))
SHARDED_F32 = ('b_gate', 'gla_b_gk', 'w_dw')
REPLICATED = ('c_ctx', 'b_ada', 'norm1_w', 'norm2_w', 'mla_q_norm_a', 'mla_kv_norm_a', 'mla_q_norm', 'mla_k_norm',
              'gla_o_norm', 'ret_decay', 'b_dw')
WEIGHT_ORDER = ('c_ctx', 'w_ada', 'b_ada', 'norm1_w', 'norm2_w', 'w_in', 'b_gate', 'mla_q_norm_a', 'mla_w_qb',
                'mla_kv_norm_a', 'mla_w_kvb', 'mla_q_norm', 'mla_k_norm', 'gla_w_gk2', 'gla_b_gk', 'gla_o_norm',
                'ret_decay', 'w_branch', 'w_out', 'w_ffn_in', 'w_dw', 'b_dw', 'w_ffn_out')
N_CHIPS = 4
N_DEV = 8
MESH = pl.DeviceIdType.MESH


def _cparams(sem):
    return pltpu.CompilerParams(dimension_semantics=sem, vmem_limit_bytes=VMEM_LIMIT_BYTES)


def _tile(n, target, unit):
    best = None
    for t in range(unit, min(n, target) + 1, unit):
        if n % t == 0:
            best = t
    return n if best is None else best


_DN = {'nn': (((1,), (0,)), ((), ())), 'nt': (((1,), (1,)), ((), ())), 'tn': (((0,), (0,)), ((), ()))}


def _raw_mm(x, y, form):
    return lax.dot_general(x.astype(MXU_DTYPE), y.astype(MXU_DTYPE), _DN[form], preferred_element_type=F32)


@functools.partial(jax.custom_vjp, nondiff_argnums=(2,))
def _mm(x, y, form):
    return _raw_mm(x, y, form)


def _mm_fwd(x, y, form):
    return _raw_mm(x, y, form), (x, y)


def _mm_bwd(form, res, g):
    x, y = res
    if form == 'nn':
        dx, dy = _mm(g, y, 'nt'), _mm(x, g, 'tn')
    elif form == 'nt':
        dx, dy = _mm(g, y, 'nn'), _mm(g, x, 'tn')
    else:
        dx, dy = _mm(y, g, 'nt'), _mm(x, g, 'nn')
    return dx.astype(x.dtype), dy.astype(y.dtype)


_mm.defvjp(_mm_fwd, _mm_bwd)


@functools.partial(jax.custom_vjp, nondiff_argnums=(1, 2))
def _roll(x, shift, axis):
    return pltpu.roll(x, shift, axis)


def _roll_fwd(x, shift, axis):
    return pltpu.roll(x, shift, axis), None


def _roll_bwd(shift, axis, _, g):
    return (pltpu.roll(g, (g.shape[axis] - shift) % g.shape[axis], axis),)


_roll.defvjp(_roll_fwd, _roll_bwd)


def _running_sum(x, reverse):
    n = x.shape[0]
    row = lax.broadcasted_iota(jnp.int32, x.shape, 0)
    d = 1
    while d < n:
        if reverse:
            x = x + jnp.where(row < n - d, pltpu.roll(x, n - d, 0), 0.0)
        else:
            x = x + jnp.where(row >= d, pltpu.roll(x, d, 0), 0.0)
        d *= 2
    return x


@functools.partial(jax.custom_vjp, nondiff_argnums=(1,))
def _cumsum_rows(x, reverse):
    return _running_sum(x, reverse)


def _cumsum_fwd(x, reverse):
    return _running_sum(x, reverse), None


def _cumsum_bwd(reverse, _, g):
    return (_running_sum(g, not reverse),)


_cumsum_rows.defvjp(_cumsum_fwd, _cumsum_bwd)


def _rms(x, n=None):
    n = x.shape[-1] if n is None else n
    return x * lax.rsqrt(jnp.sum(x * x, axis=-1, keepdims=True) / n + EPS)


def _mod_row(i, mod16, b_ada):
    m = mod16[0:8] + b_ada
    return jnp.where(i == 0, m[0:1], m[1:2])


def _row_spec(tb, spec):
    arr, cb, width = spec
    return pl.BlockSpec((tb, width), lambda i, cb=cb: (i, cb))


def _whole_spec(arr):
    nd = arr.ndim
    return pl.BlockSpec(arr.shape, lambda i, nd=nd: (0,) * nd)


def _rw_fwd(name, fn, rows, params, outs, tb):
    t = rows[0][0].shape[0]
    nr, npar = len(rows), len(params)

    def body(*refs):
        i = pl.program_id(0)
        rv = [r[...] for r in refs[:nr]]
        pv = [p[...] for p in refs[nr:nr + npar]]
        res = fn(i, rv, pv)
        for o_ref, val in zip(refs[nr + npar:], res):
            o_ref[...] = val.astype(o_ref.dtype)

    return pl.pallas_call(
        body, name=name, grid=(t // tb,),
        in_specs=[_row_spec(tb, s) for s in rows] + [_whole_spec(p) for p in params],
        out_specs=[pl.BlockSpec((tb, w), lambda i: (i, 0)) for w, _ in outs],
        out_shape=[jax.ShapeDtypeStruct((t, w), dt) for w, dt in outs],
        compiler_params=_cparams(("arbitrary",)),
    )(*[s[0] for s in rows], *params)


def _rw_bwd(name, fn, rows, params, gouts, tb, diff_rows):
    t = rows[0][0].shape[0]
    nr, npar, ng, nd = len(rows), len(params), len(gouts), len(diff_rows)

    def body(*refs):
        i = pl.program_id(0)
        rv = [r[...] for r in refs[:nr]]
        pv = [p[...] for p in refs[nr:nr + npar]]
        gv = [g[...].astype(F32) for g in refs[nr + npar:nr + npar + ng]]
        out_refs = refs[nr + npar + ng:]

        def f(dr, pvals):
            vals = list(rv)
            for k, idx in enumerate(diff_rows):
                vals[idx] = dr[k]
            return tuple(fn(i, vals, pvals))

        _, vjp = jax.vjp(f, [rv[k].astype(F32) for k in diff_rows], pv)
        drows, dpars = vjp(tuple(gv))
        for k in range(nd):
            out_refs[k][...] = drows[k]

        @pl.when(i == 0)
        def _():
            for k in range(npar):
                out_refs[nd + k][...] = jnp.zeros_like(out_refs[nd + k])

        for k in range(npar):
            out_refs[nd + k][...] += dpars[k]

    res = pl.pallas_call(
        body, name=name, grid=(t // tb,),
        in_specs=([_row_spec(tb, s) for s in rows] + [_whole_spec(p) for p in params]
                  + [pl.BlockSpec((tb, g.shape[1]), lambda i: (i, 0)) for g in gouts]),
        out_specs=([pl.BlockSpec((tb, rows[k][2]), lambda i: (i, 0)) for k in diff_rows]
                   + [_whole_spec(p) for p in params]),
        out_shape=([jax.ShapeDtypeStruct((t, rows[k][2]), F32) for k in diff_rows]
                   + [jax.ShapeDtypeStruct(p.shape, F32) for p in params]),
        compiler_params=_cparams(("arbitrary",)),
    )(*[s[0] for s in rows], *params, *gouts)
    return list(res[:nd]), list(res[nd:])


def _full(arr):
    return (arr, 0, arr.shape[1])


def _make_rw_op(fn_factory, n_rows, diff_rows, out_widths):
    @functools.partial(jax.custom_vjp, nondiff_argnums=(0, 1))
    def op(cfg, tb, *args):
        return tuple(_rw_fwd(cfg[0] + '_fwd', fn_factory(cfg), [_full(a) for a in args[:n_rows]], list(args[n_rows:]),
                             [(w, F32) for w in out_widths(cfg, args)], tb))

    def fwd(cfg, tb, *args):
        return op(cfg, tb, *args), args

    def bwd(cfg, tb, args, g):
        drows, dpars = _rw_bwd(cfg[0] + '_bwd', fn_factory(cfg), [_full(a) for a in args[:n_rows]],
                               list(args[n_rows:]), list(g), tb, diff_rows)
        full = [jnp.zeros_like(a) for a in args[:n_rows]]
        for k, idx in enumerate(diff_rows):
            full[idx] = drows[k]
        return tuple(full) + tuple(dpars)

    op.defvjp(fwd, bwd)
    return op


def _silu_fn(cfg):
    return lambda i, rows, params: (jax.nn.silu(rows[0]),)


_silu_op = _make_rw_op(_silu_fn, 1, (0,), lambda cfg, args: (args[0].shape[1],))


def _normmod_fn(cfg):
    _, shift_at, scale_at = cfg

    def fn(i, rows, params):
        (h,) = rows
        nw, mod16, b_ada = params
        mr = _mod_row(i, mod16, b_ada)
        d = h.shape[1]
        return (_rms(h) * nw * (1.0 + mr[:, scale_at * d:(scale_at + 1) * d]) + mr[:, shift_at * d:(shift_at + 1) * d],)

    return fn


_normmod_op = _make_rw_op(_normmod_fn, 1, (0,), lambda cfg, args: (args[0].shape[1],))


def _resid_fn(cfg):
    _, gate_at = cfg

    def fn(i, rows, params):
        h, y = rows
        mod16, b_ada = params
        mr = _mod_row(i, mod16, b_ada)
        d = h.shape[1]
        return (h + mr[:, gate_at * d:(gate_at + 1) * d] * y,)

    return fn


_resid_op = _make_rw_op(_resid_fn, 2, (0, 1), lambda cfg, args: (args[0].shape[1],))


def _resid_norm_fn(cfg):
    _, gate_at, shift_at, scale_at = cfg

    def fn(i, rows, params):
        h, y = rows
        mod_r, b_r, nw, mod_n, b_n = params
        d = h.shape[1]
        h = h + _mod_row(i, mod_r, b_r)[:, gate_at * d:(gate_at + 1) * d] * y
        mn = _mod_row(i, mod_n, b_n)
        return h, _rms(h) * nw * (1.0 + mn[:, scale_at * d:(scale_at + 1) * d]) + mn[:, shift_at * d:(shift_at + 1) * d]

    return fn


_resid_norm_op = _make_rw_op(_resid_norm_fn, 2, (0, 1), lambda cfg, args: (args[0].shape[1],) * 2)


def _merge_fn(cfg):
    def fn(i, rows, params):
        z0, z1, z2, pg = rows
        (bg,) = params
        d = z0.shape[1]
        out = None
        for n, z in enumerate((z0, z1, z2)):
            term = jax.nn.sigmoid(pg[:, n * d:(n + 1) * d] + bg[:, n * d:(n + 1) * d]) * z
            out = term if out is None else out + term
        return (out,)

    return fn


_merge_op = _make_rw_op(_merge_fn, 4, (0, 1, 2, 3), lambda cfg, args: (args[0].shape[1],))


def _matmul(name, a, b, form):
    if form == 'nn':
        (m, k), (_, n) = a.shape, b.shape
        tm = _tile(m, MM_ROWS, 8)
        tn = _tile(n, max(LANES, WEIGHT_BLOCK_BYTES // (k * b.dtype.itemsize)), LANES)

        def body(a_ref, b_ref, o_ref):
            o_ref[...] = _raw_mm(a_ref[...], b_ref[...], 'nn')

        return pl.pallas_call(
            body, name=name, grid=(n // tn, m // tm),
            in_specs=[pl.BlockSpec((tm, k), lambda j, i: (i, 0)), pl.BlockSpec((k, tn), lambda j, i: (0, j))],
            out_specs=pl.BlockSpec((tm, tn), lambda j, i: (i, j)),
            out_shape=jax.ShapeDtypeStruct((m, n), F32), compiler_params=_cparams(("arbitrary", "arbitrary")),
        )(a, b)
    if form == 'nt':
        (m, n), (k, _) = a.shape, b.shape
        tm = _tile(m, MM_ROWS, 8)
        tk = _tile(k, max(LANES, WEIGHT_BLOCK_BYTES // (n * b.dtype.itemsize)), LANES)

        def body(a_ref, b_ref, o_ref):
            o_ref[...] = _raw_mm(a_ref[...], b_ref[...], 'nt')

        return pl.pallas_call(
            body, name=name, grid=(k // tk, m // tm),
            in_specs=[pl.BlockSpec((tm, n), lambda j, i: (i, 0)), pl.BlockSpec((tk, n), lambda j, i: (j, 0))],
            out_specs=pl.BlockSpec((tm, tk), lambda j, i: (i, j)),
            out_shape=jax.ShapeDtypeStruct((m, k), F32), compiler_params=_cparams(("arbitrary", "arbitrary")),
        )(a, b)
    (m, ka), (_, n) = a.shape, b.shape
    tka = _tile(ka, ACC_MAX_ROWS, LANES)
    tn, tmc = _tile(n, max(LANES, ACC_BLOCK_BYTES // (4 * tka)), LANES), _tile(m, MM_ROWS, 8)

    def body(a_ref, b_ref, o_ref):
        @pl.when(pl.program_id(2) == 0)
        def _():
            o_ref[...] = jnp.zeros_like(o_ref)

        o_ref[...] += _raw_mm(a_ref[...], b_ref[...], 'tn')

    return pl.pallas_call(
        body, name=name, grid=(ka // tka, n // tn, m // tmc),
        in_specs=[pl.BlockSpec((tmc, tka), lambda i, j, s: (s, i)), pl.BlockSpec((tmc, tn), lambda i, j, s: (s, j))],
        out_specs=pl.BlockSpec((tka, tn), lambda i, j, s: (i, j)),
        out_shape=jax.ShapeDtypeStruct((ka, n), F32), compiler_params=_cparams(("arbitrary", "arbitrary", "arbitrary")),
    )(a, b)


@functools.partial(jax.custom_vjp, nondiff_argnums=(0,))
def _mm_op(name, a, w):
    return _matmul(name + '_fwd', a, w.astype(MXU_DTYPE), 'nn')


def _mm_op_fwd(name, a, w):
    wb = w.astype(MXU_DTYPE)
    return _matmul(name + '_fwd', a, wb, 'nn'), (a, wb)


def _mm_op_bwd(name, res, g):
    a, wb = res
    return _matmul(name + '_da', g, wb, 'nt'), _matmul(name + '_dw', a, g, 'tn')


_mm_op.defvjp(_mm_op_fwd, _mm_op_bwd)


def _matmul_nt_sum(name, gs, ws):
    m, k, n = gs[0].shape[0], ws[0].shape[0], len(gs)
    tm = _tile(m, 256, 8)
    row_bytes = sum(w.shape[1] * w.dtype.itemsize for w in ws)
    tk = _tile(k, max(LANES, WEIGHT_BLOCK_BYTES // row_bytes), LANES)

    def body(*refs):
        acc = _raw_mm(refs[0][...], refs[n][...], 'nt')
        for p in range(1, n):
            acc = acc + _raw_mm(refs[p][...], refs[n + p][...], 'nt')
        refs[2 * n][...] = acc

    return pl.pallas_call(
        body, name=name, grid=(k // tk, m // tm),
        in_specs=([pl.BlockSpec((tm, g.shape[1]), lambda j, i: (i, 0)) for g in gs]
                  + [pl.BlockSpec((tk, w.shape[1]), lambda j, i: (j, 0)) for w in ws]),
        out_specs=pl.BlockSpec((tm, tk), lambda j, i: (i, j)),
        out_shape=jax.ShapeDtypeStruct((m, k), F32), compiler_params=_cparams(("arbitrary", "arbitrary")),
    )(*gs, *ws)


@functools.partial(jax.custom_vjp, nondiff_argnums=(0,))
def _proj_op(names, a, ws):
    return tuple(_matmul(nm + '_fwd', a, w.astype(MXU_DTYPE), 'nn') for nm, w in zip(names, ws))


def _proj_op_fwd(names, a, ws):
    wbs = tuple(w.astype(MXU_DTYPE) for w in ws)
    return tuple(_matmul(nm + '_fwd', a, wb, 'nn') for nm, wb in zip(names, wbs)), (a, wbs)


def _proj_op_bwd(names, res, gs):
    a, wbs = res
    da = _matmul_nt_sum(names[0] + '_da_all', list(gs), list(wbs))
    return da, tuple(_matmul(nm + '_dw', a, g, 'tn') for nm, g in zip(names, gs))


_proj_op.defvjp(_proj_op_fwd, _proj_op_bwd)


def _rope128(x, c, a, b):
    return x * c + _roll(x, LANES - 8, 1) * a + _roll(x, 8, 1) * b


def _mla_prep_fn(i, rows, params):
    pm, c, a, b = rows
    qna, wqb, kvna, wkn, wv, qn, kn = params
    cq, ckv, kr_slot = pm[:, 0:256], pm[:, 256:384], pm[:, 384:512]
    q_all = _mm(_rms(cq) * qna, wqb, 'nn')
    ckvn = _rms(ckv) * kvna
    k_all = _mm(ckvn, wkn, 'nn')
    v_all = _mm(ckvn, wv, 'nn')
    qs, ks = [], []
    for h in range(MLA_HEADS):
        sl = slice(LANES * h, LANES * (h + 1))
        qs.append(_rope128(_rms(q_all[:, sl], MLA_QK) * qn, c, a, b) * MLA_Q_SCALE)
        ks.append(_rope128(_rms(k_all[:, sl] + kr_slot, MLA_QK) * kn, c, a, b))
    return jnp.concatenate(qs, axis=1), jnp.concatenate(ks, axis=1), v_all


def _attn_fwd(name, q, k, v, tb, ctx_len):
    t = q.shape[0]

    def body(q_ref, k_ref, v_ref, o_ref, lse_ref):
        qi = pl.program_id(1)

        def attend(k, v):
            s2 = _raw_mm(q_ref[...], k, 'nt')
            m2 = jnp.max(s2, axis=-1, keepdims=True)
            p = jnp.exp2(s2 - m2)
            l = jnp.sum(p, axis=-1, keepdims=True)
            o_ref[...] = _raw_mm(p, v, 'nn') / l
            lse_ref[...] = jnp.broadcast_to((m2 + jnp.log2(l)) * LN2, lse_ref.shape)

        @pl.when(qi == 0)
        def _():
            attend(k_ref[0:ctx_len, :], v_ref[0:ctx_len, :])

        @pl.when(qi != 0)
        def _():
            attend(k_ref[...], v_ref[...])

    blk = pl.BlockSpec((tb, LANES), lambda h, i: (i, h))
    whole = pl.BlockSpec((t, LANES), lambda h, i: (0, h))
    return pl.pallas_call(
        body, name=name, grid=(MLA_HEADS, t // tb), in_specs=[blk, whole, whole], out_specs=[blk, blk],
        out_shape=[jax.ShapeDtypeStruct(q.shape, F32)] * 2, compiler_params=_cparams(("arbitrary", "arbitrary")),
    )(q, k, v)


def _attn_bwd(name, q, k, v, o, lse, do, tb, ctx_len):
    t = q.shape[0]
    ck = _tile(t, 2816, 256)

    def body(q_ref, k_ref, v_ref, o_ref, lse_ref, do_ref, dq_ref, dk_ref, dv_ref):
        qi = pl.program_id(1)

        @pl.when(qi == 0)
        def _():
            dk_ref[...] = jnp.zeros_like(dk_ref)
            dv_ref[...] = jnp.zeros_like(dv_ref)

        q = q_ref[...]
        do = do_ref[...].astype(MXU_DTYPE)
        lse2 = lse_ref[...][:, 0:1] * LOG2E
        delta = jnp.sum(do_ref[...] * o_ref[...], axis=-1, keepdims=True)

        def part(rows):
            ks, vs = k_ref[rows, :], v_ref[rows, :]
            p = jnp.exp2(_raw_mm(q, ks, 'nt') - lse2)
            ds = p * ((_raw_mm(do, vs, 'nt') - delta) * LN2)
            dk_ref[rows, :] += _raw_mm(ds, q, 'tn')
            dv_ref[rows, :] += _raw_mm(p, do, 'tn')
            return _raw_mm(ds, ks, 'nn')

        @pl.when(qi == 0)
        def _():
            dq_ref[...] = part(pl.ds(0, ctx_len))

        @pl.when(qi != 0)
        def _():
            dq = part(pl.ds(0, ck))
            for c in range(1, t // ck):
                dq = dq + part(pl.ds(c * ck, ck))
            dq_ref[...] = dq

    blk = pl.BlockSpec((tb, LANES), lambda h, i: (i, h))
    whole = pl.BlockSpec((t, LANES), lambda h, i: (0, h))
    return pl.pallas_call(
        body, name=name, grid=(MLA_HEADS, t // tb), in_specs=[blk, whole, whole, blk, blk, blk],
        out_specs=[blk, whole, whole], out_shape=[jax.ShapeDtypeStruct(q.shape, F32)] * 3,
        compiler_params=_cparams(("arbitrary", "arbitrary")),
    )(q, k, v, o, lse, do)


def _mla_rows(pm, tabs):
    return [(pm, 0, pm.shape[1])] + [_full(x) for x in tabs]


@functools.partial(jax.custom_vjp, nondiff_argnums=(0, 1))
def _mla_branch(name, tb, pm, tabs, params):
    return _mla_branch_fwd(name, tb, pm, tabs, params)[0]


def _mla_branch_fwd(name, tb, pm, tabs, params):
    w = LANES * MLA_HEADS
    q, k, v = _rw_fwd(name + '_prep', _mla_prep_fn, _mla_rows(pm, tabs), list(params), [(w, MXU_DTYPE)] * 3, tb)
    o, lse = _attn_fwd(name + '_attn', q, k, v, tb, tb)
    return o, (pm, tabs, params, q, k, v, o, lse)


def _mla_branch_bwd(name, tb, res, do):
    pm, tabs, params, q, k, v, o, lse = res
    dq, dk, dv = _attn_bwd(name + '_attn_bwd', q, k, v, o, lse, do, tb, tb)
    (dpm,), dpars = _rw_bwd(name + '_prep_bwd', _mla_prep_fn, _mla_rows(pm, tabs), list(params), [dq, dk, dv], tb, (0,))
    return dpm, tuple(jnp.zeros_like(x) for x in tabs), tuple(dpars)


_mla_branch.defvjp(_mla_branch_fwd, _mla_branch_bwd)


def _chunk_masks(reverse):
    i = lax.broadcasted_iota(jnp.int32, (CHUNK, CHUNK), 0)
    j = lax.broadcasted_iota(jnp.int32, (CHUNK, CHUNK), 1)
    return i, j, ((j > i) if reverse else (j <= i))


def _gla_chunk(reverse, rows, params, st0):
    q, k, v, la = rows
    q = q * (GLA_DK ** -0.5)
    cum = _cumsum_rows(la, reverse)
    tot = cum[0:1] if reverse else cum[CHUNK - 1:CHUNK]
    vt = v.T
    st1 = st0 * jnp.exp(tot) + _mm(vt, k * jnp.exp(tot - cum), 'nn')
    qd = q * jnp.exp(cum)
    _, _, mask = _chunk_masks(reverse)
    att = jnp.where(mask, _mm(qd, k * jnp.exp(-cum), 'nt'), 0.0)
    return _mm(jnp.concatenate([qd, att], axis=1), jnp.concatenate([st0, vt], axis=1), 'nt'), st1


def _ret_chunk(reverse, rows, params, st0):
    q, k, v, cc, ss = rows
    (lg,) = params
    q = q * cc + _roll(q, RET_DK // 2, 1) * ss
    k = (k * cc + _roll(k, RET_DK // 2, 1) * ss) * (RET_DK ** -0.5)
    r = lax.broadcasted_iota(jnp.int32, (CHUNK, LANES), 0).astype(F32)
    zeta = jnp.exp((r if reverse else (CHUNK - 1.0 - r)) * lg)
    xi = jnp.exp(((CHUNK - r) if reverse else (r + 1.0)) * lg)
    vt = v.T
    st1 = st0 * jnp.exp(CHUNK * lg) + _mm(vt, k * zeta, 'nn')
    i, j, mask = _chunk_masks(reverse)
    rel = jnp.where(mask, (j - i) if reverse else (i - j), 0).astype(F32)
    dmat = jnp.where(mask, jnp.exp(rel * lg[:, 0:CHUNK]), 0.0)
    att = _mm(q, k, 'nt') * dmat
    return _mm(jnp.concatenate([q * xi, att], axis=1), jnp.concatenate([st0, vt], axis=1), 'nt'), st1


def _scan_order(reverse, nblk):
    if reverse:
        return lambda t: jnp.where(t == 0, 0, nblk - t)
    return lambda t: t


def _scan_specs(rows, params, tb, heads, blk_of):
    def rspec(spec):
        _, cb = spec
        if cb is None:
            return pl.BlockSpec((tb, LANES), lambda s: (blk_of(s), 0))
        return pl.BlockSpec((tb, heads * LANES), lambda s, cb=cb: (blk_of(s), cb // heads))

    return [rspec(s) for s in rows] + [pl.BlockSpec((heads, 1, LANES), lambda s: (0, 0, 0)) for _ in params]


def _head_rows(row_refs, rows, sl, h):
    lanes = pl.ds(h * LANES, LANES)
    return [r[sl, :] if spec[1] is None else r[sl, lanes] for r, spec in zip(row_refs, rows)]


def _scan_fwd(name, chunk_fn, reverse, rows, params, tb, heads):
    t = rows[0][0].shape[0]
    nblk, cpb = t // tb, tb // CHUNK
    blk_of = _scan_order(reverse, nblk)
    nr, npar = len(rows), len(params)
    order = list(range(cpb))[::-1] if reverse else list(range(cpb))

    def body(*refs):
        row_refs, par_refs = refs[:nr], refs[nr:nr + npar]
        o_ref, st_out_ref, st_ref = refs[nr + npar:]

        @pl.when(pl.program_id(0) == 0)
        def _():
            st_ref[...] = jnp.zeros_like(st_ref)

        for c in order:
            sl = pl.ds(c * CHUNK, CHUNK)
            for h in range(heads):
                st0 = st_ref[h]
                st_out_ref[h, c] = st0
                o, st1 = chunk_fn(reverse, _head_rows(row_refs, rows, sl, h), [p[h] for p in par_refs], st0)
                o_ref[sl, pl.ds(h * LANES, LANES)] = o
                st_ref[h] = st1

    return pl.pallas_call(
        body, name=name, grid=(nblk,), in_specs=_scan_specs(rows, params, tb, heads, blk_of),
        out_specs=[pl.BlockSpec((tb, heads * LANES), lambda s: (blk_of(s), 0)),
                   pl.BlockSpec((heads, cpb, LANES, LANES), lambda s: (0, blk_of(s), 0, 0))],
        out_shape=[jax.ShapeDtypeStruct((t, heads * LANES), F32),
                   jax.ShapeDtypeStruct((heads, t // CHUNK, LANES, LANES), F32)],
        scratch_shapes=[pltpu.VMEM((heads, LANES, LANES), F32)],
        compiler_params=_cparams(("arbitrary",)),
    )(*[s[0] for s in rows], *params)


def _scan_bwd(name, chunk_fn, reverse, rows, params, states, do, tb, heads, n_diff):
    t = rows[0][0].shape[0]
    nblk, cpb = t // tb, tb // CHUNK
    fwd_blk = _scan_order(reverse, nblk)
    blk_of = lambda s: fwd_blk(nblk - 1 - s)
    nr, npar = len(rows), len(params)
    order = list(range(cpb)) if reverse else list(range(cpb))[::-1]

    def body(*refs):
        row_refs, par_refs = refs[:nr], refs[nr:nr + npar]
        st_in_ref, do_ref = refs[nr + npar:nr + npar + 2]
        out_refs = refs[nr + npar + 2:-1]
        dst_ref = refs[-1]

        @pl.when(pl.program_id(0) == 0)
        def _():
            dst_ref[...] = jnp.zeros_like(dst_ref)
            for k in range(npar):
                out_refs[n_diff + k][...] = jnp.zeros_like(out_refs[n_diff + k])

        for c in order:
            sl = pl.ds(c * CHUNK, CHUNK)
            for h in range(heads):
                lanes = pl.ds(h * LANES, LANES)
                rv = _head_rows(row_refs, rows, sl, h)

                def f(dr, pvals, st0, rv=rv):
                    return chunk_fn(reverse, list(dr) + rv[n_diff:], pvals, st0)

                _, vjp = jax.vjp(f, rv[:n_diff], [p[h] for p in par_refs], st_in_ref[h, c])
                drows, dpars, dst0 = vjp((do_ref[sl, lanes], dst_ref[h]))
                for k in range(n_diff):
                    out_refs[k][sl, lanes] = drows[k]
                for k in range(npar):
                    out_refs[n_diff + k][h] += dpars[k]
                dst_ref[h] = dst0

    wide = pl.BlockSpec((tb, heads * LANES), lambda s: (blk_of(s), 0))
    pblk = pl.BlockSpec((heads, 1, LANES), lambda s: (0, 0, 0))
    res = pl.pallas_call(
        body, name=name, grid=(nblk,),
        in_specs=(_scan_specs(rows, params, tb, heads, blk_of)
                  + [pl.BlockSpec((heads, cpb, LANES, LANES), lambda s: (0, blk_of(s), 0, 0)), wide]),
        out_specs=[wide] * n_diff + [pblk for _ in params],
        out_shape=([jax.ShapeDtypeStruct((t, heads * LANES), F32)] * n_diff
                   + [jax.ShapeDtypeStruct(p.shape, F32) for p in params]),
        scratch_shapes=[pltpu.VMEM((heads, LANES, LANES), F32)],
        compiler_params=_cparams(("arbitrary",)),
    )(*[s[0] for s in rows], *params, states, do)
    return list(res[:n_diff]), list(res[n_diff:])


def _gla_la_fn(i, rows, params):
    (r,) = rows
    w2f, w2b, bgk = params
    la_f = jax.nn.log_sigmoid(_mm(r, w2f, 'nn') + bgk[0:1]) / GLA_GATE_NORMALIZER
    la_b = jax.nn.log_sigmoid(_mm(r, w2b, 'nn') + bgk[1:2]) / GLA_GATE_NORMALIZER
    return la_f, la_b


def _headnorm_fn(heads, with_weight):
    def fn(i, rows, params):
        o_f, o_b, g = rows
        outs = []
        for h in range(heads):
            sl = slice(LANES * h, LANES * (h + 1))
            y = _rms(o_f[:, sl] + o_b[:, sl])
            outs.append(y * params[0] if with_weight else y)
        return (jnp.concatenate(outs, axis=1) * jax.nn.silu(g),)

    return fn


@functools.partial(jax.custom_vjp, nondiff_argnums=(0, 1))
def _gla_branch(name, tb, pg, params):
    return _gla_branch_fwd(name, tb, pg, params)[0]


def _gla_rows(pg, la):
    return [(pg, 0), (pg, GLA_HEADS), (pg, 2 * GLA_HEADS), (la, 0)]


def _gla_branch_fwd(name, tb, pg, params):
    w2f, w2b, bgk, onorm = params
    w = GLA_HEADS * LANES
    la_f, la_b = _rw_fwd(name + '_la', _gla_la_fn, [(pg, 4 * w // LANES, LANES)], [w2f, w2b, bgk], [(w, F32)] * 2, tb)
    o_f, st_f = _scan_fwd(name + '_scan_f', _gla_chunk, False, _gla_rows(pg, la_f), [], tb, GLA_HEADS)
    o_b, st_b = _scan_fwd(name + '_scan_b', _gla_chunk, True, _gla_rows(pg, la_b), [], tb, GLA_HEADS)
    (y,) = _rw_fwd(name + '_norm', _headnorm_fn(GLA_HEADS, True), [_full(o_f), _full(o_b), (pg, 3, w)], [onorm],
                   [(w, F32)], tb)
    return y, (pg, params, la_f, la_b, o_f, o_b, st_f, st_b)


def _gla_branch_bwd(name, tb, res, dy):
    pg, params, la_f, la_b, o_f, o_b, st_f, st_b = res
    w2f, w2b, bgk, onorm = params
    w = GLA_HEADS * LANES
    (do_f, do_b, dg), (donorm,) = _rw_bwd(name + '_norm_bwd', _headnorm_fn(GLA_HEADS, True),
                                          [_full(o_f), _full(o_b), (pg, 3, w)], [onorm], [dy], tb, (0, 1, 2))
    (dq_f, dk_f, dv_f, dla_f), _ = _scan_bwd(name + '_scan_f_bwd', _gla_chunk, False, _gla_rows(pg, la_f), [], st_f,
                                             do_f, tb, GLA_HEADS, 4)
    (dq_b, dk_b, dv_b, dla_b), _ = _scan_bwd(name + '_scan_b_bwd', _gla_chunk, True, _gla_rows(pg, la_b), [], st_b,
                                             do_b, tb, GLA_HEADS, 4)
    (dr,), (dw2f, dw2b, dbgk) = _rw_bwd(name + '_la_bwd', _gla_la_fn, [(pg, 4 * w // LANES, LANES)], [w2f, w2b, bgk],
                                        [dla_f, dla_b], tb, (0,))
    dpg = jnp.concatenate([dq_f + dq_b, dk_f + dk_b, dv_f + dv_b, dg, dr], axis=1)
    return dpg, (dw2f, dw2b, dbgk, donorm)


_gla_branch.defvjp(_gla_branch_fwd, _gla_branch_bwd)


@functools.partial(jax.custom_vjp, nondiff_argnums=(0, 1))
def _ret_branch(name, tb, pr, tabs, lg):
    return _ret_branch_fwd(name, tb, pr, tabs, lg)[0]


def _ret_rows(pr, tabs):
    return [(pr, 0), (pr, RET_HEADS), (pr, 2 * RET_HEADS), (tabs[0], None), (tabs[1], None)]


def _ret_branch_fwd(name, tb, pr, tabs, lg):
    w = RET_HEADS * LANES
    o_f, st_f = _scan_fwd(name + '_scan_f', _ret_chunk, False, _ret_rows(pr, tabs), [lg[0]], tb, RET_HEADS)
    o_b, st_b = _scan_fwd(name + '_scan_b', _ret_chunk, True, _ret_rows(pr, tabs), [lg[1]], tb, RET_HEADS)
    (y,) = _rw_fwd(name + '_norm', _headnorm_fn(RET_HEADS, False), [_full(o_f), _full(o_b), (pr, 3, w)], [],
                   [(w, F32)], tb)
    return y, (pr, tabs, lg, o_f, o_b, st_f, st_b)


def _ret_branch_bwd(name, tb, res, dy):
    pr, tabs, lg, o_f, o_b, st_f, st_b = res
    w = RET_HEADS * LANES
    (do_f, do_b, dg), _ = _rw_bwd(name + '_norm_bwd', _headnorm_fn(RET_HEADS, False),
                                  [_full(o_f), _full(o_b), (pr, 3, w)], [], [dy], tb, (0, 1, 2))
    (dq_f, dk_f, dv_f), (dlg_f,) = _scan_bwd(name + '_scan_f_bwd', _ret_chunk, False, _ret_rows(pr, tabs), [lg[0]],
                                             st_f, do_f, tb, RET_HEADS, 3)
    (dq_b, dk_b, dv_b), (dlg_b,) = _scan_bwd(name + '_scan_b_bwd', _ret_chunk, True, _ret_rows(pr, tabs), [lg[1]],
                                             st_b, do_b, tb, RET_HEADS, 3)
    dpr = jnp.concatenate([dq_f + dq_b, dk_f + dk_b, dv_f + dv_b, dg], axis=1)
    return dpr, tuple(jnp.zeros_like(x) for x in tabs), jnp.stack([dlg_f, dlg_b])


_ret_branch.defvjp(_ret_branch_fwd, _ret_branch_bwd)


HALO = 8


def _halo_specs(tb, nblk, width, col_block):
    r = tb // HALO
    prev = pl.BlockSpec((HALO, width), lambda i: (jnp.maximum(i * r - 1, 0), col_block))
    nxt = pl.BlockSpec((HALO, width), lambda i: (jnp.minimum((i + 1) * r, nblk * r - 1), col_block))
    return prev, nxt


def _shifted(x, prev_blk, next_blk, i, nblk):
    tb = x.shape[0]
    row = lax.broadcasted_iota(jnp.int32, x.shape, 0)
    prev_row = jnp.where(i >= 2, prev_blk[HALO - 1:HALO], 0.0)
    next_row = jnp.where((i >= 1) & (i < nblk - 1), next_blk[0:1], 0.0)
    down = jnp.where(row == 0, prev_row, pltpu.roll(x, 1, 0))
    up = jnp.where(row == tb - 1, next_row, pltpu.roll(x, tb - 1, 0))
    return down, up


def _gelu_up(c, up):
    return jax.nn.gelu(c) * up


def _convact_fwd_call(name, gu, w_dw, b_dw, tb):
    t = gu.shape[0]
    nblk = t // tb
    prev_spec, next_spec = _halo_specs(tb, nblk, D_FF, 0)

    def body(g_ref, up_ref, prev_ref, next_ref, w_ref, b_ref, o_ref):
        i = pl.program_id(0)
        g = g_ref[...]
        down, upw = _shifted(g, prev_ref[...], next_ref[...], i, nblk)
        w = w_ref[...]
        c = w[0:1] * down + w[1:2] * g + w[2:3] * upw + b_ref[...]
        o_ref[...] = _gelu_up(c, up_ref[...])

    return pl.pallas_call(
        body, name=name, grid=(nblk,),
        in_specs=[pl.BlockSpec((tb, D_FF), lambda i: (i, 0)), pl.BlockSpec((tb, D_FF), lambda i: (i, 1)), prev_spec,
                  next_spec, _whole_spec(w_dw), _whole_spec(b_dw)],
        out_specs=pl.BlockSpec((tb, D_FF), lambda i: (i, 0)), out_shape=jax.ShapeDtypeStruct((t, D_FF), F32),
        compiler_params=_cparams(("arbitrary",)),
    )(gu, gu, gu, gu, w_dw, b_dw)


def _convact_bwd_calls(name, gu, w_dw, b_dw, dact, tb):
    t = gu.shape[0]
    nblk = t // tb
    prev_spec, next_spec = _halo_specs(tb, nblk, D_FF, 0)

    def body1(g_ref, up_ref, prev_ref, next_ref, w_ref, b_ref, da_ref, dc_ref, dup_ref, dw_ref, db_ref):
        i = pl.program_id(0)
        g = g_ref[...]
        down, upw = _shifted(g, prev_ref[...], next_ref[...], i, nblk)
        w = w_ref[...]
        c = w[0:1] * down + w[1:2] * g + w[2:3] * upw + b_ref[...]
        _, vjp = jax.vjp(_gelu_up, c, up_ref[...])
        dc, dup = vjp(da_ref[...])
        dc_ref[...] = dc
        dup_ref[...] = dup

        @pl.when(i == 0)
        def _():
            dw_ref[...] = jnp.zeros_like(dw_ref)
            db_ref[...] = jnp.zeros_like(db_ref)

        dw_ref[0:1, :] += jnp.sum(dc * down, axis=0, keepdims=True)
        dw_ref[1:2, :] += jnp.sum(dc * g, axis=0, keepdims=True)
        dw_ref[2:3, :] += jnp.sum(dc * upw, axis=0, keepdims=True)
        db_ref[...] += jnp.sum(dc, axis=0, keepdims=True)

    blk = pl.BlockSpec((tb, D_FF), lambda i: (i, 0))
    dc, dup, dw, db = pl.pallas_call(
        body1, name=name + '_a', grid=(nblk,),
        in_specs=[blk, pl.BlockSpec((tb, D_FF), lambda i: (i, 1)), prev_spec, next_spec, _whole_spec(w_dw),
                  _whole_spec(b_dw), blk],
        out_specs=[blk, blk, _whole_spec(w_dw), _whole_spec(b_dw)],
        out_shape=[jax.ShapeDtypeStruct((t, D_FF), F32)] * 2 + [jax.ShapeDtypeStruct(w_dw.shape, F32),
                                                                jax.ShapeDtypeStruct(b_dw.shape, F32)],
        compiler_params=_cparams(("arbitrary",)),
    )(gu, gu, gu, gu, w_dw, b_dw, dact)

    def body2(dc_ref, prev_ref, next_ref, dup_ref, w_ref, o_ref):
        i = pl.program_id(0)
        dc_blk = dc_ref[...]
        down, upw = _shifted(dc_blk, prev_ref[...], next_ref[...], i, nblk)
        w = w_ref[...]
        o_ref[:, 0:D_FF] = w[0:1] * upw + w[1:2] * dc_blk + w[2:3] * down
        o_ref[:, D_FF:2 * D_FF] = dup_ref[...]

    dgu = pl.pallas_call(
        body2, name=name + '_b', grid=(nblk,),
        in_specs=[blk, prev_spec, next_spec, blk, _whole_spec(w_dw)],
        out_specs=pl.BlockSpec((tb, 2 * D_FF), lambda i: (i, 0)), out_shape=jax.ShapeDtypeStruct((t, 2 * D_FF), F32),
        compiler_params=_cparams(("arbitrary",)),
    )(dc, dc, dc, dup, w_dw)
    return dgu, dw, db


@functools.partial(jax.custom_vjp, nondiff_argnums=(0, 1))
def _convact(name, tb, gu, w_dw, b_dw):
    return _convact_fwd_call(name + '_fwd', gu, w_dw, b_dw, tb)


def _convact_fwd(name, tb, gu, w_dw, b_dw):
    return _convact_fwd_call(name + '_fwd', gu, w_dw, b_dw, tb), (gu, w_dw, b_dw)


def _convact_bwd(name, tb, res, dact):
    gu, w_dw, b_dw = res
    return _convact_bwd_calls(name + '_bwd', gu, w_dw, b_dw, dact, tb)


_convact.defvjp(_convact_fwd, _convact_bwd)


def _loss_fwd_call(h, target, tb):
    nlat = target.shape[0] // tb

    def body(h_ref, t_ref, o_ref):
        @pl.when(pl.program_id(0) == 0)
        def _():
            o_ref[...] = jnp.zeros_like(o_ref)

        e = h_ref[...] - t_ref[...]
        o_ref[...] += jnp.sum(e * e, axis=0, keepdims=True)

    cols = pl.pallas_call(
        body, name='loss_fwd', grid=(nlat,),
        in_specs=[pl.BlockSpec((tb, D_MODEL), lambda i: (i + 1, 0)), pl.BlockSpec((tb, D_MODEL), lambda i: (i, 0))],
        out_specs=pl.BlockSpec((1, D_MODEL), lambda i: (0, 0)), out_shape=jax.ShapeDtypeStruct((1, D_MODEL), F32),
        compiler_params=_cparams(("arbitrary",)),
    )(h, target)
    return (0.5 / D_MODEL) * jnp.sum(cols)


def _loss_bwd_call(h, target, gbar, tb):
    def body(g_ref, h_ref, t_ref, o_ref):
        live = jnp.where(pl.program_id(0) == 0, 0.0, g_ref[...] * (1.0 / D_MODEL))
        o_ref[...] = live * (h_ref[...] - t_ref[...])

    return pl.pallas_call(
        body, name='loss_bwd', grid=(h.shape[0] // tb,),
        in_specs=[pl.BlockSpec((1, 1), lambda i: (0, 0)), pl.BlockSpec((tb, D_MODEL), lambda i: (i, 0)),
                  pl.BlockSpec((tb, D_MODEL), lambda i: (jnp.maximum(i - 1, 0), 0))],
        out_specs=pl.BlockSpec((tb, D_MODEL), lambda i: (i, 0)), out_shape=jax.ShapeDtypeStruct(h.shape, F32),
        compiler_params=_cparams(("arbitrary",)),
    )(gbar.reshape(1, 1), h, target)


@functools.partial(jax.custom_vjp, nondiff_argnums=(0,))
def _loss_op(tb, h, target):
    return _loss_fwd_call(h, target, tb)


def _loss_op_fwd(tb, h, target):
    return _loss_fwd_call(h, target, tb), (h, target)


def _loss_op_bwd(tb, res, gbar):
    h, target = res
    return _loss_bwd_call(h, target, gbar, tb), jnp.zeros_like(target)


_loss_op.defvjp(_loss_op_fwd, _loss_op_bwd)


def _rope_tables(pos, dim, theta):
    inv = theta ** (-jnp.arange(dim // 2, dtype=F32) * 2.0 / dim)
    ang = pos.astype(F32)[:, None] * inv[None, :]
    return jnp.cos(ang), jnp.sin(ang)


def _mla_tables(seq, ctx_len):
    rows = seq // GRID_W
    row_pos = jnp.repeat(jnp.arange(rows), GRID_W)
    col_pos = jnp.tile(jnp.arange(GRID_W), rows)
    cos_r, sin_r = _rope_tables(row_pos, MLA_ROPE // 2, ROPE_THETA)
    cos_c, sin_c = _rope_tables(col_pos, MLA_ROPE // 2, ROPE_THETA)
    one = jnp.ones((seq, MLA_NOPE), F32)
    z8 = jnp.zeros((seq, 8), F32)
    pad1 = jnp.ones((seq, LANES - MLA_QK), F32)
    pad0 = jnp.zeros((seq, LANES - MLA_QK), F32)
    z64 = jnp.zeros((seq, MLA_NOPE), F32)
    c = jnp.concatenate([one, cos_r, cos_r, cos_c, cos_c, pad1], axis=1)
    a = jnp.concatenate([z64, -sin_r, z8, -sin_c, z8, pad0], axis=1)
    b = jnp.concatenate([z64, z8, sin_r, z8, sin_c, pad0], axis=1)
    ctx_rows = lambda fill: jnp.full((ctx_len, LANES), fill, F32)
    return (jnp.concatenate([ctx_rows(1.0), c]), jnp.concatenate([ctx_rows(0.0), a]),
            jnp.concatenate([ctx_rows(0.0), b]))


def _ret_tables(total):
    inv = 1.0 / (RET_THETA ** jnp.linspace(0.0, 1.0, RET_DK // 2, dtype=F32))
    ang = jnp.arange(total).astype(F32)[:, None] * inv[None, :]
    cos, sin = jnp.cos(ang), jnp.sin(ang)
    return jnp.concatenate([cos, cos], axis=1), jnp.concatenate([-sin, sin], axis=1)


def _head_slots(w, heads, width):
    k = w.shape[0]
    return jnp.pad(w.reshape(k, heads, width), ((0, 0), (0, 0), (0, LANES - width))).reshape(k, heads * LANES)


def _pad_lanes(v, width):
    return jnp.pad(v, (0, LANES - width)).reshape(1, LANES)


IN_MLA_QKV = (0, 384)
IN_MLA_KR = (384, 416)
IN_GLA = (416, 2496)
IN_RET = (2496, 4544)
IN_GATES = (4544, 7616)
W_MLA_COLS = 512
W_GLA_COLS = 2176
BIG = ('w_ada', 'w_in', 'w_branch', 'w_out', 'w_ffn_in', 'w_ffn_out')
BIG_AXIS = dict(w_ada=1, w_in=1, w_branch=2, w_out=0, w_ffn_in=1, w_ffn_out=0)


def _big_layer_weights(full):
    w_in = full['w_in']
    zc = lambda n: jnp.zeros((D_MODEL, n), w_in.dtype)
    wbr = full['w_branch']
    wb_mla = jnp.pad(wbr[0].reshape(MLA_HEADS, MLA_V, D_MODEL), ((0, 0), (0, LANES - MLA_V), (0, 0)))
    return dict(
        ada=full['w_ada'],
        in_mla=jnp.concatenate([w_in[:, slice(*IN_MLA_QKV)], zc(MLA_NOPE), w_in[:, slice(*IN_MLA_KR)],
                                zc(LANES - MLA_QK)], axis=1),
        in_gla=jnp.concatenate([w_in[:, slice(*IN_GLA)], zc(W_GLA_COLS - (IN_GLA[1] - IN_GLA[0]))], axis=1),
        in_ret=w_in[:, slice(*IN_RET)], in_gate=w_in[:, slice(*IN_GATES)],
        br_mla=wb_mla.reshape(MLA_HEADS * LANES, D_MODEL), br_gla=wbr[1], br_ret=wbr[2],
        out=full['w_out'], ffn_in=full['w_ffn_in'], ffn_out=full['w_ffn_out'])


def _big_layer_grads(g):
    n_gla = IN_GLA[1] - IN_GLA[0]
    kr_at = IN_MLA_QKV[1] + MLA_NOPE
    w_in = jnp.concatenate([g['in_mla'][:, slice(*IN_MLA_QKV)], g['in_mla'][:, kr_at:kr_at + MLA_ROPE],
                            g['in_gla'][:, :n_gla], g['in_ret'], g['in_gate']], axis=1)
    br_mla = g['br_mla'].reshape(MLA_HEADS, LANES, D_MODEL)[:, :MLA_V].reshape(MLA_HEADS * MLA_V, D_MODEL)
    return dict(w_ada=g['ada'], w_in=w_in, w_branch=jnp.stack([br_mla, g['br_gla'], g['br_ret']]),
                w_out=g['out'], w_ffn_in=g['ffn_in'], w_ffn_out=g['ffn_out'])


def _row(v):
    return v.reshape(1, -1)


def _mixers(l, big, w, a, tabs_mla, tabs_ret, tb):
    nm = 'l%d_' % l
    row = _row
    pm, pg, pr, pgate = _proj_op(tuple(nm + s for s in ('in_mla', 'in_gla', 'in_ret', 'in_gate')), a,
                                 (big['in_mla'], big['in_gla'], big['in_ret'], big['in_gate']))

    kvb = w['mla_w_kvb'][l].reshape(-1, MLA_HEADS, MLA_NOPE + MLA_V)
    kdim = kvb.shape[0]
    mla_params = (row(w['mla_q_norm_a'][l]), _head_slots(w['mla_w_qb'][l], MLA_HEADS, MLA_QK),
                  row(w['mla_kv_norm_a'][l]), _head_slots(kvb[:, :, :MLA_NOPE].reshape(kdim, -1), MLA_HEADS, MLA_NOPE),
                  _head_slots(kvb[:, :, MLA_NOPE:].reshape(kdim, -1), MLA_HEADS, MLA_V),
                  _pad_lanes(w['mla_q_norm'][l], MLA_QK), _pad_lanes(w['mla_k_norm'][l], MLA_QK))
    y_mla = _mla_branch(nm + 'mla', tb, pm, tabs_mla, mla_params)

    gk2 = w['gla_w_gk2'][l]
    rank = gk2.shape[1]
    w2f = jnp.pad(gk2[0], ((0, LANES - rank), (0, 0)))
    w2b = jnp.pad(gk2[1], ((rank, LANES - 2 * rank), (0, 0)))
    y_gla = _gla_branch(nm + 'gla', tb, pg, (w2f, w2b, w['gla_b_gk'][l], row(w['gla_o_norm'][l])))

    log_g = -jnp.exp(w['ret_decay'][l])
    lg = jnp.broadcast_to(log_g[:, :, None, None], (2, RET_HEADS, 1, LANES))
    y_ret = _ret_branch(nm + 'ret', tb, pr, tabs_ret, lg)

    z0 = _mm_op(nm + 'br_mla', y_mla, big['br_mla'])
    z1 = _mm_op(nm + 'br_gla', y_gla, big['br_gla'])
    z2 = _mm_op(nm + 'br_ret', y_ret, big['br_ret'])
    u = _merge_op((nm + 'merge',), tb, z0, z1, z2, pgate, row(w['b_gate'][l]))[0]
    return _mm_op(nm + 'out', u, big['out'])


def _ffn(l, big, w, a2, tb):
    nm = 'l%d_' % l
    gu = _mm_op(nm + 'ffn_in', a2, big['ffn_in'])
    act = _convact(nm + 'convact', tb, gu, w['w_dw'][l], _row(w['b_dw'][l]))
    return _mm_op(nm + 'ffn_out', act, big['ffn_out'])


def _local_loss(big, w, x, c, ctx, target):
    seq, tb = x.shape[1], ctx.shape[1]
    h = jnp.concatenate([ctx[0], x[0]], axis=0)
    cond_in = jnp.concatenate([w['c_ctx'].reshape(1, -1), c, jnp.zeros((14, D_MODEL), F32)], axis=0)
    cond16 = _silu_op(('cond_silu',), 16, cond_in)[0]
    tabs_mla = _mla_tables(seq, tb)
    tabs_ret = _ret_tables(seq + tb)
    mods = [_mm_op('l%d_ada' % l, cond16, big[l]['ada']) for l in range(DEPTH)]
    b_ada = [_row(w['b_ada'][l]) for l in range(DEPTH)]
    a = _normmod_op(('l0_norm1', 0, 1), tb, h, _row(w['norm1_w'][0]), mods[0], b_ada[0])[0]
    for l in range(DEPTH):
        nm = 'l%d_' % l
        y = _mixers(l, big[l], w, a, tabs_mla, tabs_ret, tb)
        h, a2 = _resid_norm_op((nm + 'res1_norm2', 2, 3, 4), tb, h, y, mods[l], b_ada[l], _row(w['norm2_w'][l]),
                               mods[l], b_ada[l])
        f = _ffn(l, big[l], w, a2, tb)
        if l + 1 < DEPTH:
            h, a = _resid_norm_op((nm + 'res2_norm1', 5, 0, 1), tb, h, f, mods[l], b_ada[l],
                                  _row(w['norm1_w'][l + 1]), mods[l + 1], b_ada[l + 1])
        else:
            h = _resid_op((nm + 'res2', 5), tb, h, f, mods[l], b_ada[l])[0]
    return _loss_op(tb, h, target[0])


ANY = pl.BlockSpec(memory_space=pl.ANY)
FLAT_W = 1024


def _my_place():
    return lax.axis_index('x'), lax.axis_index('y'), lax.axis_index('c')


def _other_chips(x, y):
    return [(1 - x, y), (x, 1 - y), (1 - x, 1 - y)]


def _gather_chips(name, arrs):
    n = len(arrs)

    def body(*refs):
        ins, outs = refs[:n], refs[n:2 * n]
        send_sems, recv_sems, pass_send_sems, pass_recv_sems, own_send_sems, own_recv_sems = refs[2 * n:]
        x, y, c = _my_place()
        me = 2 * x + y
        chips = _other_chips(x, y)

        def half(a, which):
            h = arrs[a].shape[0] // 2
            return pl.ds(which * h, h)

        def ici(j, a, chip_slot, to):
            src = ins[a].at[half(a, c)] if chip_slot is None else outs[a].at[chip_slot, half(a, c)]
            return pltpu.make_async_remote_copy(
                src_ref=src, dst_ref=outs[a].at[me if chip_slot is None else chip_slot, half(a, c)],
                send_sem=send_sems.at[j, a], recv_sem=recv_sems.at[j, a], device_id=to, device_id_type=MESH)

        def passed(j, a, chip_slot, which):
            rows = outs[a].at[chip_slot, half(a, which)]
            return pltpu.make_async_remote_copy(src_ref=rows, dst_ref=rows, send_sem=pass_send_sems.at[j, a],
                                                recv_sem=pass_recv_sems.at[j, a], device_id=(x, y, 1 - c),
                                                device_id_type=MESH)

        def own(a):
            return pltpu.make_async_remote_copy(src_ref=ins[a], dst_ref=outs[a].at[me], send_sem=own_send_sems.at[a],
                                                recv_sem=own_recv_sems.at[a], device_id=(x, y, 1 - c),
                                                device_id_type=MESH)

        copies = [own(a) for a in range(n)]
        sends = [ici(j, a, None, (px, py, c)) for j, (px, py) in enumerate(chips) for a in range(n)]
        for cp in copies + sends:
            cp.start()
        passes = []
        for j, (px, py) in enumerate(chips):
            for a in range(n):
                ici(j, a, 2 * px + py, (px, py, c)).wait_recv()
                p = passed(j, a, 2 * px + py, c)
                p.start()
                passes.append(p)
        for j, (px, py) in enumerate(chips):
            for a in range(n):
                passed(j, a, 2 * px + py, 1 - c).wait_recv()
        for s in sends + passes:
            s.wait_send()
        for cp in copies:
            cp.wait()

    sems = pltpu.SemaphoreType.DMA((3, n))
    return pl.pallas_call(
        body, name=name, in_specs=[ANY] * n, out_specs=[ANY] * n,
        out_shape=[jax.ShapeDtypeStruct((N_CHIPS,) + a.shape, a.dtype) for a in arrs],
        scratch_shapes=[sems, sems, sems, sems, pltpu.SemaphoreType.DMA((n,)), pltpu.SemaphoreType.DMA((n,))],
    )(*arrs)


def _swap_halves(name, arrs):
    n = len(arrs)

    def body(*refs):
        ins, outs = refs[:n], refs[n:2 * n]
        send_sems, recv_sems = refs[2 * n:]
        x, y, c = _my_place()
        copies = []
        for a in range(n):
            half = arrs[a].shape[1] // 2
            cp = pltpu.make_async_remote_copy(src_ref=ins[a].at[:, pl.ds((1 - c) * half, half)], dst_ref=outs[a],
                                              send_sem=send_sems.at[a], recv_sem=recv_sems.at[a],
                                              device_id=(x, y, 1 - c), device_id_type=MESH)
            cp.start()
            copies.append(cp)
        for cp in copies:
            cp.wait()

    return pl.pallas_call(
        body, name=name, in_specs=[ANY] * n, out_specs=[ANY] * n,
        out_shape=[jax.ShapeDtypeStruct((a.shape[0], a.shape[1] // 2, a.shape[2]), a.dtype) for a in arrs],
        scratch_shapes=[pltpu.SemaphoreType.DMA((n,)), pltpu.SemaphoreType.DMA((n,))],
    )(*arrs)


def _sibling_exchange(name, arrs):
    n = len(arrs)

    def body(*refs):
        ins, outs = refs[:n], refs[n:2 * n]
        send_sems, recv_sems = refs[2 * n:]
        x, y, c = _my_place()
        copies = []
        for a in range(n):
            cp = pltpu.make_async_remote_copy(src_ref=ins[a], dst_ref=outs[a], send_sem=send_sems.at[a],
                                              recv_sem=recv_sems.at[a], device_id=(x, y, 1 - c), device_id_type=MESH)
            cp.start()
            copies.append(cp)
        for cp in copies:
            cp.wait()

    return pl.pallas_call(
        body, name=name, in_specs=[ANY] * n, out_specs=[ANY] * n,
        out_shape=[jax.ShapeDtypeStruct(a.shape, a.dtype) for a in arrs],
        scratch_shapes=[pltpu.SemaphoreType.DMA((n,)), pltpu.SemaphoreType.DMA((n,))],
    )(*arrs)


def _chip_all_to_all(name, arrs):
    n = len(arrs)

    def body(*refs):
        ins, outs = refs[:n], refs[n:2 * n]
        send_sems, recv_sems = refs[2 * n:]
        x, y, c = _my_place()
        sends = []
        for j, (px, py) in enumerate(_other_chips(x, y)):
            for a in range(n):
                s = pltpu.make_async_remote_copy(src_ref=ins[a].at[2 * px + py], dst_ref=outs[a].at[j],
                                                 send_sem=send_sems.at[j, a], recv_sem=recv_sems.at[j, a],
                                                 device_id=(px, py, c), device_id_type=MESH)
                s.start()
                sends.append(s)
        for s in sends:
            s.wait()

    return pl.pallas_call(
        body, name=name, in_specs=[ANY] * n, out_specs=[ANY] * n,
        out_shape=[jax.ShapeDtypeStruct((3,) + a.shape[1:], a.dtype) for a in arrs],
        scratch_shapes=[pltpu.SemaphoreType.DMA((3, n)), pltpu.SemaphoreType.DMA((3, n))],
    )(*arrs)


def _gather_all(name, arr):
    def body(in_ref, out_ref, send_sems, recv_sems, local_sem):
        x, y, c = _my_place()
        me = 4 * x + 2 * y + c
        mine = pltpu.make_async_copy(in_ref, out_ref.at[me], local_sem)
        mine.start()
        peers = []
        for k in range(1, N_DEV):
            px = (1 - x) if k & 4 else x
            py = (1 - y) if k & 2 else y
            pc = (1 - c) if k & 1 else c
            peers.append((px, py, pc))
        sends = []
        for k, peer in enumerate(peers):
            s = pltpu.make_async_remote_copy(src_ref=in_ref, dst_ref=out_ref.at[me], send_sem=send_sems.at[k],
                                             recv_sem=recv_sems.at[k], device_id=peer, device_id_type=MESH)
            s.start()
            sends.append(s)
        for k, (px, py, pc) in enumerate(peers):
            pltpu.make_async_remote_copy(src_ref=in_ref, dst_ref=out_ref.at[4 * px + 2 * py + pc],
                                         send_sem=send_sems.at[k], recv_sem=recv_sems.at[k], device_id=(px, py, pc),
                                         device_id_type=MESH).wait_recv()
        for s in sends:
            s.wait_send()
        mine.wait()

    return pl.pallas_call(
        body, name=name, in_specs=[ANY], out_specs=ANY, out_shape=jax.ShapeDtypeStruct((N_DEV,) + arr.shape, arr.dtype),
        scratch_shapes=[pltpu.SemaphoreType.DMA((N_DEV - 1,)), pltpu.SemaphoreType.DMA((N_DEV - 1,)),
                        pltpu.SemaphoreType.DMA],
    )(arr)


def _flat_rows(r):
    return _tile(r, 512, 16)


def _add_my_half(name, whole, other, my_c, out_dtype):
    n, half, wd = other.shape
    tr = _flat_rows(half)
    nb = half // tr

    def body(c_ref, a_ref, b_ref, o_ref):
        o_ref[...] = (a_ref[...] + b_ref[...]).astype(o_ref.dtype)

    spec = pl.BlockSpec((1, tr, wd), lambda s, i, c: (s, i, 0))
    grid_spec = pltpu.PrefetchScalarGridSpec(
        num_scalar_prefetch=1, grid=(n, nb),
        in_specs=[pl.BlockSpec((1, tr, wd), lambda s, i, c: (s, c[0] * nb + i, 0)), spec], out_specs=spec)
    return pl.pallas_call(body, name=name, grid_spec=grid_spec, out_shape=jax.ShapeDtypeStruct(other.shape, out_dtype),
                          compiler_params=_cparams(("arbitrary",) * 2))(my_c.astype(jnp.int32).reshape(1), whole, other)


def _sum_rows(name, a):
    n, r, wd = a.shape
    tr = _flat_rows(r)

    def body(a_ref, o_ref):
        acc = a_ref[0].astype(F32)
        for k in range(1, n):
            acc = acc + a_ref[k].astype(F32)
        o_ref[...] = acc

    return pl.pallas_call(body, name=name, grid=(r // tr,), in_specs=[pl.BlockSpec((n, tr, wd), lambda i: (0, i, 0))],
                          out_specs=pl.BlockSpec((tr, wd), lambda i: (i, 0)), out_shape=jax.ShapeDtypeStruct((r, wd), F32),
                          compiler_params=_cparams(("arbitrary",)))(a)


def _sum_own_received(name, sums, received, me):
    _, r, wd = sums.shape
    tr = _flat_rows(r)

    def body(me_ref, own_ref, rec_ref, o_ref):
        acc = own_ref[0].astype(F32)
        for k in range(3):
            acc = acc + rec_ref[k].astype(F32)
        o_ref[...] = acc

    grid_spec = pltpu.PrefetchScalarGridSpec(
        num_scalar_prefetch=1, grid=(r // tr,),
        in_specs=[pl.BlockSpec((1, tr, wd), lambda i, me: (me[0], i, 0)), pl.BlockSpec((3, tr, wd), lambda i, me: (0, i, 0))],
        out_specs=pl.BlockSpec((tr, wd), lambda i, me: (i, 0)))
    return pl.pallas_call(body, name=name, grid_spec=grid_spec, out_shape=jax.ShapeDtypeStruct((r, wd), F32),
                          compiler_params=_cparams(("arbitrary",)))(me.astype(jnp.int32).reshape(1), sums, received)


def _adamw_math(w, g, m, v):
    m = ADAM_B1 * m + (1.0 - ADAM_B1) * g
    v = ADAM_B2 * v + (1.0 - ADAM_B2) * (g * g)
    m_hat = m / (1.0 - ADAM_B1 ** ADAM_STEP)
    v_hat = v / (1.0 - ADAM_B2 ** ADAM_STEP)
    return -ADAM_LR * (m_hat / (jnp.sqrt(v_hat) + ADAM_EPS) + ADAM_WD * w), m, v


def _adamw(name, w, g, m, v):
    r, wd = w.shape
    tr = _tile(r, 256, 8)

    def body(w_ref, g_ref, m_ref, v_ref, d_ref, nm_ref, nv_ref):
        d_ref[...], nm_ref[...], nv_ref[...] = _adamw_math(w_ref[...], g_ref[...], m_ref[...], v_ref[...])

    spec = pl.BlockSpec((tr, wd), lambda i: (i, 0))
    return pl.pallas_call(body, name=name, grid=(r // tr,), in_specs=[spec] * 4, out_specs=[spec] * 3,
                          out_shape=[jax.ShapeDtypeStruct((r, wd), F32)] * 3,
                          compiler_params=_cparams(("arbitrary",)))(w, g, m, v)


def _adamw_halves(name, w, g_mine, g_other, m, v, my_c):
    r, wd = w.shape
    half = r // 2
    tr = _tile(half, 256, 8)
    nb = half // tr

    def body(c_ref, w_ref, gm_ref, go_ref, m_ref, v_ref, g_ref, d_ref, nm_ref, nv_ref):
        g = jnp.where(pl.program_id(0) // nb == c_ref[0], gm_ref[...], go_ref[...])
        g_ref[...] = g
        d_ref[...], nm_ref[...], nv_ref[...] = _adamw_math(w_ref[...], g, m_ref[...], v_ref[...])

    spec = pl.BlockSpec((tr, wd), lambda i, c: (i, 0))
    hspec = pl.BlockSpec((tr, wd), lambda i, c: (i % nb, 0))
    grid_spec = pltpu.PrefetchScalarGridSpec(num_scalar_prefetch=1, grid=(r // tr,),
                                             in_specs=[spec, hspec, hspec, spec, spec], out_specs=[spec] * 4)
    return pl.pallas_call(body, name=name, grid_spec=grid_spec, out_shape=[jax.ShapeDtypeStruct((r, wd), F32)] * 4,
                          compiler_params=_cparams(("arbitrary",)))(my_c.astype(jnp.int32).reshape(1), w, g_mine,
                                                                    g_other, m, v)


def _to_flat(parts, dtype, row_multiple):
    flat = jnp.concatenate([p.astype(dtype).reshape(-1) for p in parts])
    unit = FLAT_W * row_multiple
    total = -(-flat.shape[0] // unit) * unit
    return jnp.pad(flat, (0, total - flat.shape[0])).reshape(total // FLAT_W, FLAT_W)


def _from_flat(flat, shapes):
    flat = flat.reshape(-1)
    out, at = [], 0
    for shp in shapes:
        n = 1
        for d in shp:
            n *= d
        out.append(flat[at:at + n].reshape(shp))
        at += n
    return out


def _shard_piece(a, axis, s):
    n = a.shape[axis] // N_CHIPS
    return lax.slice_in_dim(a, s * n, (s + 1) * n, axis=axis)


def kernel(x, c, ctx, c_ctx, w_ada, b_ada, norm1_w, norm2_w, w_in, b_gate, mla_q_norm_a, mla_w_qb, mla_kv_norm_a, mla_w_kvb, mla_q_norm, mla_k_norm, gla_w_gk2, gla_b_gk, gla_o_norm, ret_decay, w_branch, w_out, w_ffn_in, w_dw, b_dw, w_ffn_out, loss_target, m_c_ctx, m_w_ada, m_b_ada, m_norm1_w, m_norm2_w, m_w_in, m_b_gate, m_mla_q_norm_a, m_mla_w_qb, m_mla_kv_norm_a, m_mla_w_kvb, m_mla_q_norm, m_mla_k_norm, m_gla_w_gk2, m_gla_b_gk, m_gla_o_norm, m_ret_decay, m_w_branch, m_w_out, m_w_ffn_in, m_w_dw, m_b_dw, m_w_ffn_out, v_c_ctx, v_w_ada, v_b_ada, v_norm1_w, v_norm2_w, v_w_in, v_b_gate, v_mla_q_norm_a, v_mla_w_qb, v_mla_kv_norm_a, v_mla_w_kvb, v_mla_q_norm, v_mla_k_norm, v_gla_w_gk2, v_gla_b_gk, v_gla_o_norm, v_ret_decay, v_w_branch, v_w_out, v_w_ffn_in, v_w_dw, v_b_dw, v_w_ffn_out):
    local = dict(c_ctx=c_ctx, w_ada=w_ada, b_ada=b_ada, norm1_w=norm1_w, norm2_w=norm2_w, w_in=w_in, b_gate=b_gate,
                 mla_q_norm_a=mla_q_norm_a, mla_w_qb=mla_w_qb, mla_kv_norm_a=mla_kv_norm_a, mla_w_kvb=mla_w_kvb,
                 mla_q_norm=mla_q_norm, mla_k_norm=mla_k_norm, gla_w_gk2=gla_w_gk2, gla_b_gk=gla_b_gk,
                 gla_o_norm=gla_o_norm, ret_decay=ret_decay, w_branch=w_branch, w_out=w_out, w_ffn_in=w_ffn_in,
                 w_dw=w_dw, b_dw=b_dw, w_ffn_out=w_ffn_out)
    mom_m = dict(c_ctx=m_c_ctx, w_ada=m_w_ada, b_ada=m_b_ada, norm1_w=m_norm1_w, norm2_w=m_norm2_w, w_in=m_w_in,
                 b_gate=m_b_gate, mla_q_norm_a=m_mla_q_norm_a, mla_w_qb=m_mla_w_qb, mla_kv_norm_a=m_mla_kv_norm_a,
                 mla_w_kvb=m_mla_w_kvb, mla_q_norm=m_mla_q_norm, mla_k_norm=m_mla_k_norm, gla_w_gk2=m_gla_w_gk2,
                 gla_b_gk=m_gla_b_gk, gla_o_norm=m_gla_o_norm, ret_decay=m_ret_decay, w_branch=m_w_branch,
                 w_out=m_w_out, w_ffn_in=m_w_ffn_in, w_dw=m_w_dw, b_dw=m_b_dw, w_ffn_out=m_w_ffn_out)
    mom_v = dict(c_ctx=v_c_ctx, w_ada=v_w_ada, b_ada=v_b_ada, norm1_w=v_norm1_w, norm2_w=v_norm2_w, w_in=v_w_in,
                 b_gate=v_b_gate, mla_q_norm_a=v_mla_q_norm_a, mla_w_qb=v_mla_w_qb, mla_kv_norm_a=v_mla_kv_norm_a,
                 mla_w_kvb=v_mla_w_kvb, mla_q_norm=v_mla_q_norm, mla_k_norm=v_mla_k_norm, gla_w_gk2=v_gla_w_gk2,
                 gla_b_gk=v_gla_b_gk, gla_o_norm=v_gla_o_norm, ret_decay=v_ret_decay, w_branch=v_w_branch,
                 w_out=v_w_out, w_ffn_in=v_w_ffn_in, w_dw=v_w_dw, b_dw=v_b_dw, w_ffn_out=v_w_ffn_out)
    axis_of = dict(SHARDED)
    small_narrow = tuple(n for n, _ in SHARDED if n not in BIG and n not in SHARDED_F32)
    small_sharded = small_narrow + SHARDED_F32
    kinds = ('grad', 'delta', 'new_m', 'new_v')
    my_x, my_y, my_c = _my_place()
    my_chip = 2 * my_x + my_y

    gathered = _gather_chips('gather_weights', [local[n].astype(MXU_DTYPE) for n in BIG] + [
        _to_flat([local[n] for n in small_narrow], MXU_DTYPE, 32), _to_flat([local[n] for n in SHARDED_F32], F32, 16)])
    big = []
    for l in range(DEPTH):
        full_l = {n: jnp.concatenate([g[s, l] for s in range(N_CHIPS)], axis=BIG_AXIS[n]) for n, g in zip(BIG, gathered)}
        big.append({k: v.astype(F32) for k, v in _big_layer_weights(full_l).items()})
    small = {n: local[n] for n in REPLICATED}
    for names, flat4 in ((small_narrow, gathered[-2]), (SHARDED_F32, gathered[-1])):
        pieces = [_from_flat(flat4[s], [local[n].shape for n in names]) for s in range(N_CHIPS)]
        for k, n in enumerate(names):
            small[n] = jnp.concatenate([pieces[s][k] for s in range(N_CHIPS)], axis=axis_of[n]).astype(F32)

    loss_local, (grad_big, grad_small, grad_x) = jax.value_and_grad(_local_loss, argnums=(0, 1, 2))(
        big, small, x, c, ctx, loss_target)
    loss = lax.psum(loss_local, ('x', 'y', 'c'))

    per_layer = [_big_layer_grads(grad_big[l]) for l in range(DEPTH)]
    stacks = []
    for n in BIG:
        st = jnp.stack([jnp.stack([_shard_piece(per_layer[l][n], BIG_AXIS[n], s) for l in range(DEPTH)])
                        for s in range(N_CHIPS)])
        stacks.append(st.reshape(N_CHIPS, -1, st.shape[-1]))
    stacks.append(jnp.stack([_to_flat([_shard_piece(grad_small[n], axis_of[n], s) for n in small_sharded], F32, 64)
                             for s in range(N_CHIPS)]))
    labels = BIG + ('small',)
    swapped = _swap_halves('grad_swap_halves', stacks)
    chip_sums = [_add_my_half('grad_add_' + n, st, sw, my_c, MXU_DTYPE) for n, st, sw in zip(labels, stacks, swapped)]
    received = _chip_all_to_all('grad_all_to_all', chip_sums)
    mine = [_sum_own_received('grad_sum_' + n, cs, rc, my_chip) for n, cs, rc in zip(labels, chip_sums, received)]
    other = _sibling_exchange('grad_share_result', mine)

    g_rep = _sum_rows('grad_sum_replicated', _gather_all('grad_gather_replicated',
                                                          _to_flat([grad_small[n] for n in REPLICATED], F32, 8)))

    results = {}
    as_rows = lambda t: t.reshape(-1, t.shape[-1])
    for k, n in enumerate(BIG):
        res = _adamw_halves('adamw_' + n, as_rows(local[n]), mine[k], other[k], as_rows(mom_m[n]), as_rows(mom_v[n]), my_c)
        for kind, val in zip(kinds, res):
            results[kind, n] = val.reshape(local[n].shape)
    flat_small = lambda d: _to_flat([d[n] for n in small_sharded], F32, 64)
    res = _adamw_halves('adamw_small', flat_small(local), mine[-1], other[-1], flat_small(mom_m), flat_small(mom_v), my_c)
    for kind, flat in zip(kinds, res):
        for n, val in zip(small_sharded, _from_flat(flat, [local[n].shape for n in small_sharded])):
            results[kind, n] = val
    flat_rep = lambda d: _to_flat([d[n] for n in REPLICATED], F32, 8)
    upd_r = _adamw('adamw_replicated', flat_rep(local), g_rep, flat_rep(mom_m), flat_rep(mom_v))
    for kind, flat in zip(kinds, (g_rep,) + tuple(upd_r)):
        for n, val in zip(REPLICATED, _from_flat(flat, [local[n].shape for n in REPLICATED])):
            results[kind, n] = val
    out = [loss, grad_x]
    for kind in ('grad', 'delta', 'new_m', 'new_v'):
        out += [results[kind, n] for n in WEIGHT_ORDER]
    return tuple(out)
```

```python
import functools

import jax
import jax.numpy as jnp
from jax import lax
from jax.experimental import pallas as pl
from jax.experimental.pallas import tpu as pltpu

F32 = jnp.float32
MXU_DTYPE = jnp.bfloat16

DEPTH = 2
D_MODEL = 1024
GRID_W = 64
CHUNK = 64
LANES = 128
MLA_HEADS = 8
MLA_NOPE = 64
MLA_ROPE = 32
MLA_QK = MLA_NOPE + MLA_ROPE
MLA_V = 64
GLA_HEADS = 4
GLA_DK = 128
GLA_GATE_NORMALIZER = 16.0
RET_HEADS = 4
RET_DK = 128
D_FF = 2816
ROPE_THETA = 10000.0
RET_THETA = 10000.0
EPS = 1e-6
ADAM_LR = 0.001
ADAM_B1 = 0.9
ADAM_B2 = 0.999
ADAM_EPS = 1e-08
ADAM_WD = 0.01
ADAM_STEP = 10
NEG_BIG = -1e30

VMEM_LIMIT_BYTES = 56 * 1024 * 1024
WEIGHT_BLOCK_BYTES = 8 * 1024 * 1024
ACC_BLOCK_BYTES = 13 * 1024 * 1024
ACC_MAX_ROWS = 2816
MM_ROWS = 512
LOG2E = 1.4426950408889634
LN2 = 0.6931471805599453
MLA_Q_SCALE = MLA_QK ** -0.5 * LOG2E

SHARDED = (('w_ada', 2), ('w_in', 2), ('b_gate', 2), ('mla_w_qb', 2), ('mla_w_kvb', 2), ('gla_w_gk2', 3),
           ('gla_b_gk', 2), ('w_branch', 3), ('w_out', 1), ('w_ffn_in', 2), ('w_dw', 2), ('w_ffn_out', 1))
SHARDED_F32 = ('b_gate', 'gla_b_gk', 'w_dw')
REPLICATED = ('c_ctx', 'b_ada', 'norm1_w', 'norm2_w', 'mla_q_norm_a', 'mla_kv_norm_a', 'mla_q_norm', 'mla_k_norm',
              'gla_o_norm', 'ret_decay', 'b_dw')
WEIGHT_ORDER = ('c_ctx', 'w_ada', 'b_ada', 'norm1_w', 'norm2_w', 'w_in', 'b_gate', 'mla_q_norm_a', 'mla_w_qb',
                'mla_kv_norm_a', 'mla_w_kvb', 'mla_q_norm', 'mla_k_norm', 'gla_w_gk2', 'gla_b_gk', 'gla_o_norm',
                'ret_decay', 'w_branch', 'w_out', 'w_ffn_in', 'w_dw', 'b_dw', 'w_ffn_out')
N_CHIPS = 4
N_DEV = 8
MESH = pl.DeviceIdType.MESH


def _cparams(sem):
    return pltpu.CompilerParams(dimension_semantics=sem, vmem_limit_bytes=VMEM_LIMIT_BYTES)


def _tile(n, target, unit):
    best = None
    for t in range(unit, min(n, target) + 1, unit):
        if n % t == 0:
            best = t
    return n if best is None else best


_DN = {'nn': (((1,), (0,)), ((), ())), 'nt': (((1,), (1,)), ((), ())), 'tn': (((0,), (0,)), ((), ()))}


def _raw_mm(x, y, form):
    return lax.dot_general(x.astype(MXU_DTYPE), y.astype(MXU_DTYPE), _DN[form], preferred_element_type=F32)


@functools.partial(jax.custom_vjp, nondiff_argnums=(2,))
def _mm(x, y, form):
    return _raw_mm(x, y, form)


def _mm_fwd(x, y, form):
    return _raw_mm(x, y, form), (x, y)


def _mm_bwd(form, res, g):
    x, y = res
    if form == 'nn':
        dx, dy = _mm(g, y, 'nt'), _mm(x, g, 'tn')
    elif form == 'nt':
        dx, dy = _mm(g, y, 'nn'), _mm(g, x, 'tn')
    else:
        dx, dy = _mm(y, g, 'nt'), _mm(x, g, 'nn')
    return dx.astype(x.dtype), dy.astype(y.dtype)


_mm.defvjp(_mm_fwd, _mm_bwd)


@functools.partial(jax.custom_vjp, nondiff_argnums=(1, 2))
def _roll(x, shift, axis):
    return pltpu.roll(x, shift, axis)


def _roll_fwd(x, shift, axis):
    return pltpu.roll(x, shift, axis), None


def _roll_bwd(shift, axis, _, g):
    return (pltpu.roll(g, (g.shape[axis] - shift) % g.shape[axis], axis),)


_roll.defvjp(_roll_fwd, _roll_bwd)


def _running_sum(x, reverse):
    n = x.shape[0]
    row = lax.broadcasted_iota(jnp.int32, x.shape, 0)
    d = 1
    while d < n:
        if reverse:
            x = x + jnp.where(row < n - d, pltpu.roll(x, n - d, 0), 0.0)
        else:
            x = x + jnp.where(row >= d, pltpu.roll(x, d, 0), 0.0)
        d *= 2
    return x


@functools.partial(jax.custom_vjp, nondiff_argnums=(1,))
def _cumsum_rows(x, reverse):
    return _running_sum(x, reverse)


def _cumsum_fwd(x, reverse):
    return _running_sum(x, reverse), None


def _cumsum_bwd(reverse, _, g):
    return (_running_sum(g, not reverse),)


_cumsum_rows.defvjp(_cumsum_fwd, _cumsum_bwd)


def _rms(x, n=None):
    n = x.shape[-1] if n is None else n
    return x * lax.rsqrt(jnp.sum(x * x, axis=-1, keepdims=True) / n + EPS)


def _mod_row(i, mod16, b_ada):
    m = mod16[0:8] + b_ada
    return jnp.where(i == 0, m[0:1], m[1:2])


def _row_spec(tb, spec):
    arr, cb, width = spec
    return pl.BlockSpec((tb, width), lambda i, cb=cb: (i, cb))


def _whole_spec(arr):
    nd = arr.ndim
    return pl.BlockSpec(arr.shape, lambda i, nd=nd: (0,) * nd)


def _rw_fwd(name, fn, rows, params, outs, tb):
    t = rows[0][0].shape[0]
    nr, npar = len(rows), len(params)

    def body(*refs):
        i = pl.program_id(0)
        rv = [r[...] for r in refs[:nr]]
        pv = [p[...] for p in refs[nr:nr + npar]]
        res = fn(i, rv, pv)
        for o_ref, val in zip(refs[nr + npar:], res):
            o_ref[...] = val.astype(o_ref.dtype)

    return pl.pallas_call(
        body, name=name, grid=(t // tb,),
        in_specs=[_row_spec(tb, s) for s in rows] + [_whole_spec(p) for p in params],
        out_specs=[pl.BlockSpec((tb, w), lambda i: (i, 0)) for w, _ in outs],
        out_shape=[jax.ShapeDtypeStruct((t, w), dt) for w, dt in outs],
        compiler_params=_cparams(("arbitrary",)),
    )(*[s[0] for s in rows], *params)


def _rw_bwd(name, fn, rows, params, gouts, tb, diff_rows):
    t = rows[0][0].shape[0]
    nr, npar, ng, nd = len(rows), len(params), len(gouts), len(diff_rows)

    def body(*refs):
        i = pl.program_id(0)
        rv = [r[...] for r in refs[:nr]]
        pv = [p[...] for p in refs[nr:nr + npar]]
        gv = [g[...].astype(F32) for g in refs[nr + npar:nr + npar + ng]]
        out_refs = refs[nr + npar + ng:]

        def f(dr, pvals):
            vals = list(rv)
            for k, idx in enumerate(diff_rows):
                vals[idx] = dr[k]
            return tuple(fn(i, vals, pvals))

        _, vjp = jax.vjp(f, [rv[k].astype(F32) for k in diff_rows], pv)
        drows, dpars = vjp(tuple(gv))
        for k in range(nd):
            out_refs[k][...] = drows[k]

        @pl.when(i == 0)
        def _():
            for k in range(npar):
                out_refs[nd + k][...] = jnp.zeros_like(out_refs[nd + k])

        for k in range(npar):
            out_refs[nd + k][...] += dpars[k]

    res = pl.pallas_call(
        body, name=name, grid=(t // tb,),
        in_specs=([_row_spec(tb, s) for s in rows] + [_whole_spec(p) for p in params]
                  + [pl.BlockSpec((tb, g.shape[1]), lambda i: (i, 0)) for g in gouts]),
        out_specs=([pl.BlockSpec((tb, rows[k][2]), lambda i: (i, 0)) for k in diff_rows]
                   + [_whole_spec(p) for p in params]),
        out_shape=([jax.ShapeDtypeStruct((t, rows[k][2]), F32) for k in diff_rows]
                   + [jax.ShapeDtypeStruct(p.shape, F32) for p in params]),
        compiler_params=_cparams(("arbitrary",)),
    )(*[s[0] for s in rows], *params, *gouts)
    return list(res[:nd]), list(res[nd:])


def _full(arr):
    return (arr, 0, arr.shape[1])


def _make_rw_op(fn_factory, n_rows, diff_rows, out_widths):
    @functools.partial(jax.custom_vjp, nondiff_argnums=(0, 1))
    def op(cfg, tb, *args):
        return tuple(_rw_fwd(cfg[0] + '_fwd', fn_factory(cfg), [_full(a) for a in args[:n_rows]], list(args[n_rows:]),
                             [(w, F32) for w in out_widths(cfg, args)], tb))

    def fwd(cfg, tb, *args):
        return op(cfg, tb, *args), args

    def bwd(cfg, tb, args, g):
        drows, dpars = _rw_bwd(cfg[0] + '_bwd', fn_factory(cfg), [_full(a) for a in args[:n_rows]],
                               list(args[n_rows:]), list(g), tb, diff_rows)
        full = [jnp.zeros_like(a) for a in args[:n_rows]]
        for k, idx in enumerate(diff_rows):
            full[idx] = drows[k]
        return tuple(full) + tuple(dpars)

    op.defvjp(fwd, bwd)
    return op


def _silu_fn(cfg):
    return lambda i, rows, params: (jax.nn.silu(rows[0]),)


_silu_op = _make_rw_op(_silu_fn, 1, (0,), lambda cfg, args: (args[0].shape[1],))


def _normmod_fn(cfg):
    _, shift_at, scale_at = cfg

    def fn(i, rows, params):
        (h,) = rows
        nw, mod16, b_ada = params
        mr = _mod_row(i, mod16, b_ada)
        d = h.shape[1]
        return (_rms(h) * nw * (1.0 + mr[:, scale_at * d:(scale_at + 1) * d]) + mr[:, shift_at * d:(shift_at + 1) * d],)

    return fn


_normmod_op = _make_rw_op(_normmod_fn, 1, (0,), lambda cfg, args: (args[0].shape[1],))


def _resid_fn(cfg):
    _, gate_at = cfg

    def fn(i, rows, params):
        h, y = rows
        mod16, b_ada = params
        mr = _mod_row(i, mod16, b_ada)
        d = h.shape[1]
        return (h + mr[:, gate_at * d:(gate_at + 1) * d] * y,)

    return fn


_resid_op = _make_rw_op(_resid_fn, 2, (0, 1), lambda cfg, args: (args[0].shape[1],))


def _resid_norm_fn(cfg):
    _, gate_at, shift_at, scale_at = cfg

    def fn(i, rows, params):
        h, y = rows
        mod_r, b_r, nw, mod_n, b_n = params
        d = h.shape[1]
        h = h + _mod_row(i, mod_r, b_r)[:, gate_at * d:(gate_at + 1) * d] * y
        mn = _mod_row(i, mod_n, b_n)
        return h, _rms(h) * nw * (1.0 + mn[:, scale_at * d:(scale_at + 1) * d]) + mn[:, shift_at * d:(shift_at + 1) * d]

    return fn


_resid_norm_op = _make_rw_op(_resid_norm_fn, 2, (0, 1), lambda cfg, args: (args[0].shape[1],) * 2)


def _merge_fn(cfg):
    def fn(i, rows, params):
        z0, z1, z2, pg = rows
        (bg,) = params
        d = z0.shape[1]
        out = None
        for n, z in enumerate((z0, z1, z2)):
            term = jax.nn.sigmoid(pg[:, n * d:(n + 1) * d] + bg[:, n * d:(n + 1) * d]) * z
            out = term if out is None else out + term
        return (out,)

    return fn


_merge_op = _make_rw_op(_merge_fn, 4, (0, 1, 2, 3), lambda cfg, args: (args[0].shape[1],))


def _matmul(name, a, b, form):
    if form == 'nn':
        (m, k), (_, n) = a.shape, b.shape
        tm = _tile(m, MM_ROWS, 8)
        tn = _tile(n, max(LANES, WEIGHT_BLOCK_BYTES // (k * b.dtype.itemsize)), LANES)

        def body(a_ref, b_ref, o_ref):
            o_ref[...] = _raw_mm(a_ref[...], b_ref[...], 'nn')

        return pl.pallas_call(
            body, name=name, grid=(n // tn, m // tm),
            in_specs=[pl.BlockSpec((tm, k), lambda j, i: (i, 0)), pl.BlockSpec((k, tn), lambda j, i: (0, j))],
            out_specs=pl.BlockSpec((tm, tn), lambda j, i: (i, j)),
            out_shape=jax.ShapeDtypeStruct((m, n), F32), compiler_params=_cparams(("arbitrary", "arbitrary")),
        )(a, b)
    if form == 'nt':
        (m, n), (k, _) = a.shape, b.shape
        tm = _tile(m, MM_ROWS, 8)
        tk = _tile(k, max(LANES, WEIGHT_BLOCK_BYTES // (n * b.dtype.itemsize)), LANES)

        def body(a_ref, b_ref, o_ref):
            o_ref[...] = _raw_mm(a_ref[...], b_ref[...], 'nt')

        return pl.pallas_call(
            body, name=name, grid=(k // tk, m // tm),
            in_specs=[pl.BlockSpec((tm, n), lambda j, i: (i, 0)), pl.BlockSpec((tk, n), lambda j, i: (j, 0))],
            out_specs=pl.BlockSpec((tm, tk), lambda j, i: (i, j)),
            out_shape=jax.ShapeDtypeStruct((m, k), F32), compiler_params=_cparams(("arbitrary", "arbitrary")),
        )(a, b)
    (m, ka), (_, n) = a.shape, b.shape
    tka = _tile(ka, ACC_MAX_ROWS, LANES)
    tn, tmc = _tile(n, max(LANES, ACC_BLOCK_BYTES // (4 * tka)), LANES), _tile(m, MM_ROWS, 8)

    def body(a_ref, b_ref, o_ref):
        @pl.when(pl.program_id(2) == 0)
        def _():
            o_ref[...] = jnp.zeros_like(o_ref)

        o_ref[...] += _raw_mm(a_ref[...], b_ref[...], 'tn')

    return pl.pallas_call(
        body, name=name, grid=(ka // tka, n // tn, m // tmc),
        in_specs=[pl.BlockSpec((tmc, tka), lambda i, j, s: (s, i)), pl.BlockSpec((tmc, tn), lambda i, j, s: (s, j))],
        out_specs=pl.BlockSpec((tka, tn), lambda i, j, s: (i, j)),
        out_shape=jax.ShapeDtypeStruct((ka, n), F32), compiler_params=_cparams(("arbitrary", "arbitrary", "arbitrary")),
    )(a, b)


@functools.partial(jax.custom_vjp, nondiff_argnums=(0,))
def _mm_op(name, a, w):
    return _matmul(name + '_fwd', a, w.astype(MXU_DTYPE), 'nn')


def _mm_op_fwd(name, a, w):
    wb = w.astype(MXU_DTYPE)
    return _matmul(name + '_fwd', a, wb, 'nn'), (a, wb)


def _mm_op_bwd(name, res, g):
    a, wb = res
    return _matmul(name + '_da', g, wb, 'nt'), _matmul(name + '_dw', a, g, 'tn')


_mm_op.defvjp(_mm_op_fwd, _mm_op_bwd)


def _matmul_nt_sum(name, gs, ws):
    m, k, n = gs[0].shape[0], ws[0].shape[0], len(gs)
    tm = _tile(m, 256, 8)
    row_bytes = sum(w.shape[1] * w.dtype.itemsize for w in ws)
    tk = _tile(k, max(LANES, WEIGHT_BLOCK_BYTES // row_bytes), LANES)

    def body(*refs):
        acc = _raw_mm(refs[0][...], refs[n][...], 'nt')
        for p in range(1, n):
            acc = acc + _raw_mm(refs[p][...], refs[n + p][...], 'nt')
        refs[2 * n][...] = acc

    return pl.pallas_call(
        body, name=name, grid=(k // tk, m // tm),
        in_specs=([pl.BlockSpec((tm, g.shape[1]), lambda j, i: (i, 0)) for g in gs]
                  + [pl.BlockSpec((tk, w.shape[1]), lambda j, i: (j, 0)) for w in ws]),
        out_specs=pl.BlockSpec((tm, tk), lambda j, i: (i, j)),
        out_shape=jax.ShapeDtypeStruct((m, k), F32), compiler_params=_cparams(("arbitrary", "arbitrary")),
    )(*gs, *ws)


@functools.partial(jax.custom_vjp, nondiff_argnums=(0,))
def _proj_op(names, a, ws):
    return tuple(_matmul(nm + '_fwd', a, w.astype(MXU_DTYPE), 'nn') for nm, w in zip(names, ws))


def _proj_op_fwd(names, a, ws):
    wbs = tuple(w.astype(MXU_DTYPE) for w in ws)
    return tuple(_matmul(nm + '_fwd', a, wb, 'nn') for nm, wb in zip(names, wbs)), (a, wbs)


def _proj_op_bwd(names, res, gs):
    a, wbs = res
    da = _matmul_nt_sum(names[0] + '_da_all', list(gs), list(wbs))
    return da, tuple(_matmul(nm + '_dw', a, g, 'tn') for nm, g in zip(names, gs))


_proj_op.defvjp(_proj_op_fwd, _proj_op_bwd)


def _rope128(x, c, a, b):
    return x * c + _roll(x, LANES - 8, 1) * a + _roll(x, 8, 1) * b


def _mla_prep_fn(i, rows, params):
    pm, c, a, b = rows
    qna, wqb, kvna, wkn, wv, qn, kn = params
    cq, ckv, kr_slot = pm[:, 0:256], pm[:, 256:384], pm[:, 384:512]
    q_all = _mm(_rms(cq) * qna, wqb, 'nn')
    ckvn = _rms(ckv) * kvna
    k_all = _mm(ckvn, wkn, 'nn')
    v_all = _mm(ckvn, wv, 'nn')
    qs, ks = [], []
    for h in range(MLA_HEADS):
        sl = slice(LANES * h, LANES * (h + 1))
        qs.append(_rope128(_rms(q_all[:, sl], MLA_QK) * qn, c, a, b) * MLA_Q_SCALE)
        ks.append(_rope128(_rms(k_all[:, sl] + kr_slot, MLA_QK) * kn, c, a, b))
    lane = lax.broadcasted_iota(jnp.int32, v_all.shape, 1)
    return jnp.concatenate(qs, axis=1), jnp.concatenate(ks, axis=1), jnp.where(lane % LANES == MLA_V, 1.0, v_all)


def _attn_fwd(name, q, k, v, tb, ctx_len):
    t = q.shape[0]

    def body(q_ref, k_ref, v_ref, o_ref, lse_ref):
        qi = pl.program_id(1)

        def attend(k, v):
            s2 = _raw_mm(q_ref[...], k, 'nt')
            m2 = jnp.max(s2, axis=-1, keepdims=True)
            acc = _raw_mm(jnp.exp2(s2 - m2), v, 'nn')
            l = acc[:, MLA_V:MLA_V + 1]
            o_ref[...] = acc / l
            lse_ref[...] = jnp.broadcast_to((m2 + jnp.log2(l)) * LN2, lse_ref.shape)

        @pl.when(qi == 0)
        def _():
            attend(k_ref[0:ctx_len, :], v_ref[0:ctx_len, :])

        @pl.when(qi != 0)
        def _():
            attend(k_ref[...], v_ref[...])

    blk = pl.BlockSpec((tb, LANES), lambda h, i: (i, h))
    whole = pl.BlockSpec((t, LANES), lambda h, i: (0, h))
    return pl.pallas_call(
        body, name=name, grid=(MLA_HEADS, t // tb), in_specs=[blk, whole, whole], out_specs=[blk, blk],
        out_shape=[jax.ShapeDtypeStruct(q.shape, F32)] * 2, compiler_params=_cparams(("arbitrary", "arbitrary")),
    )(q, k, v)


def _attn_bwd(name, q, k, v, o, lse, do, tb, ctx_len):
    t = q.shape[0]
    ck = _tile(t, 2816, 256)

    def body(q_ref, k_ref, v_ref, o_ref, lse_ref, do_ref, dq_ref, dk_ref, dv_ref):
        qi = pl.program_id(1)

        @pl.when(qi == 0)
        def _():
            dk_ref[...] = jnp.zeros_like(dk_ref)
            dv_ref[...] = jnp.zeros_like(dv_ref)

        q = q_ref[...]
        do = do_ref[...].astype(MXU_DTYPE)
        lse2 = lse_ref[...][:, 0:1] * LOG2E
        delta = jnp.sum(do_ref[...] * o_ref[...], axis=-1, keepdims=True)

        def part(rows):
            ks, vs = k_ref[rows, :], v_ref[rows, :]
            p = jnp.exp2(_raw_mm(q, ks, 'nt') - lse2)
            ds = p * ((_raw_mm(do, vs, 'nt') - delta) * LN2)
            dk_ref[rows, :] += _raw_mm(ds, q, 'tn')
            dv_ref[rows, :] += _raw_mm(p, do, 'tn')
            return _raw_mm(ds, ks, 'nn')

        @pl.when(qi == 0)
        def _():
            dq_ref[...] = part(pl.ds(0, ctx_len))

        @pl.when(qi != 0)
        def _():
            dq = part(pl.ds(0, ck))
            for c in range(1, t // ck):
                dq = dq + part(pl.ds(c * ck, ck))
            dq_ref[...] = dq

    blk = pl.BlockSpec((tb, LANES), lambda h, i: (i, h))
    whole = pl.BlockSpec((t, LANES), lambda h, i: (0, h))
    return pl.pallas_call(
        body, name=name, grid=(MLA_HEADS, t // tb), in_specs=[blk, whole, whole, blk, blk, blk],
        out_specs=[blk, whole, whole], out_shape=[jax.ShapeDtypeStruct(q.shape, F32)] * 3,
        compiler_params=_cparams(("arbitrary", "arbitrary")),
    )(q, k, v, o, lse, do)


def _mla_rows(pm, tabs):
    return [(pm, 0, pm.shape[1])] + [_full(x) for x in tabs]


@functools.partial(jax.custom_vjp, nondiff_argnums=(0, 1))
def _mla_branch(name, tb, pm, tabs, params):
    return _mla_branch_fwd(name, tb, pm, tabs, params)[0]


def _mla_branch_fwd(name, tb, pm, tabs, params):
    w = LANES * MLA_HEADS
    q, k, v = _rw_fwd(name + '_prep', _mla_prep_fn, _mla_rows(pm, tabs), list(params), [(w, MXU_DTYPE)] * 3, tb)
    o, lse = _attn_fwd(name + '_attn', q, k, v, tb, tb)
    return o, (pm, tabs, params, q, k, v, o, lse)


def _mla_branch_bwd(name, tb, res, do):
    pm, tabs, params, q, k, v, o, lse = res
    dq, dk, dv = _attn_bwd(name + '_attn_bwd', q, k, v, o, lse, do, tb, tb)
    (dpm,), dpars = _rw_bwd(name + '_prep_bwd', _mla_prep_fn, _mla_rows(pm, tabs), list(params), [dq, dk, dv], tb, (0,))
    return dpm, tuple(jnp.zeros_like(x) for x in tabs), tuple(dpars)


_mla_branch.defvjp(_mla_branch_fwd, _mla_branch_bwd)


def _chunk_masks(reverse):
    i = lax.broadcasted_iota(jnp.int32, (CHUNK, CHUNK), 0)
    j = lax.broadcasted_iota(jnp.int32, (CHUNK, CHUNK), 1)
    return i, j, ((j > i) if reverse else (j <= i))


def _gla_chunk(reverse, rows, params, st0):
    q, k, v, la = rows
    q = q * (GLA_DK ** -0.5)
    cum = _cumsum_rows(la, reverse)
    tot = cum[0:1] if reverse else cum[CHUNK - 1:CHUNK]
    vt = v.T
    st1 = st0 * jnp.exp(tot) + _mm(vt, k * jnp.exp(tot - cum), 'nn')
    qd = q * jnp.exp(cum)
    _, _, mask = _chunk_masks(reverse)
    att = jnp.where(mask, _mm(qd, k * jnp.exp(-cum), 'nt'), 0.0)
    return _mm(jnp.concatenate([qd, att], axis=1), jnp.concatenate([st0, vt], axis=1), 'nt'), st1


def _ret_chunk(reverse, rows, params, st0):
    q, k, v, cc, ss = rows
    (lg,) = params
    q = q * cc + _roll(q, RET_DK // 2, 1) * ss
    k = (k * cc + _roll(k, RET_DK // 2, 1) * ss) * (RET_DK ** -0.5)
    r = lax.broadcasted_iota(jnp.int32, (CHUNK, LANES), 0).astype(F32)
    zeta = jnp.exp((r if reverse else (CHUNK - 1.0 - r)) * lg)
    xi = jnp.exp(((CHUNK - r) if reverse else (r + 1.0)) * lg)
    vt = v.T
    st1 = st0 * jnp.exp(CHUNK * lg) + _mm(vt, k * zeta, 'nn')
    i, j, mask = _chunk_masks(reverse)
    rel = jnp.where(mask, (j - i) if reverse else (i - j), 0).astype(F32)
    dmat = jnp.where(mask, jnp.exp(rel * lg[:, 0:CHUNK]), 0.0)
    att = _mm(q, k, 'nt') * dmat
    return _mm(jnp.concatenate([q * xi, att], axis=1), jnp.concatenate([st0, vt], axis=1), 'nt'), st1


def _scan_order(reverse, nblk):
    if reverse:
        return lambda t: jnp.where(t == 0, 0, nblk - t)
    return lambda t: t


def _scan_specs(rows, params, tb, heads, blk_of):
    def rspec(spec):
        _, cb = spec
        if cb is None:
            return pl.BlockSpec((tb, LANES), lambda s: (blk_of(s), 0))
        return pl.BlockSpec((tb, heads * LANES), lambda s, cb=cb: (blk_of(s), cb // heads))

    return [rspec(s) for s in rows] + [pl.BlockSpec((heads, 1, LANES), lambda s: (0, 0, 0)) for _ in params]


def _head_rows(row_refs, rows, sl, h):
    lanes = pl.ds(h * LANES, LANES)
    return [r[sl, :] if spec[1] is None else r[sl, lanes] for r, spec in zip(row_refs, rows)]


def _scan_fwd(name, chunk_fn, reverse, rows, params, tb, heads):
    t = rows[0][0].shape[0]
    nblk, cpb = t // tb, tb // CHUNK
    blk_of = _scan_order(reverse, nblk)
    nr, npar = len(rows), len(params)
    order = list(range(cpb))[::-1] if reverse else list(range(cpb))

    def body(*refs):
        row_refs, par_refs = refs[:nr], refs[nr:nr + npar]
        o_ref, st_out_ref, st_ref = refs[nr + npar:]

        @pl.when(pl.program_id(0) == 0)
        def _():
            st_ref[...] = jnp.zeros_like(st_ref)

        for c in order:
            sl = pl.ds(c * CHUNK, CHUNK)
            for h in range(heads):
                st0 = st_ref[h]
                st_out_ref[h, c] = st0
                o, st1 = chunk_fn(reverse, _head_rows(row_refs, rows, sl, h), [p[h] for p in par_refs], st0)
                o_ref[sl, pl.ds(h * LANES, LANES)] = o
                st_ref[h] = st1

    return pl.pallas_call(
        body, name=name, grid=(nblk,), in_specs=_scan_specs(rows, params, tb, heads, blk_of),
        out_specs=[pl.BlockSpec((tb, heads * LANES), lambda s: (blk_of(s), 0)),
                   pl.BlockSpec((heads, cpb, LANES, LANES), lambda s: (0, blk_of(s), 0, 0))],
        out_shape=[jax.ShapeDtypeStruct((t, heads * LANES), F32),
                   jax.ShapeDtypeStruct((heads, t // CHUNK, LANES, LANES), F32)],
        scratch_shapes=[pltpu.VMEM((heads, LANES, LANES), F32)],
        compiler_params=_cparams(("arbitrary",)),
    )(*[s[0] for s in rows], *params)


def _scan_bwd(name, chunk_fn, reverse, rows, params, states, do, tb, heads, n_diff):
    t = rows[0][0].shape[0]
    nblk, cpb = t // tb, tb // CHUNK
    fwd_blk = _scan_order(reverse, nblk)
    blk_of = lambda s: fwd_blk(nblk - 1 - s)
    nr, npar = len(rows), len(params)
    order = list(range(cpb)) if reverse else list(range(cpb))[::-1]

    def body(*refs):
        row_refs, par_refs = refs[:nr], refs[nr:nr + npar]
        st_in_ref, do_ref = refs[nr + npar:nr + npar + 2]
        out_refs = refs[nr + npar + 2:-1]
        dst_ref = refs[-1]

        @pl.when(pl.program_id(0) == 0)
        def _():
            dst_ref[...] = jnp.zeros_like(dst_ref)
            for k in range(npar):
                out_refs[n_diff + k][...] = jnp.zeros_like(out_refs[n_diff + k])

        for c in order:
            sl = pl.ds(c * CHUNK, CHUNK)
            for h in range(heads):
                lanes = pl.ds(h * LANES, LANES)
                rv = _head_rows(row_refs, rows, sl, h)

                def f(dr, pvals, st0, rv=rv):
                    return chunk_fn(reverse, list(dr) + rv[n_diff:], pvals, st0)

                _, vjp = jax.vjp(f, rv[:n_diff], [p[h] for p in par_refs], st_in_ref[h, c])
                drows, dpars, dst0 = vjp((do_ref[sl, lanes], dst_ref[h]))
                for k in range(n_diff):
                    out_refs[k][sl, lanes] = drows[k]
                for k in range(npar):
                    out_refs[n_diff + k][h] += dpars[k]
                dst_ref[h] = dst0

    wide = pl.BlockSpec((tb, heads * LANES), lambda s: (blk_of(s), 0))
    pblk = pl.BlockSpec((heads, 1, LANES), lambda s: (0, 0, 0))
    res = pl.pallas_call(
        body, name=name, grid=(nblk,),
        in_specs=(_scan_specs(rows, params, tb, heads, blk_of)
                  + [pl.BlockSpec((heads, cpb, LANES, LANES), lambda s: (0, blk_of(s), 0, 0)), wide]),
        out_specs=[wide] * n_diff + [pblk for _ in params],
        out_shape=([jax.ShapeDtypeStruct((t, heads * LANES), F32)] * n_diff
                   + [jax.ShapeDtypeStruct(p.shape, F32) for p in params]),
        scratch_shapes=[pltpu.VMEM((heads, LANES, LANES), F32)],
        compiler_params=_cparams(("arbitrary",)),
    )(*[s[0] for s in rows], *params, states, do)
    return list(res[:n_diff]), list(res[n_diff:])


def _gla_la_fn(i, rows, params):
    (r,) = rows
    w2f, w2b, bgk = params
    la_f = jax.nn.log_sigmoid(_mm(r, w2f, 'nn') + bgk[0:1]) / GLA_GATE_NORMALIZER
    la_b = jax.nn.log_sigmoid(_mm(r, w2b, 'nn') + bgk[1:2]) / GLA_GATE_NORMALIZER
    return la_f, la_b


def _headnorm_fn(heads, with_weight):
    def fn(i, rows, params):
        o_f, o_b, g = rows
        outs = []
        for h in range(heads):
            sl = slice(LANES * h, LANES * (h + 1))
            y = _rms(o_f[:, sl] + o_b[:, sl])
            outs.append(y * params[0] if with_weight else y)
        return (jnp.concatenate(outs, axis=1) * jax.nn.silu(g),)

    return fn


@functools.partial(jax.custom_vjp, nondiff_argnums=(0, 1))
def _gla_branch(name, tb, pg, params):
    return _gla_branch_fwd(name, tb, pg, params)[0]


def _gla_rows(pg, la):
    return [(pg, 0), (pg, GLA_HEADS), (pg, 2 * GLA_HEADS), (la, 0)]


def _gla_branch_fwd(name, tb, pg, params):
    w2f, w2b, bgk, onorm = params
    w = GLA_HEADS * LANES
    la_f, la_b = _rw_fwd(name + '_la', _gla_la_fn, [(pg, 4 * w // LANES, LANES)], [w2f, w2b, bgk], [(w, F32)] * 2, tb)
    o_f, st_f = _scan_fwd(name + '_scan_f', _gla_chunk, False, _gla_rows(pg, la_f), [], tb, GLA_HEADS)
    o_b, st_b = _scan_fwd(name + '_scan_b', _gla_chunk, True, _gla_rows(pg, la_b), [], tb, GLA_HEADS)
    (y,) = _rw_fwd(name + '_norm', _headnorm_fn(GLA_HEADS, True), [_full(o_f), _full(o_b), (pg, 3, w)], [onorm],
                   [(w, F32)], tb)
    return y, (pg, params, la_f, la_b, o_f, o_b, st_f, st_b)


def _gla_branch_bwd(name, tb, res, dy):
    pg, params, la_f, la_b, o_f, o_b, st_f, st_b = res
    w2f, w2b, bgk, onorm = params
    w = GLA_HEADS * LANES
    (do_f, do_b, dg), (donorm,) = _rw_bwd(name + '_norm_bwd', _headnorm_fn(GLA_HEADS, True),
                                          [_full(o_f), _full(o_b), (pg, 3, w)], [onorm], [dy], tb, (0, 1, 2))
    (dq_f, dk_f, dv_f, dla_f), _ = _scan_bwd(name + '_scan_f_bwd', _gla_chunk, False, _gla_rows(pg, la_f), [], st_f,
                                             do_f, tb, GLA_HEADS, 4)
    (dq_b, dk_b, dv_b, dla_b), _ = _scan_bwd(name + '_scan_b_bwd', _gla_chunk, True, _gla_rows(pg, la_b), [], st_b,
                                             do_b, tb, GLA_HEADS, 4)
    (dr,), (dw2f, dw2b, dbgk) = _rw_bwd(name + '_la_bwd', _gla_la_fn, [(pg, 4 * w // LANES, LANES)], [w2f, w2b, bgk],
                                        [dla_f, dla_b], tb, (0,))
    dpg = jnp.concatenate([dq_f + dq_b, dk_f + dk_b, dv_f + dv_b, dg, dr], axis=1)
    return dpg, (dw2f, dw2b, dbgk, donorm)


_gla_branch.defvjp(_gla_branch_fwd, _gla_branch_bwd)


@functools.partial(jax.custom_vjp, nondiff_argnums=(0, 1))
def _ret_branch(name, tb, pr, tabs, lg):
    return _ret_branch_fwd(name, tb, pr, tabs, lg)[0]


def _ret_rows(pr, tabs):
    return [(pr, 0), (pr, RET_HEADS), (pr, 2 * RET_HEADS), (tabs[0], None), (tabs[1], None)]


def _ret_branch_fwd(name, tb, pr, tabs, lg):
    w = RET_HEADS * LANES
    o_f, st_f = _scan_fwd(name + '_scan_f', _ret_chunk, False, _ret_rows(pr, tabs), [lg[0]], tb, RET_HEADS)
    o_b, st_b = _scan_fwd(name + '_scan_b', _ret_chunk, True, _ret_rows(pr, tabs), [lg[1]], tb, RET_HEADS)
    (y,) = _rw_fwd(name + '_norm', _headnorm_fn(RET_HEADS, False), [_full(o_f), _full(o_b), (pr, 3, w)], [],
                   [(w, F32)], tb)
    return y, (pr, tabs, lg, o_f, o_b, st_f, st_b)


def _ret_branch_bwd(name, tb, res, dy):
    pr, tabs, lg, o_f, o_b, st_f, st_b = res
    w = RET_HEADS * LANES
    (do_f, do_b, dg), _ = _rw_bwd(name + '_norm_bwd', _headnorm_fn(RET_HEADS, False),
                                  [_full(o_f), _full(o_b), (pr, 3, w)], [], [dy], tb, (0, 1, 2))
    (dq_f, dk_f, dv_f), (dlg_f,) = _scan_bwd(name + '_scan_f_bwd', _ret_chunk, False, _ret_rows(pr, tabs), [lg[0]],
                                             st_f, do_f, tb, RET_HEADS, 3)
    (dq_b, dk_b, dv_b), (dlg_b,) = _scan_bwd(name + '_scan_b_bwd', _ret_chunk, True, _ret_rows(pr, tabs), [lg[1]],
                                             st_b, do_b, tb, RET_HEADS, 3)
    dpr = jnp.concatenate([dq_f + dq_b, dk_f + dk_b, dv_f + dv_b, dg], axis=1)
    return dpr, tuple(jnp.zeros_like(x) for x in tabs), jnp.stack([dlg_f, dlg_b])


_ret_branch.defvjp(_ret_branch_fwd, _ret_branch_bwd)


HALO = 8


def _halo_specs(tb, nblk, width, col_block):
    r = tb // HALO
    prev = pl.BlockSpec((HALO, width), lambda i: (jnp.maximum(i * r - 1, 0), col_block))
    nxt = pl.BlockSpec((HALO, width), lambda i: (jnp.minimum((i + 1) * r, nblk * r - 1), col_block))
    return prev, nxt


def _shifted(x, prev_blk, next_blk, i, nblk):
    tb = x.shape[0]
    row = lax.broadcasted_iota(jnp.int32, x.shape, 0)
    prev_row = jnp.where(i >= 2, prev_blk[HALO - 1:HALO], 0.0)
    next_row = jnp.where((i >= 1) & (i < nblk - 1), next_blk[0:1], 0.0)
    down = jnp.where(row == 0, prev_row, pltpu.roll(x, 1, 0))
    up = jnp.where(row == tb - 1, next_row, pltpu.roll(x, tb - 1, 0))
    return down, up


def _gelu_up(c, up):
    return jax.nn.gelu(c) * up


def _convact_fwd_call(name, gu, w_dw, b_dw, tb):
    t = gu.shape[0]
    nblk = t // tb
    prev_spec, next_spec = _halo_specs(tb, nblk, D_FF, 0)

    def body(g_ref, up_ref, prev_ref, next_ref, w_ref, b_ref, o_ref):
        i = pl.program_id(0)
        g = g_ref[...]
        down, upw = _shifted(g, prev_ref[...], next_ref[...], i, nblk)
        w = w_ref[...]
        c = w[0:1] * down + w[1:2] * g + w[2:3] * upw + b_ref[...]
        o_ref[...] = _gelu_up(c, up_ref[...])

    return pl.pallas_call(
        body, name=name, grid=(nblk,),
        in_specs=[pl.BlockSpec((tb, D_FF), lambda i: (i, 0)), pl.BlockSpec((tb, D_FF), lambda i: (i, 1)), prev_spec,
                  next_spec, _whole_spec(w_dw), _whole_spec(b_dw)],
        out_specs=pl.BlockSpec((tb, D_FF), lambda i: (i, 0)), out_shape=jax.ShapeDtypeStruct((t, D_FF), F32),
        compiler_params=_cparams(("arbitrary",)),
    )(gu, gu, gu, gu, w_dw, b_dw)


def _convact_bwd_calls(name, gu, w_dw, b_dw, dact, tb):
    t = gu.shape[0]
    nblk = t // tb
    prev_spec, next_spec = _halo_specs(tb, nblk, D_FF, 0)

    def body1(g_ref, up_ref, prev_ref, next_ref, w_ref, b_ref, da_ref, dc_ref, dup_ref, dw_ref, db_ref):
        i = pl.program_id(0)
        g = g_ref[...]
        down, upw = _shifted(g, prev_ref[...], next_ref[...], i, nblk)
        w = w_ref[...]
        c = w[0:1] * down + w[1:2] * g + w[2:3] * upw + b_ref[...]
        _, vjp = jax.vjp(_gelu_up, c, up_ref[...])
        dc, dup = vjp(da_ref[...])
        dc_ref[...] = dc
        dup_ref[...] = dup

        @pl.when(i == 0)
        def _():
            dw_ref[...] = jnp.zeros_like(dw_ref)
            db_ref[...] = jnp.zeros_like(db_ref)

        dw_ref[0:1, :] += jnp.sum(dc * down, axis=0, keepdims=True)
        dw_ref[1:2, :] += jnp.sum(dc * g, axis=0, keepdims=True)
        dw_ref[2:3, :] += jnp.sum(dc * upw, axis=0, keepdims=True)
        db_ref[...] += jnp.sum(dc, axis=0, keepdims=True)

    blk = pl.BlockSpec((tb, D_FF), lambda i: (i, 0))
    dc, dup, dw, db = pl.pallas_call(
        body1, name=name + '_a', grid=(nblk,),
        in_specs=[blk, pl.BlockSpec((tb, D_FF), lambda i: (i, 1)), prev_spec, next_spec, _whole_spec(w_dw),
                  _whole_spec(b_dw), blk],
        out_specs=[blk, blk, _whole_spec(w_dw), _whole_spec(b_dw)],
        out_shape=[jax.ShapeDtypeStruct((t, D_FF), F32)] * 2 + [jax.ShapeDtypeStruct(w_dw.shape, F32),
                                                                jax.ShapeDtypeStruct(b_dw.shape, F32)],
        compiler_params=_cparams(("arbitrary",)),
    )(gu, gu, gu, gu, w_dw, b_dw, dact)

    def body2(dc_ref, prev_ref, next_ref, dup_ref, w_ref, o_ref):
        i = pl.program_id(0)
        dc_blk = dc_ref[...]
        down, upw = _shifted(dc_blk, prev_ref[...], next_ref[...], i, nblk)
        w = w_ref[...]
        o_ref[:, 0:D_FF] = w[0:1] * upw + w[1:2] * dc_blk + w[2:3] * down
        o_ref[:, D_FF:2 * D_FF] = dup_ref[...]

    dgu = pl.pallas_call(
        body2, name=name + '_b', grid=(nblk,),
        in_specs=[blk, prev_spec, next_spec, blk, _whole_spec(w_dw)],
        out_specs=pl.BlockSpec((tb, 2 * D_FF), lambda i: (i, 0)), out_shape=jax.ShapeDtypeStruct((t, 2 * D_FF), F32),
        compiler_params=_cparams(("arbitrary",)),
    )(dc, dc, dc, dup, w_dw)
    return dgu, dw, db


@functools.partial(jax.custom_vjp, nondiff_argnums=(0, 1))
def _convact(name, tb, gu, w_dw, b_dw):
    return _convact_fwd_call(name + '_fwd', gu, w_dw, b_dw, tb)


def _convact_fwd(name, tb, gu, w_dw, b_dw):
    return _convact_fwd_call(name + '_fwd', gu, w_dw, b_dw, tb), (gu, w_dw, b_dw)


def _convact_bwd(name, tb, res, dact):
    gu, w_dw, b_dw = res
    return _convact_bwd_calls(name + '_bwd', gu, w_dw, b_dw, dact, tb)


_convact.defvjp(_convact_fwd, _convact_bwd)


def _loss_fwd_call(h, target, tb):
    nlat = target.shape[0] // tb

    def body(h_ref, t_ref, o_ref):
        @pl.when(pl.program_id(0) == 0)
        def _():
            o_ref[...] = jnp.zeros_like(o_ref)

        e = h_ref[...] - t_ref[...]
        o_ref[...] += jnp.sum(e * e, axis=0, keepdims=True)

    cols = pl.pallas_call(
        body, name='loss_fwd', grid=(nlat,),
        in_specs=[pl.BlockSpec((tb, D_MODEL), lambda i: (i + 1, 0)), pl.BlockSpec((tb, D_MODEL), lambda i: (i, 0))],
        out_specs=pl.BlockSpec((1, D_MODEL), lambda i: (0, 0)), out_shape=jax.ShapeDtypeStruct((1, D_MODEL), F32),
        compiler_params=_cparams(("arbitrary",)),
    )(h, target)
    return (0.5 / D_MODEL) * jnp.sum(cols)


def _loss_bwd_call(h, target, gbar, tb):
    def body(g_ref, h_ref, t_ref, o_ref):
        live = jnp.where(pl.program_id(0) == 0, 0.0, g_ref[...] * (1.0 / D_MODEL))
        o_ref[...] = live * (h_ref[...] - t_ref[...])

    return pl.pallas_call(
        body, name='loss_bwd', grid=(h.shape[0] // tb,),
        in_specs=[pl.BlockSpec((1, 1), lambda i: (0, 0)), pl.BlockSpec((tb, D_MODEL), lambda i: (i, 0)),
                  pl.BlockSpec((tb, D_MODEL), lambda i: (jnp.maximum(i - 1, 0), 0))],
        out_specs=pl.BlockSpec((tb, D_MODEL), lambda i: (i, 0)), out_shape=jax.ShapeDtypeStruct(h.shape, F32),
        compiler_params=_cparams(("arbitrary",)),
    )(gbar.reshape(1, 1), h, target)


@functools.partial(jax.custom_vjp, nondiff_argnums=(0,))
def _loss_op(tb, h, target):
    return _loss_fwd_call(h, target, tb)


def _loss_op_fwd(tb, h, target):
    return _loss_fwd_call(h, target, tb), (h, target)


def _loss_op_bwd(tb, res, gbar):
    h, target = res
    return _loss_bwd_call(h, target, gbar, tb), jnp.zeros_like(target)


_loss_op.defvjp(_loss_op_fwd, _loss_op_bwd)


def _rope_tables(pos, dim, theta):
    inv = theta ** (-jnp.arange(dim // 2, dtype=F32) * 2.0 / dim)
    ang = pos.astype(F32)[:, None] * inv[None, :]
    return jnp.cos(ang), jnp.sin(ang)


def _mla_tables(seq, ctx_len):
    rows = seq // GRID_W
    row_pos = jnp.repeat(jnp.arange(rows), GRID_W)
    col_pos = jnp.tile(jnp.arange(GRID_W), rows)
    cos_r, sin_r = _rope_tables(row_pos, MLA_ROPE // 2, ROPE_THETA)
    cos_c, sin_c = _rope_tables(col_pos, MLA_ROPE // 2, ROPE_THETA)
    one = jnp.ones((seq, MLA_NOPE), F32)
    z8 = jnp.zeros((seq, 8), F32)
    pad1 = jnp.ones((seq, LANES - MLA_QK), F32)
    pad0 = jnp.zeros((seq, LANES - MLA_QK), F32)
    z64 = jnp.zeros((seq, MLA_NOPE), F32)
    c = jnp.concatenate([one, cos_r, cos_r, cos_c, cos_c, pad1], axis=1)
    a = jnp.concatenate([z64, -sin_r, z8, -sin_c, z8, pad0], axis=1)
    b = jnp.concatenate([z64, z8, sin_r, z8, sin_c, pad0], axis=1)
    ctx_rows = lambda fill: jnp.full((ctx_len, LANES), fill, F32)
    return (jnp.concatenate([ctx_rows(1.0), c]), jnp.concatenate([ctx_rows(0.0), a]),
            jnp.concatenate([ctx_rows(0.0), b]))


def _ret_tables(total):
    inv = 1.0 / (RET_THETA ** jnp.linspace(0.0, 1.0, RET_DK // 2, dtype=F32))
    ang = jnp.arange(total).astype(F32)[:, None] * inv[None, :]
    cos, sin = jnp.cos(ang), jnp.sin(ang)
    return jnp.concatenate([cos, cos], axis=1), jnp.concatenate([-sin, sin], axis=1)


def _head_slots(w, heads, width):
    k = w.shape[0]
    return jnp.pad(w.reshape(k, heads, width), ((0, 0), (0, 0), (0, LANES - width))).reshape(k, heads * LANES)


def _pad_lanes(v, width):
    return jnp.pad(v, (0, LANES - width)).reshape(1, LANES)


IN_MLA_QKV = (0, 384)
IN_MLA_KR = (384, 416)
IN_GLA = (416, 2496)
IN_RET = (2496, 4544)
IN_GATES = (4544, 7616)
W_MLA_COLS = 512
W_GLA_COLS = 2176
BIG = ('w_ada', 'w_in', 'w_branch', 'w_out', 'w_ffn_in', 'w_ffn_out')
BIG_AXIS = dict(w_ada=1, w_in=1, w_branch=2, w_out=0, w_ffn_in=1, w_ffn_out=0)


def _big_layer_weights(full):
    w_in = full['w_in']
    zc = lambda n: jnp.zeros((D_MODEL, n), w_in.dtype)
    wbr = full['w_branch']
    wb_mla = jnp.pad(wbr[0].reshape(MLA_HEADS, MLA_V, D_MODEL), ((0, 0), (0, LANES - MLA_V), (0, 0)))
    return dict(
        ada=full['w_ada'],
        in_mla=jnp.concatenate([w_in[:, slice(*IN_MLA_QKV)], zc(MLA_NOPE), w_in[:, slice(*IN_MLA_KR)],
                                zc(LANES - MLA_QK)], axis=1),
        in_gla=jnp.concatenate([w_in[:, slice(*IN_GLA)], zc(W_GLA_COLS - (IN_GLA[1] - IN_GLA[0]))], axis=1),
        in_ret=w_in[:, slice(*IN_RET)], in_gate=w_in[:, slice(*IN_GATES)],
        br_mla=wb_mla.reshape(MLA_HEADS * LANES, D_MODEL), br_gla=wbr[1], br_ret=wbr[2],
        out=full['w_out'], ffn_in=full['w_ffn_in'], ffn_out=full['w_ffn_out'])


def _big_layer_grads(g):
    n_gla = IN_GLA[1] - IN_GLA[0]
    kr_at = IN_MLA_QKV[1] + MLA_NOPE
    w_in = jnp.concatenate([g['in_mla'][:, slice(*IN_MLA_QKV)], g['in_mla'][:, kr_at:kr_at + MLA_ROPE],
                            g['in_gla'][:, :n_gla], g['in_ret'], g['in_gate']], axis=1)
    br_mla = g['br_mla'].reshape(MLA_HEADS, LANES, D_MODEL)[:, :MLA_V].reshape(MLA_HEADS * MLA_V, D_MODEL)
    return dict(w_ada=g['ada'], w_in=w_in, w_branch=jnp.stack([br_mla, g['br_gla'], g['br_ret']]),
                w_out=g['out'], w_ffn_in=g['ffn_in'], w_ffn_out=g['ffn_out'])


def _row(v):
    return v.reshape(1, -1)


def _mixers(l, big, w, a, tabs_mla, tabs_ret, tb):
    nm = 'l%d_' % l
    row = _row
    pm, pg, pr, pgate = _proj_op(tuple(nm + s for s in ('in_mla', 'in_gla', 'in_ret', 'in_gate')), a,
                                 (big['in_mla'], big['in_gla'], big['in_ret'], big['in_gate']))

    kvb = w['mla_w_kvb'][l].reshape(-1, MLA_HEADS, MLA_NOPE + MLA_V)
    kdim = kvb.shape[0]
    mla_params = (row(w['mla_q_norm_a'][l]), _head_slots(w['mla_w_qb'][l], MLA_HEADS, MLA_QK),
                  row(w['mla_kv_norm_a'][l]), _head_slots(kvb[:, :, :MLA_NOPE].reshape(kdim, -1), MLA_HEADS, MLA_NOPE),
                  _head_slots(kvb[:, :, MLA_NOPE:].reshape(kdim, -1), MLA_HEADS, MLA_V),
                  _pad_lanes(w['mla_q_norm'][l], MLA_QK), _pad_lanes(w['mla_k_norm'][l], MLA_QK))
    y_mla = _mla_branch(nm + 'mla', tb, pm, tabs_mla, mla_params)

    gk2 = w['gla_w_gk2'][l]
    rank = gk2.shape[1]
    w2f = jnp.pad(gk2[0], ((0, LANES - rank), (0, 0)))
    w2b = jnp.pad(gk2[1], ((rank, LANES - 2 * rank), (0, 0)))
    y_gla = _gla_branch(nm + 'gla', tb, pg, (w2f, w2b, w['gla_b_gk'][l], row(w['gla_o_norm'][l])))

    log_g = -jnp.exp(w['ret_decay'][l])
    lg = jnp.broadcast_to(log_g[:, :, None, None], (2, RET_HEADS, 1, LANES))
    y_ret = _ret_branch(nm + 'ret', tb, pr, tabs_ret, lg)

    z0 = _mm_op(nm + 'br_mla', y_mla, big['br_mla'])
    z1 = _mm_op(nm + 'br_gla', y_gla, big['br_gla'])
    z2 = _mm_op(nm + 'br_ret', y_ret, big['br_ret'])
    u = _merge_op((nm + 'merge',), tb, z0, z1, z2, pgate, row(w['b_gate'][l]))[0]
    return _mm_op(nm + 'out', u, big['out'])


def _ffn(l, big, w, a2, tb):
    nm = 'l%d_' % l
    gu = _mm_op(nm + 'ffn_in', a2, big['ffn_in'])
    act = _convact(nm + 'convact', tb, gu, w['w_dw'][l], _row(w['b_dw'][l]))
    return _mm_op(nm + 'ffn_out', act, big['ffn_out'])


def _local_loss(big, w, x, c, ctx, target):
    seq, tb = x.shape[1], ctx.shape[1]
    h = jnp.concatenate([ctx[0], x[0]], axis=0)
    cond_in = jnp.concatenate([w['c_ctx'].reshape(1, -1), c, jnp.zeros((14, D_MODEL), F32)], axis=0)
    cond16 = _silu_op(('cond_silu',), 16, cond_in)[0]
    tabs_mla = _mla_tables(seq, tb)
    tabs_ret = _ret_tables(seq + tb)
    mods = [_mm_op('l%d_ada' % l, cond16, big[l]['ada']) for l in range(DEPTH)]
    b_ada = [_row(w['b_ada'][l]) for l in range(DEPTH)]
    a = _normmod_op(('l0_norm1', 0, 1), tb, h, _row(w['norm1_w'][0]), mods[0], b_ada[0])[0]
    for l in range(DEPTH):
        nm = 'l%d_' % l
        y = _mixers(l, big[l], w, a, tabs_mla, tabs_ret, tb)
        h, a2 = _resid_norm_op((nm + 'res1_norm2', 2, 3, 4), tb, h, y, mods[l], b_ada[l], _row(w['norm2_w'][l]),
                               mods[l], b_ada[l])
        f = _ffn(l, big[l], w, a2, tb)
        if l + 1 < DEPTH:
            h, a = _resid_norm_op((nm + 'res2_norm1', 5, 0, 1), tb, h, f, mods[l], b_ada[l],
                                  _row(w['norm1_w'][l + 1]), mods[l + 1], b_ada[l + 1])
        else:
            h = _resid_op((nm + 'res2', 5), tb, h, f, mods[l], b_ada[l])[0]
    return _loss_op(tb, h, target[0])


ANY = pl.BlockSpec(memory_space=pl.ANY)
FLAT_W = 1024


def _my_place():
    return lax.axis_index('x'), lax.axis_index('y'), lax.axis_index('c')


def _other_chips(x, y):
    return [(1 - x, y), (x, 1 - y), (1 - x, 1 - y)]


def _gather_chips(name, arrs):
    n = len(arrs)

    def body(*refs):
        ins, outs = refs[:n], refs[n:2 * n]
        send_sems, recv_sems, pass_send_sems, pass_recv_sems, own_send_sems, own_recv_sems = refs[2 * n:]
        x, y, c = _my_place()
        me = 2 * x + y
        chips = _other_chips(x, y)

        def half(a, which):
            h = arrs[a].shape[0] // 2
            return pl.ds(which * h, h)

        def ici(j, a, chip_slot, to):
            src = ins[a].at[half(a, c)] if chip_slot is None else outs[a].at[chip_slot, half(a, c)]
            return pltpu.make_async_remote_copy(
                src_ref=src, dst_ref=outs[a].at[me if chip_slot is None else chip_slot, half(a, c)],
                send_sem=send_sems.at[j, a], recv_sem=recv_sems.at[j, a], device_id=to, device_id_type=MESH)

        def passed(j, a, chip_slot, which):
            rows = outs[a].at[chip_slot, half(a, which)]
            return pltpu.make_async_remote_copy(src_ref=rows, dst_ref=rows, send_sem=pass_send_sems.at[j, a],
                                                recv_sem=pass_recv_sems.at[j, a], device_id=(x, y, 1 - c),
                                                device_id_type=MESH)

        def own(a):
            return pltpu.make_async_remote_copy(src_ref=ins[a], dst_ref=outs[a].at[me], send_sem=own_send_sems.at[a],
                                                recv_sem=own_recv_sems.at[a], device_id=(x, y, 1 - c),
                                                device_id_type=MESH)

        copies = [own(a) for a in range(n)]
        sends = [ici(j, a, None, (px, py, c)) for j, (px, py) in enumerate(chips) for a in range(n)]
        for cp in copies + sends:
            cp.start()
        passes = []
        for j, (px, py) in enumerate(chips):
            for a in range(n):
                ici(j, a, 2 * px + py, (px, py, c)).wait_recv()
                p = passed(j, a, 2 * px + py, c)
                p.start()
                passes.append(p)
        for j, (px, py) in enumerate(chips):
            for a in range(n):
                passed(j, a, 2 * px + py, 1 - c).wait_recv()
        for s in sends + passes:
            s.wait_send()
        for cp in copies:
            cp.wait()

    sems = pltpu.SemaphoreType.DMA((3, n))
    return pl.pallas_call(
        body, name=name, in_specs=[ANY] * n, out_specs=[ANY] * n,
        out_shape=[jax.ShapeDtypeStruct((N_CHIPS,) + a.shape, a.dtype) for a in arrs],
        scratch_shapes=[sems, sems, sems, sems, pltpu.SemaphoreType.DMA((n,)), pltpu.SemaphoreType.DMA((n,))],
    )(*arrs)


def _swap_halves(name, arrs):
    n = len(arrs)

    def body(*refs):
        ins, outs = refs[:n], refs[n:2 * n]
        send_sems, recv_sems = refs[2 * n:]
        x, y, c = _my_place()
        copies = []
        for a in range(n):
            half = arrs[a].shape[1] // 2
            cp = pltpu.make_async_remote_copy(src_ref=ins[a].at[:, pl.ds((1 - c) * half, half)], dst_ref=outs[a],
                                              send_sem=send_sems.at[a], recv_sem=recv_sems.at[a],
                                              device_id=(x, y, 1 - c), device_id_type=MESH)
            cp.start()
            copies.append(cp)
        for cp in copies:
            cp.wait()

    return pl.pallas_call(
        body, name=name, in_specs=[ANY] * n, out_specs=[ANY] * n,
        out_shape=[jax.ShapeDtypeStruct((a.shape[0], a.shape[1] // 2, a.shape[2]), a.dtype) for a in arrs],
        scratch_shapes=[pltpu.SemaphoreType.DMA((n,)), pltpu.SemaphoreType.DMA((n,))],
    )(*arrs)


def _sibling_exchange(name, arrs):
    n = len(arrs)

    def body(*refs):
        ins, outs = refs[:n], refs[n:2 * n]
        send_sems, recv_sems = refs[2 * n:]
        x, y, c = _my_place()
        copies = []
        for a in range(n):
            cp = pltpu.make_async_remote_copy(src_ref=ins[a], dst_ref=outs[a], send_sem=send_sems.at[a],
                                              recv_sem=recv_sems.at[a], device_id=(x, y, 1 - c), device_id_type=MESH)
            cp.start()
            copies.append(cp)
        for cp in copies:
            cp.wait()

    return pl.pallas_call(
        body, name=name, in_specs=[ANY] * n, out_specs=[ANY] * n,
        out_shape=[jax.ShapeDtypeStruct(a.shape, a.dtype) for a in arrs],
        scratch_shapes=[pltpu.SemaphoreType.DMA((n,)), pltpu.SemaphoreType.DMA((n,))],
    )(*arrs)


def _chip_all_to_all(name, arrs):
    n = len(arrs)

    def body(*refs):
        ins, outs = refs[:n], refs[n:2 * n]
        send_sems, recv_sems = refs[2 * n:]
        x, y, c = _my_place()
        sends = []
        for j, (px, py) in enumerate(_other_chips(x, y)):
            for a in range(n):
                s = pltpu.make_async_remote_copy(src_ref=ins[a].at[2 * px + py], dst_ref=outs[a].at[j],
                                                 send_sem=send_sems.at[j, a], recv_sem=recv_sems.at[j, a],
                                                 device_id=(px, py, c), device_id_type=MESH)
                s.start()
                sends.append(s)
        for s in sends:
            s.wait()

    return pl.pallas_call(
        body, name=name, in_specs=[ANY] * n, out_specs=[ANY] * n,
        out_shape=[jax.ShapeDtypeStruct((3,) + a.shape[1:], a.dtype) for a in arrs],
        scratch_shapes=[pltpu.SemaphoreType.DMA((3, n)), pltpu.SemaphoreType.DMA((3, n))],
    )(*arrs)


def _gather_all(name, arr):
    def body(in_ref, out_ref, send_sems, recv_sems, local_sem):
        x, y, c = _my_place()
        me = 4 * x + 2 * y + c
        mine = pltpu.make_async_copy(in_ref, out_ref.at[me], local_sem)
        mine.start()
        peers = []
        for k in range(1, N_DEV):
            px = (1 - x) if k & 4 else x
            py = (1 - y) if k & 2 else y
            pc = (1 - c) if k & 1 else c
            peers.append((px, py, pc))
        sends = []
        for k, peer in enumerate(peers):
            s = pltpu.make_async_remote_copy(src_ref=in_ref, dst_ref=out_ref.at[me], send_sem=send_sems.at[k],
                                             recv_sem=recv_sems.at[k], device_id=peer, device_id_type=MESH)
            s.start()
            sends.append(s)
        for k, (px, py, pc) in enumerate(peers):
            pltpu.make_async_remote_copy(src_ref=in_ref, dst_ref=out_ref.at[4 * px + 2 * py + pc],
                                         send_sem=send_sems.at[k], recv_sem=recv_sems.at[k], device_id=(px, py, pc),
                                         device_id_type=MESH).wait_recv()
        for s in sends:
            s.wait_send()
        mine.wait()

    return pl.pallas_call(
        body, name=name, in_specs=[ANY], out_specs=ANY, out_shape=jax.ShapeDtypeStruct((N_DEV,) + arr.shape, arr.dtype),
        scratch_shapes=[pltpu.SemaphoreType.DMA((N_DEV - 1,)), pltpu.SemaphoreType.DMA((N_DEV - 1,)),
                        pltpu.SemaphoreType.DMA],
    )(arr)


def _flat_rows(r):
    return _tile(r, 512, 16)


def _add_my_half(name, whole, other, my_c, out_dtype):
    n, half, wd = other.shape
    tr = _flat_rows(half)
    nb = half // tr

    def body(c_ref, a_ref, b_ref, o_ref):
        o_ref[...] = (a_ref[...] + b_ref[...]).astype(o_ref.dtype)

    spec = pl.BlockSpec((1, tr, wd), lambda s, i, c: (s, i, 0))
    grid_spec = pltpu.PrefetchScalarGridSpec(
        num_scalar_prefetch=1, grid=(n, nb),
        in_specs=[pl.BlockSpec((1, tr, wd), lambda s, i, c: (s, c[0] * nb + i, 0)), spec], out_specs=spec)
    return pl.pallas_call(body, name=name, grid_spec=grid_spec, out_shape=jax.ShapeDtypeStruct(other.shape, out_dtype),
                          compiler_params=_cparams(("arbitrary",) * 2))(my_c.astype(jnp.int32).reshape(1), whole, other)


def _sum_rows(name, a):
    n, r, wd = a.shape
    tr = _flat_rows(r)

    def body(a_ref, o_ref):
        acc = a_ref[0].astype(F32)
        for k in range(1, n):
            acc = acc + a_ref[k].astype(F32)
        o_ref[...] = acc

    return pl.pallas_call(body, name=name, grid=(r // tr,), in_specs=[pl.BlockSpec((n, tr, wd), lambda i: (0, i, 0))],
                          out_specs=pl.BlockSpec((tr, wd), lambda i: (i, 0)), out_shape=jax.ShapeDtypeStruct((r, wd), F32),
                          compiler_params=_cparams(("arbitrary",)))(a)


def _sum_own_received(name, sums, received, me):
    _, r, wd = sums.shape
    tr = _flat_rows(r)

    def body(me_ref, own_ref, rec_ref, o_ref):
        acc = own_ref[0].astype(F32)
        for k in range(3):
            acc = acc + rec_ref[k].astype(F32)
        o_ref[...] = acc

    grid_spec = pltpu.PrefetchScalarGridSpec(
        num_scalar_prefetch=1, grid=(r // tr,),
        in_specs=[pl.BlockSpec((1, tr, wd), lambda i, me: (me[0], i, 0)), pl.BlockSpec((3, tr, wd), lambda i, me: (0, i, 0))],
        out_specs=pl.BlockSpec((tr, wd), lambda i, me: (i, 0)))
    return pl.pallas_call(body, name=name, grid_spec=grid_spec, out_shape=jax.ShapeDtypeStruct((r, wd), F32),
                          compiler_params=_cparams(("arbitrary",)))(me.astype(jnp.int32).reshape(1), sums, received)


def _adamw_math(w, g, m, v):
    m = ADAM_B1 * m + (1.0 - ADAM_B1) * g
    v = ADAM_B2 * v + (1.0 - ADAM_B2) * (g * g)
    m_hat = m / (1.0 - ADAM_B1 ** ADAM_STEP)
    v_hat = v / (1.0 - ADAM_B2 ** ADAM_STEP)
    return -ADAM_LR * (m_hat / (jnp.sqrt(v_hat) + ADAM_EPS) + ADAM_WD * w), m, v


def _adamw(name, w, g, m, v):
    r, wd = w.shape
    tr = _tile(r, 256, 8)

    def body(w_ref, g_ref, m_ref, v_ref, d_ref, nm_ref, nv_ref):
        d_ref[...], nm_ref[...], nv_ref[...] = _adamw_math(w_ref[...], g_ref[...], m_ref[...], v_ref[...])

    spec = pl.BlockSpec((tr, wd), lambda i: (i, 0))
    return pl.pallas_call(body, name=name, grid=(r // tr,), in_specs=[spec] * 4, out_specs=[spec] * 3,
                          out_shape=[jax.ShapeDtypeStruct((r, wd), F32)] * 3,
                          compiler_params=_cparams(("arbitrary",)))(w, g, m, v)


def _adamw_halves(name, w, g_mine, g_other, m, v, my_c):
    r, wd = w.shape
    half = r // 2
    tr = _tile(half, 256, 8)
    nb = half // tr

    def body(c_ref, w_ref, gm_ref, go_ref, m_ref, v_ref, g_ref, d_ref, nm_ref, nv_ref):
        g = jnp.where(pl.program_id(0) // nb == c_ref[0], gm_ref[...], go_ref[...])
        g_ref[...] = g
        d_ref[...], nm_ref[...], nv_ref[...] = _adamw_math(w_ref[...], g, m_ref[...], v_ref[...])

    spec = pl.BlockSpec((tr, wd), lambda i, c: (i, 0))
    hspec = pl.BlockSpec((tr, wd), lambda i, c: (i % nb, 0))
    grid_spec = pltpu.PrefetchScalarGridSpec(num_scalar_prefetch=1, grid=(r // tr,),
                                             in_specs=[spec, hspec, hspec, spec, spec], out_specs=[spec] * 4)
    return pl.pallas_call(body, name=name, grid_spec=grid_spec, out_shape=[jax.ShapeDtypeStruct((r, wd), F32)] * 4,
                          compiler_params=_cparams(("arbitrary",)))(my_c.astype(jnp.int32).reshape(1), w, g_mine,
                                                                    g_other, m, v)


def _to_flat(parts, dtype, row_multiple):
    flat = jnp.concatenate([p.astype(dtype).reshape(-1) for p in parts])
    unit = FLAT_W * row_multiple
    total = -(-flat.shape[0] // unit) * unit
    return jnp.pad(flat, (0, total - flat.shape[0])).reshape(total // FLAT_W, FLAT_W)


def _from_flat(flat, shapes):
    flat = flat.reshape(-1)
    out, at = [], 0
    for shp in shapes:
        n = 1
        for d in shp:
            n *= d
        out.append(flat[at:at + n].reshape(shp))
        at += n
    return out


def _shard_piece(a, axis, s):
    n = a.shape[axis] // N_CHIPS
    return lax.slice_in_dim(a, s * n, (s + 1) * n, axis=axis)


def kernel(x, c, ctx, c_ctx, w_ada, b_ada, norm1_w, norm2_w, w_in, b_gate, mla_q_norm_a, mla_w_qb, mla_kv_norm_a, mla_w_kvb, mla_q_norm, mla_k_norm, gla_w_gk2, gla_b_gk, gla_o_norm, ret_decay, w_branch, w_out, w_ffn_in, w_dw, b_dw, w_ffn_out, loss_target, m_c_ctx, m_w_ada, m_b_ada, m_norm1_w, m_norm2_w, m_w_in, m_b_gate, m_mla_q_norm_a, m_mla_w_qb, m_mla_kv_norm_a, m_mla_w_kvb, m_mla_q_norm, m_mla_k_norm, m_gla_w_gk2, m_gla_b_gk, m_gla_o_norm, m_ret_decay, m_w_branch, m_w_out, m_w_ffn_in, m_w_dw, m_b_dw, m_w_ffn_out, v_c_ctx, v_w_ada, v_b_ada, v_norm1_w, v_norm2_w, v_w_in, v_b_gate, v_mla_q_norm_a, v_mla_w_qb, v_mla_kv_norm_a, v_mla_w_kvb, v_mla_q_norm, v_mla_k_norm, v_gla_w_gk2, v_gla_b_gk, v_gla_o_norm, v_ret_decay, v_w_branch, v_w_out, v_w_ffn_in, v_w_dw, v_b_dw, v_w_ffn_out):
    local = dict(c_ctx=c_ctx, w_ada=w_ada, b_ada=b_ada, norm1_w=norm1_w, norm2_w=norm2_w, w_in=w_in, b_gate=b_gate,
                 mla_q_norm_a=mla_q_norm_a, mla_w_qb=mla_w_qb, mla_kv_norm_a=mla_kv_norm_a, mla_w_kvb=mla_w_kvb,
                 mla_q_norm=mla_q_norm, mla_k_norm=mla_k_norm, gla_w_gk2=gla_w_gk2, gla_b_gk=gla_b_gk,
                 gla_o_norm=gla_o_norm, ret_decay=ret_decay, w_branch=w_branch, w_out=w_out, w_ffn_in=w_ffn_in,
                 w_dw=w_dw, b_dw=b_dw, w_ffn_out=w_ffn_out)
    mom_m = dict(c_ctx=m_c_ctx, w_ada=m_w_ada, b_ada=m_b_ada, norm1_w=m_norm1_w, norm2_w=m_norm2_w, w_in=m_w_in,
                 b_gate=m_b_gate, mla_q_norm_a=m_mla_q_norm_a, mla_w_qb=m_mla_w_qb, mla_kv_norm_a=m_mla_kv_norm_a,
                 mla_w_kvb=m_mla_w_kvb, mla_q_norm=m_mla_q_norm, mla_k_norm=m_mla_k_norm, gla_w_gk2=m_gla_w_gk2,
                 gla_b_gk=m_gla_b_gk, gla_o_norm=m_gla_o_norm, ret_decay=m_ret_decay, w_branch=m_w_branch,
                 w_out=m_w_out, w_ffn_in=m_w_ffn_in, w_dw=m_w_dw, b_dw=m_b_dw, w_ffn_out=m_w_ffn_out)
    mom_v = dict(c_ctx=v_c_ctx, w_ada=v_w_ada, b_ada=v_b_ada, norm1_w=v_norm1_w, norm2_w=v_norm2_w, w_in=v_w_in,
                 b_gate=v_b_gate, mla_q_norm_a=v_mla_q_norm_a, mla_w_qb=v_mla_w_qb, mla_kv_norm_a=v_mla_kv_norm_a,
                 mla_w_kvb=v_mla_w_kvb, mla_q_norm=v_mla_q_norm, mla_k_norm=v_mla_k_norm, gla_w_gk2=v_gla_w_gk2,
                 gla_b_gk=v_gla_b_gk, gla_o_norm=v_gla_o_norm, ret_decay=v_ret_decay, w_branch=v_w_branch,
                 w_out=v_w_out, w_ffn_in=v_w_ffn_in, w_dw=v_w_dw, b_dw=v_b_dw, w_ffn_out=v_w_ffn_out)
    axis_of = dict(SHARDED)
    small_narrow = tuple(n for n, _ in SHARDED if n not in BIG and n not in SHARDED_F32)
    small_sharded = small_narrow + SHARDED_F32
    kinds = ('grad', 'delta', 'new_m', 'new_v')
    my_x, my_y, my_c = _my_place()
    my_chip = 2 * my_x + my_y

    gathered = _gather_chips('gather_weights', [local[n].astype(MXU_DTYPE) for n in BIG] + [
        _to_flat([local[n] for n in small_narrow], MXU_DTYPE, 32), _to_flat([local[n] for n in SHARDED_F32], F32, 16)])
    big = []
    for l in range(DEPTH):
        full_l = {n: jnp.concatenate([g[s, l] for s in range(N_CHIPS)], axis=BIG_AXIS[n]) for n, g in zip(BIG, gathered)}
        big.append({k: v.astype(F32) for k, v in _big_layer_weights(full_l).items()})
    small = {n: local[n] for n in REPLICATED}
    for names, flat4 in ((small_narrow, gathered[-2]), (SHARDED_F32, gathered[-1])):
        pieces = [_from_flat(flat4[s], [local[n].shape for n in names]) for s in range(N_CHIPS)]
        for k, n in enumerate(names):
            small[n] = jnp.concatenate([pieces[s][k] for s in range(N_CHIPS)], axis=axis_of[n]).astype(F32)

    loss_local, (grad_big, grad_small, grad_x) = jax.value_and_grad(_local_loss, argnums=(0, 1, 2))(
        big, small, x, c, ctx, loss_target)
    loss = lax.psum(loss_local, ('x', 'y', 'c'))

    per_layer = [_big_layer_grads(grad_big[l]) for l in range(DEPTH)]
    stacks = []
    for n in BIG:
        st = jnp.stack([jnp.stack([_shard_piece(per_layer[l][n], BIG_AXIS[n], s) for l in range(DEPTH)])
                        for s in range(N_CHIPS)])
        stacks.append(st.reshape(N_CHIPS, -1, st.shape[-1]))
    stacks.append(jnp.stack([_to_flat([_shard_piece(grad_small[n], axis_of[n], s) for n in small_sharded], F32, 64)
                             for s in range(N_CHIPS)]))
    labels = BIG + ('small',)
    swapped = _swap_halves('grad_swap_halves', stacks)
    chip_sums = [_add_my_half('grad_add_' + n, st, sw, my_c, MXU_DTYPE) for n, st, sw in zip(labels, stacks, swapped)]
    received = _chip_all_to_all('grad_all_to_all', chip_sums)
    mine = [_sum_own_received('grad_sum_' + n, cs, rc, my_chip) for n, cs, rc in zip(labels, chip_sums, received)]
    other = _sibling_exchange('grad_share_result', mine)

    g_rep = _sum_rows('grad_sum_replicated', _gather_all('grad_gather_replicated',
                                                          _to_flat([grad_small[n] for n in REPLICATED], F32, 8)))

    results = {}
    as_rows = lambda t: t.reshape(-1, t.shape[-1])
    for k, n in enumerate(BIG):
        res = _adamw_halves('adamw_' + n, as_rows(local[n]), mine[k], other[k], as_rows(mom_m[n]), as_rows(mom_v[n]), my_c)
        for kind, val in zip(kinds, res):
            results[kind, n] = val.reshape(local[n].shape)
    flat_small = lambda d: _to_flat([d[n] for n in small_sharded], F32, 64)
    res = _adamw_halves('adamw_small', flat_small(local), mine[-1], other[-1], flat_small(mom_m), flat_small(mom_v), my_c)
    for kind, flat in zip(kinds, res):
        for n, val in zip(small_sharded, _from_flat(flat, [local[n].shape for n in small_sharded])):
            results[kind, n] = val
    flat_rep = lambda d: _to_flat([d[n] for n in REPLICATED], F32, 8)
    upd_r = _adamw('adamw_replicated', flat_rep(local), g_rep, flat_rep(mom_m), flat_rep(mom_v))
    for kind, flat in zip(kinds, (g_rep,) + tuple(upd_r)):
        for n, val in zip(REPLICATED, _from_flat(flat, [local[n].shape for n in REPLICATED])):
            results[kind, n] = val
    out = [loss, grad_x]
    for kind in ('grad', 'delta', 'new_m', 'new_v'):
        out += [results[kind, n] for n in WEIGHT_ORDER]
    return tuple(out)
```
